```python
import math
import jax
import jax.numpy as jnp
from jax import lax
import numpy as np

D_MODEL = 2048
BATCH = 8
SEQ = 2048
DEPTH = 1

CTX_LEN = 256
GRID_W = 64

MLSTM_HEADS = 4
MLSTM_HEAD_DIM = 256
MLSTM_WIDTH = MLSTM_HEADS * MLSTM_HEAD_DIM
MLSTM_CHUNK = 64

HYENA_WIDTH = D_MODEL // 2
HYENA_ORDER = 2
HYENA_EMB = 33
HYENA_FILTER_HIDDEN = 64
HYENA_DECAY_TARGET = 1e-2
HYENA_SHORT_DECAY_PCT = 0.3
HYENA_LONG_DECAY_PCT = 1.5

N_EXPERTS = 32
TOP_K = 4
D_EXPERT = D_MODEL
SWIGLU_LIMIT = 7.0
SWIGLU_ALPHA = 1.702
MOE_BLOCK = 256

LN_EPS = 1e-5
DEEPNORM_ALPHA = (2.0 * DEPTH) ** 0.25
DEEPNORM_BETA = (8.0 * DEPTH) ** -0.25

IN_QK = 0
IN_V = IN_QK + MLSTM_WIDTH
IN_GATES = IN_V + MLSTM_WIDTH
IN_O = IN_GATES + 4 * MLSTM_HEADS
IN_HYENA = IN_O + MLSTM_WIDTH
IN_BRANCH_GATES = IN_HYENA + (HYENA_ORDER + 1) * HYENA_WIDTH
N_IN = IN_BRANCH_GATES + 2 * D_MODEL

kernel_name = 'hybrid_mlstm_hyena_moe_diffusion_block'

F32 = jnp.float32


def layer_norm(x, g, b):
    xf = x.astype(F32)
    mu = xf.mean(-1, keepdims=True)
    var = jnp.square(xf - mu).mean(-1, keepdims=True)
    return ((xf - mu) * lax.rsqrt(var + LN_EPS) * g.astype(F32) + b.astype(F32)).astype(x.dtype)


def grid_dwconv(u, w, b, rows, cols):
    bsz, length, ch = u.shape
    img = u.reshape(bsz, rows, cols, ch)
    out = lax.conv_general_dilated(img, w[:, :, None, :].astype(u.dtype), window_strides=(1, 1),
                                   padding=((1, 1), (1, 1)), dimension_numbers=('NHWC', 'HWIO', 'NHWC'),
                                   feature_group_count=ch)
    return out.reshape(bsz, length, ch) + b.astype(u.dtype)


def mlstm_zero_state(bsz):
    return (jnp.zeros((bsz, MLSTM_HEADS, MLSTM_HEAD_DIM, MLSTM_HEAD_DIM), F32),
            jnp.zeros((bsz, MLSTM_HEADS, MLSTM_HEAD_DIM), F32),
            jnp.zeros((bsz, MLSTM_HEADS), F32))


def mlstm_prepare(z, rows, cols, conv_w, conv_b, w_qh, w_kh):
    bsz, length, _ = z.shape
    u = jax.nn.silu(grid_dwconv(z[..., IN_QK:IN_V], conv_w, conv_b, rows, cols))
    u = u.astype(F32).reshape(bsz, length, MLSTM_HEADS, MLSTM_HEAD_DIM)
    q = jnp.einsum('blhd,hde->bhle', u, w_qh.astype(F32))
    k = jnp.einsum('blhd,hde->bhle', u, w_kh.astype(F32)) * MLSTM_HEAD_DIM ** -0.5
    v = z[..., IN_V:IN_GATES].astype(F32).reshape(bsz, length, MLSTM_HEADS, MLSTM_HEAD_DIM).transpose(0, 2, 1, 3)
    gates = z[..., IN_GATES:IN_O].astype(F32).reshape(bsz, length, 4, MLSTM_HEADS).transpose(2, 0, 3, 1)
    fwd = (q, k, v, gates[0], jax.nn.log_sigmoid(gates[1]))
    bwd = (q[:, :, ::-1], k[:, :, ::-1], v[:, :, ::-1], gates[2][..., ::-1], jax.nn.log_sigmoid(gates[3])[..., ::-1])
    return fwd, bwd


def mlstm_chunkwise(q, k, v, log_i, log_f, state):
    bsz, nh, length, dh = q.shape
    nc = length // MLSTM_CHUNK

    def chunks(a):
        return jnp.moveaxis(a.reshape(bsz, nh, nc, MLSTM_CHUNK, *a.shape[3:]), 2, 0)

    causal = jnp.tril(jnp.ones((MLSTM_CHUNK, MLSTM_CHUNK), bool))

    def step(carry, xs):
        c_mat, n_vec, m = carry
        qc, kc, vc, li, lf = xs
        b = jnp.cumsum(lf, axis=-1)
        inter = b + m[..., None]
        dmat = jnp.where(causal, b[..., :, None] - b[..., None, :] + li[..., None, :], -jnp.inf)
        m_t = jnp.maximum(inter, dmat.max(-1))
        wts = jnp.exp(dmat - m_t[..., None])
        s_inter = jnp.exp(inter - m_t)
        scores = jnp.einsum('bhtd,bhsd->bhts', qc, kc) * wts
        num = s_inter[..., None] * jnp.einsum('bhvd,bhtd->bhtv', c_mat, qc) + jnp.einsum('bhts,bhsv->bhtv', scores, vc)
        den = s_inter * jnp.einsum('bhd,bhtd->bht', n_vec, qc) + scores.sum(-1)
        h = num / jnp.maximum(jnp.abs(den), jnp.exp(-m_t))[..., None]
        b_end = b[..., -1]
        g = b_end[..., None] - b + li
        m_new = jnp.maximum(b_end + m, g.max(-1))
        decay = jnp.exp(b_end + m - m_new)
        w = jnp.exp(g - m_new[..., None])
        c_mat = decay[..., None, None] * c_mat + jnp.einsum('bhsv,bhsd->bhvd', vc * w[..., None], kc)
        n_vec = decay[..., None] * n_vec + jnp.einsum('bhs,bhsd->bhd', w, kc)
        return (c_mat, n_vec, m_new), h

    state, h = lax.scan(step, state, (chunks(q), chunks(k), chunks(v), chunks(log_i), chunks(log_f)))
    return jnp.moveaxis(h, 0, 2).reshape(bsz, nh, length, dh), state


def heads_to_tokens(h):
    bsz, nh, length, dh = h.shape
    return h.transpose(0, 2, 1, 3).reshape(bsz, length, nh * dh)


def hyena_filter_spectra(length, w_f1, b_f1, w_fh, b_fh, freq, w_fout):
    t = jnp.linspace(0.0, 1.0, length, dtype=F32)[:, None]
    bands = (HYENA_EMB - 1) // 2
    f = jnp.linspace(1e-4, bands - 1, bands, dtype=F32)[None, :]
    ang = 2.0 * math.pi * jnp.arange(length, dtype=F32)[:, None] * f / length
    z = jnp.concatenate([t, jnp.cos(ang), -jnp.sin(ang)], axis=-1)
    freq = freq.astype(F32)
    a = jnp.sin(freq[0] * (z @ w_f1.astype(F32) + b_f1.astype(F32)))
    a = jnp.sin(freq[1] * (a @ w_fh[0].astype(F32) + b_fh[0].astype(F32)))
    a = jnp.sin(freq[2] * (a @ w_fh[1].astype(F32) + b_fh[1].astype(F32)))
    h = (a @ w_fout.astype(F32)).reshape(length, HYENA_ORDER, 2, HYENA_WIDTH)
    max_decay = math.log(HYENA_DECAY_TARGET) / HYENA_SHORT_DECAY_PCT
    min_decay = math.log(HYENA_DECAY_TARGET) / HYENA_LONG_DECAY_PCT
    deltas = jnp.abs(jnp.linspace(min_decay, max_decay, HYENA_WIDTH, dtype=F32))
    h = h * jnp.exp(-t * deltas)[:, None, None, :]
    h_fwd, h_bwd = h[:, :, 0], h[:, :, 1]
    taps = jnp.concatenate([h_fwd, jnp.zeros((1, HYENA_ORDER, HYENA_WIDTH), F32), h_bwd[1:][::-1]], axis=0)
    return jnp.fft.rfft(taps, axis=0)


def hyena_operator(z, rows, cols, conv_w, conv_b, w_f1, b_f1, w_fh, b_fh, freq, w_fout, skip):
    u = grid_dwconv(z[..., IN_HYENA:IN_BRANCH_GATES], conv_w, conv_b, rows, cols).astype(F32)
    x1, x2, v = jnp.split(u, 3, axis=-1)
    length = u.shape[1]
    spec = hyena_filter_spectra(length, w_f1, b_f1, w_fh, b_fh, freq, w_fout)
    skip = skip.astype(F32)
    for n, gate in enumerate((x1, x2)):
        vf = jnp.fft.rfft(v, n=2 * length, axis=1)
        y = jnp.fft.irfft(vf * spec[None, :, n], n=2 * length, axis=1)[:, :length]
        v = gate * (y + v * skip[n])
    return v


def merge_branches(z, h_mlstm, h_hyena, w_proj_a, w_proj_h, w_out):
    o = jax.nn.sigmoid(z[..., IN_O:IN_HYENA])
    y_a = (o * h_mlstm.astype(z.dtype)) @ w_proj_a
    y_h = h_hyena.astype(z.dtype) @ w_proj_h
    g_a, g_h = jnp.split(jax.nn.sigmoid(z[..., IN_BRANCH_GATES:]), 2, axis=-1)
    return (g_a * y_a + g_h * y_h) @ w_out


def moe_ffn(h, w_router, b_router, w_gate, b_gate, w_up, b_up, w_down, b_down):
    n_tok, dm = h.shape
    logits = (h @ w_router + b_router).astype(F32)
    top_logit, top_e = lax.top_k(logits, TOP_K)
    weights = jax.nn.softmax(top_logit, axis=-1)
    tk = n_tok * TOP_K
    flat_e = top_e.reshape(tk)
    flat_t = jnp.repeat(jnp.arange(n_tok, dtype=jnp.int32), TOP_K)
    flat_w = weights.reshape(tk)
    order = jnp.argsort(flat_e)
    se, st, sw = flat_e[order], flat_t[order], flat_w[order]
    counts = jnp.bincount(flat_e, length=N_EXPERTS)
    padded = (counts + MOE_BLOCK - 1) // MOE_BLOCK * MOE_BLOCK
    start = jnp.cumsum(counts) - counts
    pend = jnp.cumsum(padded)
    pstart = pend - padded
    dest = pstart[se] + jnp.arange(tk) - start[se]
    n_blocks = -(-(tk + N_EXPERTS * (MOE_BLOCK - 1)) // MOE_BLOCK)
    rows_total = n_blocks * MOE_BLOCK
    row_tok = jnp.zeros((rows_total,), jnp.int32).at[dest].set(st)
    row_w = jnp.zeros((rows_total,), F32).at[dest].set(sw)
    block_e = jnp.minimum(jnp.searchsorted(pend, jnp.arange(n_blocks) * MOE_BLOCK, side='right'), N_EXPERTS - 1)
    xb = h[row_tok].reshape(n_blocks, MOE_BLOCK, dm)

    def expert_block(args):
        xe, e = args
        g = jnp.minimum(xe @ w_gate[e] + b_gate[e], SWIGLU_LIMIT)
        u = jnp.clip(xe @ w_up[e] + b_up[e], -SWIGLU_LIMIT, SWIGLU_LIMIT)
        act = g * jax.nn.sigmoid(SWIGLU_ALPHA * g) * (u + 1.0)
        return act @ w_down[e] + b_down[e]

    yb = lax.map(expert_block, (xb, block_e)).reshape(rows_total, dm)
    return jax.ops.segment_sum(yb * row_w[:, None].astype(yb.dtype), row_tok, num_segments=n_tok)


def setup_inputs(seed: int = 0) -> dict:
    key = jax.random.key(seed)
    ks = iter(jax.random.split(key, 48))

    def nrm(shape, scale):
        return jax.random.normal(next(ks), shape, F32) * scale

    dm, nl, nh, fh = D_MODEL, DEPTH, MLSTM_HEADS, HYENA_FILTER_HIDDEN
    b_in = nrm((nl, N_IN), 0.02)
    f_bias = jnp.linspace(3.0, 6.0, nh, dtype=F32)
    b_in = b_in.at[:, IN_GATES + nh:IN_GATES + 2 * nh].add(f_bias).at[:, IN_GATES + 3 * nh:IN_GATES + 4 * nh].add(f_bias)
    return {
        'x': nrm((BATCH, SEQ, dm), 1.0),
        'c': nrm((BATCH, dm), 1.0),
        'ctx': nrm((BATCH, CTX_LEN, dm), 1.0),
        'c_ctx': nrm((dm,), 1.0),
        'ln_in_g': 1.0 + nrm((dm,), 0.02),
        'ln_in_b': nrm((dm,), 0.02),
        'w_mod': nrm((nl, dm, 6 * dm), 0.5 * dm ** -0.5),
        'b_mod': nrm((nl, 6 * dm), 0.02),
        'w_in': nrm((nl, dm, N_IN), dm ** -0.5),
        'b_in': b_in,
        'mlstm_conv_w': nrm((nl, 3, 3, MLSTM_WIDTH), 1.0 / 3.0),
        'mlstm_conv_b': nrm((nl, MLSTM_WIDTH), 0.02),
        'w_qh': nrm((nl, nh, MLSTM_HEAD_DIM, MLSTM_HEAD_DIM), MLSTM_HEAD_DIM ** -0.5),
        'w_kh': nrm((nl, nh, MLSTM_HEAD_DIM, MLSTM_HEAD_DIM), MLSTM_HEAD_DIM ** -0.5),
        'hyena_conv_w': nrm((nl, 3, 3, (HYENA_ORDER + 1) * HYENA_WIDTH), 1.0 / 3.0),
        'hyena_conv_b': nrm((nl, (HYENA_ORDER + 1) * HYENA_WIDTH), 0.02),
        'filt_w1': nrm((nl, HYENA_EMB, fh), HYENA_EMB ** -0.5),
        'filt_b1': nrm((nl, fh), 0.1),
        'filt_wh': nrm((nl, 2, fh, fh), fh ** -0.5),
        'filt_bh': nrm((nl, 2, fh), 0.1),
        'filt_freq': 1.0 + nrm((nl, 3, fh), 0.02),
        'filt_wout': nrm((nl, fh, HYENA_ORDER * 2 * HYENA_WIDTH), 0.05 * fh ** -0.5),
        'hyena_skip': nrm((nl, HYENA_ORDER, HYENA_WIDTH), 0.5),
        'w_proj_a': nrm((nl, MLSTM_WIDTH, dm), MLSTM_WIDTH ** -0.5),
        'w_proj_h': nrm((nl, HYENA_WIDTH, dm), HYENA_WIDTH ** -0.5),
        'w_out': nrm((nl, dm, dm), DEEPNORM_BETA * dm ** -0.5),
        'ln1_g': 1.0 + nrm((nl, dm), 0.02),
        'ln1_b': nrm((nl, dm), 0.02),
        'w_router': nrm((nl, dm, N_EXPERTS), dm ** -0.5),
        'b_router': nrm((nl, N_EXPERTS), 0.01),
        'w_gate': nrm((nl, N_EXPERTS, dm, D_EXPERT), dm ** -0.5),
        'b_gate': nrm((nl, N_EXPERTS, D_EXPERT), 0.02),
        'w_up': nrm((nl, N_EXPERTS, dm, D_EXPERT), dm ** -0.5),
        'b_up': nrm((nl, N_EXPERTS, D_EXPERT), 0.02),
        'w_down': nrm((nl, N_EXPERTS, D_EXPERT, dm), DEEPNORM_BETA * D_EXPERT ** -0.5),
        'b_down': nrm((nl, N_EXPERTS, dm), 0.02),
        'ln2_g': 1.0 + nrm((nl, dm), 0.02),
        'ln2_b': nrm((nl, dm), 0.02),
    }


def reference(x, c, ctx, c_ctx, ln_in_g, ln_in_b, w_mod, b_mod, w_in, b_in, mlstm_conv_w, mlstm_conv_b,
              w_qh, w_kh, hyena_conv_w, hyena_conv_b, filt_w1, filt_b1, filt_wh, filt_bh, filt_freq, filt_wout,
              hyena_skip, w_proj_a, w_proj_h, w_out, ln1_g, ln1_b, w_router, b_router, w_gate, b_gate,
              w_up, b_up, w_down, b_down, ln2_g, ln2_b):
    bsz, seq, dm = x.shape
    rows = seq // GRID_W
    ctx_len = ctx.shape[1]
    x = layer_norm(x, ln_in_g, ln_in_b)
    xc = layer_norm(ctx, ln_in_g, ln_in_b)
    s_lat = jax.nn.silu(c)
    s_ctx = jax.nn.silu(c_ctx)[None]
    for l in range(DEPTH):
        last = l == DEPTH - 1
        sh1, sc1, g1, sh2, sc2, g2 = jnp.split((s_lat @ w_mod[l] + b_mod[l])[:, None, :], 6, axis=-1)
        csh1, csc1, cg1, csh2, csc2, cg2 = jnp.split((s_ctx @ w_mod[l] + b_mod[l])[:, None, :], 6, axis=-1)

        hx = x * (1.0 + sc1) + sh1
        hc = xc * (1.0 + csc1) + csh1
        zx = hx @ w_in[l] + b_in[l]
        n_ctx_cols = IN_O if last else N_IN
        zc = hc @ w_in[l][:, :n_ctx_cols] + b_in[l][:n_ctx_cols]
        mlstm_p = (mlstm_conv_w[l], mlstm_conv_b[l], w_qh[l], w_kh[l])
        hyena_p = (hyena_conv_w[l], hyena_conv_b[l], filt_w1[l], filt_b1[l], filt_wh[l], filt_bh[l],
                   filt_freq[l], filt_wout[l], hyena_skip[l])
        zero = mlstm_zero_state(bsz)
        ctx_fwd, ctx_bwd = mlstm_prepare(zc, 1, ctx_len, *mlstm_p)
        hcf, state_f = mlstm_chunkwise(*ctx_fwd, zero)
        hcb, state_b = mlstm_chunkwise(*ctx_bwd, zero)
        lat_fwd, lat_bwd = mlstm_prepare(zx, rows, GRID_W, *mlstm_p)
        hxf, _ = mlstm_chunkwise(*lat_fwd, state_f)
        hxb, _ = mlstm_chunkwise(*lat_bwd, state_b)
        h_mlstm_x = heads_to_tokens(hxf + hxb[:, :, ::-1])
        h_hyena_x = hyena_operator(zx, rows, GRID_W, *hyena_p)
        mix_x = merge_branches(zx, h_mlstm_x, h_hyena_x, w_proj_a[l], w_proj_h[l], w_out[l])
        x = layer_norm(DEEPNORM_ALPHA * x + g1 * mix_x, ln1_g[l], ln1_b[l])
        if not last:
            h_mlstm_c = heads_to_tokens(hcf + hcb[:, :, ::-1])
            h_hyena_c = hyena_operator(zc, 1, ctx_len, *hyena_p)
            mix_c = merge_branches(zc, h_mlstm_c, h_hyena_c, w_proj_a[l], w_proj_h[l], w_out[l])
            xc = layer_norm(DEEPNORM_ALPHA * xc + cg1 * mix_c, ln1_g[l], ln1_b[l])

        tokens = (x * (1.0 + sc2) + sh2).reshape(bsz * seq, dm)
        if not last:
            tokens = jnp.concatenate([tokens, (xc * (1.0 + csc2) + csh2).reshape(bsz * ctx_len, dm)], axis=0)
        y = moe_ffn(tokens, w_router[l], b_router[l], w_gate[l], b_gate[l], w_up[l], b_up[l], w_down[l], b_down[l])
        x = layer_norm(DEEPNORM_ALPHA * x + g2 * y[:bsz * seq].reshape(bsz, seq, dm), ln2_g[l], ln2_b[l])
        if not last:
            xc = layer_norm(DEEPNORM_ALPHA * xc + cg2 * y[bsz * seq:].reshape(bsz, ctx_len, dm), ln2_g[l], ln2_b[l])
    return x
```

```python
import functools
import math

import numpy as np
import jax
import jax.numpy as jnp
from jax import lax
from jax.experimental import pallas as pl
from jax.experimental.pallas import tpu as pltpu

F32 = jnp.float32
BF16 = jnp.bfloat16
I32 = jnp.int32
HIGHEST = lax.Precision.HIGHEST

D_MODEL = 2048
GRID_W = 64
HEADS = 4
HEAD_DIM = 256
MLSTM_WIDTH = HEADS * HEAD_DIM
HYENA_WIDTH = D_MODEL // 2
HYENA_EMB = 33
HYENA_DECAY_TARGET = 1e-2
HYENA_SHORT_DECAY_PCT = 0.3
HYENA_LONG_DECAY_PCT = 1.5
N_EXPERTS = 32
TOP_K = 4
SWIGLU_LIMIT = 7.0
SWIGLU_ALPHA = 1.702
LN_EPS = 1e-5
DEPTH = 1
DEEPNORM_ALPHA = (2.0 * DEPTH) ** 0.25

IN_V = MLSTM_WIDTH
IN_GATES = 2 * MLSTM_WIDTH
IN_O = IN_GATES + 4 * HEADS
Z_QK, Z_V, Z_O, Z_HY, Z_BG = 0, 1024, 2048, 3072, 6144
Z_COLS = 10240

LANES = 128
MXU = 256
CHUNK = 256
MOE_BLOCK = 256
FF_CHUNK = 1024
MIB = 1024 * 1024


def _cparams(semantics, vmem_mib):
    return pltpu.CompilerParams(dimension_semantics=semantics, vmem_limit_bytes=vmem_mib * MIB)


def _const_spec(shape):
    nd = len(shape)
    return pl.BlockSpec(shape, lambda *_: (0,) * nd, pipeline_mode=pl.Buffered(1))


def _layer_norm(x, g, b):
    mu = jnp.mean(x, axis=-1, keepdims=True)
    xc = x - mu
    var = jnp.mean(xc * xc, axis=-1, keepdims=True)
    return xc * lax.rsqrt(var + LN_EPS) * g + b


def _sigmoid(x):
    return 1.0 / (1.0 + jnp.exp(-x))


def _silu(x):
    return x * _sigmoid(x)


def _log_sigmoid(x):
    return jnp.minimum(x, 0.0) - jnp.log(1.0 + jnp.exp(-jnp.abs(x)))


def _mod_body(c_ref, w_ref, b_ref, o_ref):
    s = _silu(c_ref[...])
    o_ref[...] = jnp.dot(s.astype(BF16), w_ref[...].astype(BF16), preferred_element_type=F32) + b_ref[...]


def _mod(cond, w, b):
    rows, dm = cond.shape
    n = w.shape[1]
    tn = 1024
    return pl.pallas_call(
        _mod_body,
        grid=(n // tn,),
        in_specs=[pl.BlockSpec((rows, dm), lambda j: (0, 0)),
                  pl.BlockSpec((dm, tn), lambda j: (0, j)),
                  pl.BlockSpec((1, tn), lambda j: (0, j))],
        out_specs=pl.BlockSpec((rows, tn), lambda j: (0, j)),
        out_shape=jax.ShapeDtypeStruct((rows, n), F32),
        compiler_params=_cparams(("arbitrary",), 40),
        name="mod",
    )(cond, w, b)


def _in_proj_body(x_ref, lng_ref, lnb_ref, sc_ref, sh_ref, w_ref, b_ref, wg_ref, bg_ref,
                  z_ref, g_ref, hx_s, *, tm):
    @pl.when(pl.program_id(1) == 0)
    def _():
        def rows(r, carry):
            sl = pl.ds(pl.multiple_of(r * 128, 128), 128)
            xn = _layer_norm(x_ref[sl, :], lng_ref[...], lnb_ref[...])
            hx_s[sl, :] = (xn * (1.0 + sc_ref[0]) + sh_ref[0]).astype(BF16)
            return carry
        lax.fori_loop(0, tm // 128, rows, 0)
        g_ref[...] = jnp.dot(hx_s[...], wg_ref[...], preferred_element_type=F32) + bg_ref[...]

    z_ref[...] = jnp.dot(hx_s[...], w_ref[...], preferred_element_type=F32) + b_ref[...]


def _in_proj(x2d, ln_g, ln_b, scale, shift, w, b, w_gates, b_gates, rows_per_mod):
    t, dm = x2d.shape
    n = w.shape[1]
    tm, tn = min(1024, t), 512
    return pl.pallas_call(
        functools.partial(_in_proj_body, tm=tm),
        grid=(t // tm, n // tn),
        in_specs=[pl.BlockSpec((tm, dm), lambda i, j: (i, 0)),
                  pl.BlockSpec((1, dm), lambda i, j: (0, 0)),
                  pl.BlockSpec((1, dm), lambda i, j: (0, 0)),
                  pl.BlockSpec((1, 1, dm), lambda i, j: (i * tm // rows_per_mod, 0, 0)),
                  pl.BlockSpec((1, 1, dm), lambda i, j: (i * tm // rows_per_mod, 0, 0)),
                  pl.BlockSpec((dm, tn), lambda i, j: (0, j)),
                  pl.BlockSpec((1, tn), lambda i, j: (0, j)),
                  pl.BlockSpec((dm, LANES), lambda i, j: (0, 0)),
                  pl.BlockSpec((1, LANES), lambda i, j: (0, 0))],
        out_specs=[pl.BlockSpec((tm, tn), lambda i, j: (i, j)),
                   pl.BlockSpec((tm, LANES), lambda i, j: (i, 0))],
        out_shape=[jax.ShapeDtypeStruct((t, n), F32), jax.ShapeDtypeStruct((t, LANES), F32)],
        scratch_shapes=[pltpu.VMEM((tm, dm), BF16)],
        compiler_params=_cparams(("parallel", "arbitrary"), 48),
        name="in_proj",
    )(x2d, ln_g, ln_b, scale, shift, w, b, w_gates, b_gates)


def _dwconv(u, w9, bias, width, single_row):
    length, ch = u.shape
    col = lax.broadcasted_iota(I32, (length, ch), 0) % width
    if not single_row:
        zpad = jnp.zeros((width, ch), F32)
        up = jnp.concatenate([zpad, u[:length - width]], axis=0)
        dn = jnp.concatenate([u[width:], zpad], axis=0)
    out = None
    for dc in (-1, 0, 1):
        a = u * w9[4 + dc:5 + dc]
        if not single_row:
            a = a + up * w9[1 + dc:2 + dc] + dn * w9[7 + dc:8 + dc]
        if dc == -1:
            a = jnp.where(col == 0, 0.0, pltpu.roll(a, 1, 0))
        elif dc == 1:
            a = jnp.where(col == width - 1, 0.0, pltpu.roll(a, length - 1, 0))
        out = a if out is None else out + a
    return out + bias


def _mlstm_body(zqk_ref, zv_ref, zo_ref, cqk_ref, cv_ref, gt_ref, cw_ref, cb_ref, wq_ref, wk_ref,
                o_ref, q_s, k_s, v_s, hf_s, hb_s, r_s, ct_s, *, ctx_len, seq):
    n_chunks = (ctx_len + seq) // CHUNK
    total = ctx_len + seq
    cw = cw_ref[...]
    cb = cb_ref[...]
    wq = wq_ref[0].astype(BF16)
    wk = wk_ref[0].astype(BF16)

    def project(u, off, n):
        ub = u.astype(BF16)
        q_s[off:off + n, :] = jnp.dot(ub, wq, preferred_element_type=F32).astype(BF16)
        k_s[off:off + n, :] = (jnp.dot(ub, wk, preferred_element_type=F32) * HEAD_DIM ** -0.5).astype(BF16)

    project(_silu(_dwconv(cqk_ref[...], cw, cb, ctx_len, True)), 0, ctx_len)
    project(_silu(_dwconv(zqk_ref[...], cw, cb, GRID_W, False)), ctx_len, seq)
    v_s[0:ctx_len, :] = cv_ref[...]
    v_s[ctx_len:total, :] = zv_ref[...]

    gt = gt_ref[0, 0]
    lf = _log_sigmoid(gt)
    pos = lax.broadcasted_iota(I32, (4, total), 1) % CHUNK
    pre, suf = lf, lf
    s = 1
    while s < CHUNK:
        pre = pre + jnp.where(pos >= s, pltpu.roll(pre, s, 1), 0.0)
        suf = suf + jnp.where(pos < CHUNK - s, pltpu.roll(suf, total - s, 1), 0.0)
        s *= 2
    r_s[...] = jnp.concatenate([pre[1:2], gt[0:1], suf[3:4], gt[2:3], jnp.zeros((4, total), F32)], axis=0)

    ct_s[...] = jnp.zeros_like(ct_s)
    row_i = lax.broadcasted_iota(I32, (CHUNK, CHUNK), 0)
    col_i = lax.broadcasted_iota(I32, (CHUNK, CHUNK), 1)
    masks = (row_i >= col_i, row_i <= col_i)
    zfill = jnp.zeros((CHUNK - 8, CHUNK), F32)

    def chunk_step(c, d, n_vec, m):
        off = c * CHUNK if isinstance(c, int) else pl.multiple_of(c * CHUNK, CHUNK)
        rows = r_s[:, pl.ds(off, CHUNK)]
        cols = jnp.concatenate([rows, zfill], axis=0).T
        b_row, li_row = rows[2 * d:2 * d + 1], rows[2 * d + 1:2 * d + 2]
        b_col, li_col = cols[:, 2 * d:2 * d + 1], cols[:, 2 * d + 1:2 * d + 2]
        b_end = b_row[:, CHUNK - 1:CHUNK] if d == 0 else b_row[:, 0:1]
        qc = q_s[pl.ds(off, CHUNK), :]
        kc = k_s[pl.ds(off, CHUNK), :]
        vc = v_s[pl.ds(off, CHUNK), :]
        ct = ct_s[d]

        inter = b_col + m
        dmat = jnp.where(masks[d], b_col - b_row + li_row, -jnp.inf)
        m_t = jnp.maximum(inter, jnp.max(dmat, axis=-1, keepdims=True))
        wts = jnp.exp(dmat - m_t)
        s_inter = jnp.exp(inter - m_t)
        scores = lax.dot_general(qc, kc, (((1,), (1,)), ((), ())), preferred_element_type=F32) * wts
        num = (s_inter * jnp.dot(qc, ct.astype(BF16), preferred_element_type=F32)
               + jnp.dot(scores.astype(BF16), vc.astype(BF16), preferred_element_type=F32))
        den = (s_inter * jnp.sum(qc.astype(F32) * n_vec, axis=-1, keepdims=True)
               + jnp.sum(scores, axis=-1, keepdims=True))
        h = num / jnp.maximum(jnp.abs(den), jnp.exp(-m_t))

        g = b_end - b_col + li_col
        m_new = jnp.maximum(b_end + m, jnp.max(g, axis=0, keepdims=True))
        decay = jnp.exp(b_end + m - m_new)
        w = jnp.exp(g - m_new)
        kv = lax.dot_general(kc, (vc * w).astype(BF16), (((0,), (0,)), ((), ())), preferred_element_type=F32)
        ct_s[d] = decay * ct + kv
        n_new = decay * n_vec + jnp.sum(kc.astype(F32) * w, axis=0, keepdims=True)
        return h, n_new, m_new

    n0 = jnp.zeros((1, HEAD_DIM), F32)
    m0 = jnp.zeros((1, 1), F32)
    _, nf, mf = chunk_step(0, 0, n0, m0)
    _, nb, mb = chunk_step(0, 1, n0, m0)

    def body(i, carry):
        nf, mf, nb, mb = carry
        hf, nf, mf = chunk_step(i, 0, nf, mf)
        hf_s[pl.ds(pl.multiple_of(i * CHUNK - ctx_len, CHUNK), CHUNK), :] = hf
        j = n_chunks - i
        hb, nb, mb = chunk_step(j, 1, nb, mb)
        hb_s[pl.ds(pl.multiple_of(j * CHUNK - ctx_len, CHUNK), CHUNK), :] = hb
        return nf, mf, nb, mb

    lax.fori_loop(1, n_chunks, body, (nf, mf, nb, mb))
    o_ref[...] = (_sigmoid(zo_ref[...]) * (hf_s[...] + hb_s[...])).astype(BF16)


def _mlstm(z, zc, gates_t, conv_w9, conv_b, w_qh, w_kh, bsz, seq, ctx_len):
    total = ctx_len + seq
    hd = HEAD_DIM
    qk_blk, v_blk, o_blk = Z_QK // hd, Z_V // hd, Z_O // hd
    return pl.pallas_call(
        functools.partial(_mlstm_body, ctx_len=ctx_len, seq=seq),
        grid=(bsz, HEADS),
        in_specs=[pl.BlockSpec((seq, hd), lambda b, h: (b, qk_blk + h)),
                  pl.BlockSpec((seq, hd), lambda b, h: (b, v_blk + h)),
                  pl.BlockSpec((seq, hd), lambda b, h: (b, o_blk + h)),
                  pl.BlockSpec((ctx_len, hd), lambda b, h: (b, qk_blk + h)),
                  pl.BlockSpec((ctx_len, hd), lambda b, h: (b, v_blk + h)),
                  pl.BlockSpec((1, 1, 4, total), lambda b, h: (b, h, 0, 0)),
                  pl.BlockSpec((9, hd), lambda b, h: (0, h)),
                  pl.BlockSpec((1, hd), lambda b, h: (0, h)),
                  pl.BlockSpec((1, hd, hd), lambda b, h: (h, 0, 0)),
                  pl.BlockSpec((1, hd, hd), lambda b, h: (h, 0, 0))],
        out_specs=pl.BlockSpec((seq, hd), lambda b, h: (b, h)),
        out_shape=jax.ShapeDtypeStruct((bsz * seq, MLSTM_WIDTH), BF16),
        scratch_shapes=[pltpu.VMEM((total, hd), BF16), pltpu.VMEM((total, hd), BF16),
                        pltpu.VMEM((total, hd), F32), pltpu.VMEM((seq, hd), F32), pltpu.VMEM((seq, hd), F32),
                        pltpu.VMEM((8, total), F32), pltpu.VMEM((2, hd, hd), F32)],
        compiler_params=_cparams(("parallel", "arbitrary"), 48),
        name="mlstm",
    )(z, z, z, zc, zc, gates_t, conv_w9, conv_b, w_qh, w_kh)


def _filt_body(z_ref, w1_ref, b1_ref, wh_ref, bh_ref, fr_ref, a_ref):
    fr = fr_ref[...]
    a = jnp.sin(fr[0:1] * (jnp.dot(z_ref[...], w1_ref[...], precision=HIGHEST, preferred_element_type=F32)
                           + b1_ref[...]))
    for i in range(2):
        a = jnp.sin(fr[i + 1:i + 2] * (jnp.dot(a, wh_ref[i], precision=HIGHEST, preferred_element_type=F32)
                                       + bh_ref[i:i + 1]))
    a_ref[...] = a


def _filt(feats, w1, b1, wh, bh, freq):
    length = feats.shape[0]
    fh = w1.shape[1]
    return pl.pallas_call(
        _filt_body,
        out_shape=jax.ShapeDtypeStruct((length, fh), F32),
        name="filt",
    )(feats, w1, b1, wh, bh, freq)


def _hyena_body(zx1_ref, zx2_ref, zv_ref, cw1_ref, cw2_ref, cwv_ref, cb1_ref, cb2_ref, cbv_ref,
                a_ref, wf_ref, dl_ref, t_ref, skip_ref, cos_ref, sin_ref, o_ref,
                hc_s, hs_s, hn_s, v_s, vb_s, zc_s, zs_s, g_s, *, seq):
    n_fft = 2 * seq
    mrows = 512
    n_m = seq // mrows
    sign = jnp.where(lax.broadcasted_iota(I32, (seq, 1), 0) % 2 == 0, 1.0, -1.0)
    row0 = lax.broadcasted_iota(I32, (seq, 1), 0) == 0

    @pl.when(pl.program_id(1) == 0)
    def _():
        window = jnp.exp(-t_ref[...] * dl_ref[...])
        for o in range(2):
            fwd = jnp.dot(a_ref[...], wf_ref[2 * o], precision=HIGHEST, preferred_element_type=F32) * window
            bwd = jnp.dot(a_ref[...], wf_ref[2 * o + 1], precision=HIGHEST, preferred_element_type=F32) * window
            bwd = jnp.where(row0, 0.0, bwd)
            even = fwd + bwd
            hc_s[o] = jnp.dot(cos_ref[...], even.astype(BF16), preferred_element_type=F32)
            hs_s[o] = jnp.dot(sin_ref[...], (fwd - bwd).astype(BF16), preferred_element_type=F32)
            hn_s[o] = jnp.sum(even * sign, axis=0, keepdims=True)

    def conv_to(dst_ref, z_ref, cw_ref, cb_ref):
        for lo in range(0, z_ref.shape[1], LANES):
            ls = slice(lo, lo + LANES)
            dst_ref[:, ls] = _dwconv(z_ref[:, ls], cw_ref[:, ls], cb_ref[:, ls], GRID_W, False)

    conv_to(v_s, zv_ref, cwv_ref, cbv_ref)

    for o, (zg_ref, cwg_ref, cbg_ref) in enumerate(((zx1_ref, cw1_ref, cb1_ref), (zx2_ref, cw2_ref, cb2_ref))):
        vb_s[...] = v_s[...].astype(BF16)
        x_nyq = jnp.sum(v_s[...] * sign, axis=0, keepdims=True)
        nyq = x_nyq * hn_s[o] * (1.0 / n_fft)

        def fwd_rows(m, carry, o=o):
            sl = pl.ds(pl.multiple_of(m * mrows, mrows), mrows)
            xc = jnp.dot(cos_ref[sl, :], vb_s[...], preferred_element_type=F32)
            xs = jnp.dot(sin_ref[sl, :], vb_s[...], preferred_element_type=F32)
            hc = hc_s[o, sl, :]
            hs = hs_s[o, sl, :]
            first = (lax.broadcasted_iota(I32, (mrows, 1), 0) + m * mrows) == 0
            scale = jnp.where(first, 1.0 / n_fft, 2.0 / n_fft)
            zc_s[sl, :] = (scale * (xc * hc - xs * hs)).astype(BF16)
            zs_s[sl, :] = (scale * (xc * hs + xs * hc)).astype(BF16)
            return carry
        lax.fori_loop(0, n_m, fwd_rows, 0)

        conv_to(g_s, zg_ref, cwg_ref, cbg_ref)

        def inv_rows(m, carry, o=o, nyq=nyq):
            sl = pl.ds(pl.multiple_of(m * mrows, mrows), mrows)
            y = (jnp.dot(cos_ref[sl, :], zc_s[...], preferred_element_type=F32)
                 + jnp.dot(sin_ref[sl, :], zs_s[...], preferred_element_type=F32))
            sgn = jnp.where((lax.broadcasted_iota(I32, (mrows, 1), 0) + m * mrows) % 2 == 0, 1.0, -1.0)
            v_s[sl, :] = g_s[sl, :] * (y + sgn * nyq + v_s[sl, :] * skip_ref[o:o + 1, :])
            return carry
        lax.fori_loop(0, n_m, inv_rows, 0)

    o_ref[...] = v_s[...].astype(BF16)


def _hyena(z, conv_w9, conv_b, a, w_fout4, deltas, tcol, skip, cos_m, sin_m, bsz, seq):
    ct = MXU
    n_ct = HYENA_WIDTH // ct
    hy = Z_HY // ct
    zspec = lambda off: pl.BlockSpec((seq, ct), lambda j, b: (b, hy + off * n_ct + j))
    wspec = lambda off: pl.BlockSpec((9, ct), lambda j, b: (0, off * n_ct + j))
    bspec = lambda off: pl.BlockSpec((1, ct), lambda j, b: (0, off * n_ct + j))
    fh = a.shape[1]
    return pl.pallas_call(
        functools.partial(_hyena_body, seq=seq),
        grid=(n_ct, bsz),
        in_specs=[zspec(0), zspec(1), zspec(2), wspec(0), wspec(1), wspec(2), bspec(0), bspec(1), bspec(2),
                  _const_spec((seq, fh)),
                  pl.BlockSpec((4, fh, ct), lambda j, b: (0, 0, j)),
                  pl.BlockSpec((1, ct), lambda j, b: (0, j)),
                  _const_spec((seq, 1)),
                  pl.BlockSpec((2, ct), lambda j, b: (0, j)),
                  _const_spec((seq, seq)), _const_spec((seq, seq))],
        out_specs=pl.BlockSpec((seq, ct), lambda j, b: (b, j)),
        out_shape=jax.ShapeDtypeStruct((bsz * seq, HYENA_WIDTH), BF16),
        scratch_shapes=[pltpu.VMEM((2, seq, ct), F32), pltpu.VMEM((2, seq, ct), F32), pltpu.VMEM((2, 1, ct), F32),
                        pltpu.VMEM((seq, ct), F32), pltpu.VMEM((seq, ct), BF16),
                        pltpu.VMEM((seq, ct), BF16), pltpu.VMEM((seq, ct), BF16), pltpu.VMEM((seq, ct), F32)],
        compiler_params=_cparams(("arbitrary", "arbitrary"), 60),
        name="hyena",
    )(z, z, z, conv_w9, conv_w9, conv_w9, conv_b, conv_b, conv_b, a, w_fout4, deltas, tcol, skip, cos_m, sin_m)


def _merge_body(oh_ref, hh_ref, ga_ref, gh_ref, x_ref, lng_ref, lnb_ref, g1_ref, l1g_ref, l1b_ref,
                sc2_ref, sh2_ref, wa_ref, wh_ref, wo_ref, wr_ref, br_ref, x1_ref, tok_ref, lg_ref):
    y_a = jnp.dot(oh_ref[...], wa_ref[...], preferred_element_type=F32)
    y_h = jnp.dot(hh_ref[...], wh_ref[...], preferred_element_type=F32)
    mix = _sigmoid(ga_ref[...]) * y_a + _sigmoid(gh_ref[...]) * y_h
    mix = jnp.dot(mix.astype(BF16), wo_ref[...], preferred_element_type=F32)
    x0 = _layer_norm(x_ref[...], lng_ref[...], lnb_ref[...])
    x1 = _layer_norm(DEEPNORM_ALPHA * x0 + g1_ref[0] * mix, l1g_ref[...], l1b_ref[...])
    x1_ref[...] = x1
    tok = x1 * (1.0 + sc2_ref[0]) + sh2_ref[0]
    tok_ref[...] = tok
    lg_ref[...] = jnp.dot(tok, wr_ref[...], precision=HIGHEST, preferred_element_type=F32) + br_ref[...]


def _merge(oh, hh, z, x2d, ln_g, ln_b, g1, ln1_g, ln1_b, sc2, sh2, w_a, w_h, w_o, w_r, b_r, seq):
    t, dm = x2d.shape
    tm = 256
    per_b = seq // tm
    row = lambda i: (i, 0)
    mod = lambda i: (i // per_b, 0, 0)
    return pl.pallas_call(
        _merge_body,
        grid=(t // tm,),
        in_specs=[pl.BlockSpec((tm, MLSTM_WIDTH), row), pl.BlockSpec((tm, HYENA_WIDTH), row),
                  pl.BlockSpec((tm, dm), lambda i: (i, Z_BG // dm)),
                  pl.BlockSpec((tm, dm), lambda i: (i, Z_BG // dm + 1)),
                  pl.BlockSpec((tm, dm), row),
                  _const_spec((1, dm)), _const_spec((1, dm)),
                  pl.BlockSpec((1, 1, dm), mod),
                  _const_spec((1, dm)), _const_spec((1, dm)),
                  pl.BlockSpec((1, 1, dm), mod), pl.BlockSpec((1, 1, dm), mod),
                  _const_spec((MLSTM_WIDTH, dm)), _const_spec((HYENA_WIDTH, dm)), _const_spec((dm, dm)),
                  _const_spec((dm, LANES)), _const_spec((1, LANES))],
        out_specs=[pl.BlockSpec((tm, dm), row), pl.BlockSpec((tm, dm), row), pl.BlockSpec((tm, LANES), row)],
        out_shape=[jax.ShapeDtypeStruct((t, dm), F32), jax.ShapeDtypeStruct((t, dm), F32),
                   jax.ShapeDtypeStruct((t, LANES), F32)],
        compiler_params=_cparams(("parallel",), 56),
        name="merge",
    )(oh, hh, z, z, x2d, ln_g, ln_b, g1, ln1_g, ln1_b, sc2, sh2, w_a, w_h, w_o, w_r, b_r)


def _route_body(lg_ref, e_ref, w_ref, r_ref, cnt_ref, run_s, *, tr):
    @pl.when(pl.program_id(0) == 0)
    def _():
        run_s[...] = jnp.zeros_like(run_s)

    lane = lax.broadcasted_iota(I32, (tr, LANES), 1)
    lane_f = lane.astype(F32)
    logit = lg_ref[...]
    hot, idx, val = [], [], []
    for _ in range(TOP_K):
        mk = jnp.max(logit, axis=-1, keepdims=True)
        ik = jnp.min(jnp.where(logit == mk, lane_f, float(LANES)), axis=-1, keepdims=True)
        hk = lane_f == ik
        logit = jnp.where(hk, -jnp.inf, logit)
        hot.append(hk)
        idx.append(ik)
        val.append(mk)
    ex = [jnp.exp(v - val[0]) for v in val]
    denom = ex[0] + ex[1] + ex[2] + ex[3]

    cnt = jnp.zeros((tr, LANES), F32)
    for hk in hot:
        cnt = cnt + jnp.where(hk, 1.0, 0.0)
    lower = (lax.broadcasted_iota(I32, (tr, tr), 0) > lax.broadcasted_iota(I32, (tr, tr), 1))
    before = jnp.dot(jnp.where(lower, 1.0, 0.0).astype(BF16), cnt.astype(BF16),
                     preferred_element_type=F32) + run_s[0:1, :]
    e_out = jnp.zeros((tr, LANES), I32)
    w_out = jnp.zeros((tr, LANES), F32)
    r_out = jnp.zeros((tr, LANES), I32)
    for k in range(TOP_K):
        rank = jnp.sum(jnp.where(hot[k], before, 0.0), axis=-1, keepdims=True)
        e_out = jnp.where(lane == k, idx[k].astype(I32), e_out)
        w_out = jnp.where(lane == k, ex[k] / denom, w_out)
        r_out = jnp.where(lane == k, rank.astype(I32), r_out)
    e_ref[...] = e_out
    w_ref[...] = w_out
    r_ref[...] = r_out
    run = run_s[0:1, :] + jnp.sum(cnt, axis=0, keepdims=True)
    run_s[...] = jnp.broadcast_to(run, run_s.shape)
    cnt_ref[...] = jnp.broadcast_to(run, cnt_ref.shape).astype(I32)


def _route(logits):
    t = logits.shape[0]
    tr = 512
    row = lambda i: (i, 0)
    return pl.pallas_call(
        functools.partial(_route_body, tr=tr),
        grid=(t // tr,),
        in_specs=[pl.BlockSpec((tr, LANES), row)],
        out_specs=[pl.BlockSpec((tr, LANES), row), pl.BlockSpec((tr, LANES), row), pl.BlockSpec((tr, LANES), row),
                   pl.BlockSpec((8, LANES), lambda i: (0, 0))],
        out_shape=[jax.ShapeDtypeStruct((t, LANES), I32), jax.ShapeDtypeStruct((t, LANES), F32),
                   jax.ShapeDtypeStruct((t, LANES), I32), jax.ShapeDtypeStruct((8, LANES), I32)],
        scratch_shapes=[pltpu.VMEM((8, LANES), F32)],
        compiler_params=_cparams(("arbitrary",), 32),
        name="route",
    )(logits)


def _scatter_body(cnt_ref, pstart_ref, dest_ref, tok_ref, xb_ref, zero_s, sem, pad_sem, *, ts):
    def row_copy(src, r_src, r_dst, s):
        return pltpu.make_async_copy(src.at[pl.ds(r_src, 1)], xb_ref.at[pl.ds(r_dst, 1)], s)

    @pl.when(pl.program_id(0) == 0)
    def _():
        zero_s[...] = jnp.zeros_like(zero_s)

        def per_expert(e, carry):
            cnt = cnt_ref[e]
            lo = pstart_ref[e] + cnt
            hi = pstart_ref[e] + (cnt + MOE_BLOCK - 1) // MOE_BLOCK * MOE_BLOCK

            def start(r, c):
                row_copy(zero_s, 0, r, pad_sem).start()
                return c

            def wait(r, c):
                row_copy(zero_s, 0, r, pad_sem).wait()
                return c
            lax.fori_loop(lo, hi, start, 0)
            lax.fori_loop(lo, hi, wait, 0)
            return carry
        lax.fori_loop(0, N_EXPERTS, per_expert, 0)

    def row(r, carry):
        for k in range(TOP_K):
            row_copy(tok_ref, r, dest_ref[r * TOP_K + k], sem).start()
        return carry
    lax.fori_loop(0, ts, row, 0)
    for _ in range(TOP_K):
        pltpu.make_async_copy(tok_ref, xb_ref.at[pl.ds(0, ts)], sem).wait()


def _scatter(tok, dest_flat, counts, pstart, rows_total):
    t, dm = tok.shape
    ts = 256
    grid_spec = pltpu.PrefetchScalarGridSpec(
        num_scalar_prefetch=2,
        grid=(t // ts,),
        in_specs=[pl.BlockSpec((ts * TOP_K,), lambda i, *_: (i,), memory_space=pltpu.SMEM),
                  pl.BlockSpec((ts, dm), lambda i, *_: (i, 0))],
        out_specs=pl.BlockSpec(memory_space=pl.ANY),
        scratch_shapes=[pltpu.VMEM((8, dm), F32), pltpu.SemaphoreType.DMA(()), pltpu.SemaphoreType.DMA(())],
    )
    return pl.pallas_call(
        functools.partial(_scatter_body, ts=ts),
        grid_spec=grid_spec,
        out_shape=jax.ShapeDtypeStruct((rows_total, dm), F32),
        compiler_params=_cparams(("arbitrary",), 32),
        name="scatter",
    )(counts, pstart, dest_flat, tok)


def _ffn1_body(e_ref, c_ref, blk_ref, first_ref, n_ref, x_ref, wg_ref, wu_ref, bg_ref, bu_ref, a_ref, wg_s, wu_s):
    s = pl.program_id(0)

    @pl.when(jnp.logical_and(s < n_ref[0], first_ref[s] == 1))
    def _():
        wg_s[...] = wg_ref[0].astype(BF16)
        wu_s[...] = wu_ref[0].astype(BF16)

    @pl.when(s < n_ref[0])
    def _():
        x = x_ref[...].astype(BF16)
        g = jnp.minimum(jnp.dot(x, wg_s[...], preferred_element_type=F32) + bg_ref[0], SWIGLU_LIMIT)
        u = jnp.clip(jnp.dot(x, wu_s[...], preferred_element_type=F32) + bu_ref[0], -SWIGLU_LIMIT, SWIGLU_LIMIT)
        a_ref[...] = (g * _sigmoid(SWIGLU_ALPHA * g) * (u + 1.0)).astype(BF16)


def _ffn2_body(e_ref, c_ref, blk_ref, first_ref, n_ref, a_ref, wd_ref, bd_ref, y_ref, wd_s):
    s = pl.program_id(0)

    @pl.when(jnp.logical_and(s < n_ref[0], first_ref[s] == 1))
    def _():
        wd_s[...] = wd_ref[0].astype(BF16)

    @pl.when(s < n_ref[0])
    def _():
        y_ref[...] = jnp.dot(a_ref[...], wd_s[...], preferred_element_type=F32) + bd_ref[0]


def _ffn1(plan, xb, w_gate, w_up, b_gate, b_up, n_steps):
    rows, dm = xb.shape
    de = w_gate.shape[2]
    wspec = pl.BlockSpec((1, dm, FF_CHUNK), lambda s, e, c, blk, f, n: (e[s], 0, c[s]))
    bspec = pl.BlockSpec((1, 1, FF_CHUNK), lambda s, e, c, blk, f, n: (e[s], 0, c[s]))
    grid_spec = pltpu.PrefetchScalarGridSpec(
        num_scalar_prefetch=5,
        grid=(n_steps,),
        in_specs=[pl.BlockSpec((MOE_BLOCK, dm), lambda s, e, c, blk, f, n: (blk[s], 0)), wspec, wspec, bspec, bspec],
        out_specs=pl.BlockSpec((MOE_BLOCK, FF_CHUNK), lambda s, e, c, blk, f, n: (blk[s], c[s])),
        scratch_shapes=[pltpu.VMEM((dm, FF_CHUNK), BF16), pltpu.VMEM((dm, FF_CHUNK), BF16)],
    )
    return pl.pallas_call(
        _ffn1_body, grid_spec=grid_spec,
        out_shape=jax.ShapeDtypeStruct((rows, de), BF16),
        compiler_params=_cparams(("arbitrary",), 56),
        name="ffn1",
    )(*plan, xb, w_gate, w_up, b_gate, b_up)


def _ffn2(plan, act, w_down, b_down, n_steps):
    rows, de = act.shape
    dm = w_down.shape[2]
    grid_spec = pltpu.PrefetchScalarGridSpec(
        num_scalar_prefetch=5,
        grid=(n_steps,),
        in_specs=[pl.BlockSpec((MOE_BLOCK, de), lambda s, e, c, blk, f, n: (blk[s], 0)),
                  pl.BlockSpec((1, de, FF_CHUNK), lambda s, e, c, blk, f, n: (e[s], 0, c[s])),
                  pl.BlockSpec((1, 1, FF_CHUNK), lambda s, e, c, blk, f, n: (e[s], 0, c[s]))],
        out_specs=pl.BlockSpec((MOE_BLOCK, FF_CHUNK), lambda s, e, c, blk, f, n: (blk[s], c[s])),
        scratch_shapes=[pltpu.VMEM((de, FF_CHUNK), BF16)],
    )
    return pl.pallas_call(
        _ffn2_body, grid_spec=grid_spec,
        out_shape=jax.ShapeDtypeStruct((rows, dm), F32),
        compiler_params=_cparams(("arbitrary",), 48),
        name="ffn2",
    )(*plan, act, w_down, b_down)


def _combine_body(dcur_ref, dnxt_ref, w_ref, x1_ref, g2_ref, lg_ref, lb_ref, yb_ref, o_ref, buf, sem, *, tc, n_tiles):
    i = pl.program_id(0)

    def issue(d_ref, slot):
        def row(r, carry):
            for k in range(TOP_K):
                pltpu.make_async_copy(yb_ref.at[pl.ds(d_ref[r * TOP_K + k], 1)],
                                      buf.at[slot, k, pl.ds(r, 1)], sem.at[slot]).start()
            return carry
        lax.fori_loop(0, tc, row, 0)

    @pl.when(i == 0)
    def _():
        issue(dcur_ref, 0)

    @pl.when(i + 1 < n_tiles)
    def _():
        issue(dnxt_ref, (i + 1) % 2)

    slot = i % 2
    for k in range(TOP_K):
        pltpu.make_async_copy(yb_ref.at[pl.ds(0, tc)], buf.at[slot, k], sem.at[slot]).wait()
    w = w_ref[...]
    y = w[:, 0:1] * buf[slot, 0]
    for k in range(1, TOP_K):
        y = y + w[:, k:k + 1] * buf[slot, k]
    o_ref[...] = _layer_norm(DEEPNORM_ALPHA * x1_ref[...] + g2_ref[0] * y, lg_ref[...], lb_ref[...])


def _combine(dest_flat, w4, x1, g2, ln_g, ln_b, yb, seq):
    t, dm = x1.shape
    tc = 256
    n_tiles = t // tc
    per_b = seq // tc
    row = lambda i: (i, 0)
    return pl.pallas_call(
        functools.partial(_combine_body, tc=tc, n_tiles=n_tiles),
        grid=(n_tiles,),
        in_specs=[pl.BlockSpec((tc * TOP_K,), lambda i: (i,), memory_space=pltpu.SMEM),
                  pl.BlockSpec((tc * TOP_K,), lambda i: (jnp.minimum(i + 1, n_tiles - 1),), memory_space=pltpu.SMEM),
                  pl.BlockSpec((tc, LANES), row),
                  pl.BlockSpec((tc, dm), row),
                  pl.BlockSpec((1, 1, dm), lambda i: (i // per_b, 0, 0)),
                  _const_spec((1, dm)), _const_spec((1, dm)),
                  pl.BlockSpec(memory_space=pl.ANY)],
        out_specs=pl.BlockSpec((tc, dm), row),
        out_shape=jax.ShapeDtypeStruct((t, dm), F32),
        scratch_shapes=[pltpu.VMEM((2, TOP_K, tc, dm), F32), pltpu.SemaphoreType.DMA((2,))],
        compiler_params=_cparams(("arbitrary",), 40),
        name="combine",
    )(dest_flat, dest_flat, w4, x1, g2, ln_g, ln_b, yb)


def _dft_tables(seq):
    n = 2 * seq
    k = np.arange(seq, dtype=np.int64)
    ang = (2.0 * np.pi / n) * ((k[:, None] * k[None, :]) % n).astype(np.float64)
    return np.cos(ang).astype(BF16), np.sin(ang).astype(BF16)


def _filter_features(seq):
    t = jnp.linspace(0.0, 1.0, seq, dtype=F32)[:, None]
    bands = (HYENA_EMB - 1) // 2
    f = jnp.linspace(1e-4, bands - 1, bands, dtype=F32)[None, :]
    ang = 2.0 * math.pi * jnp.arange(seq, dtype=F32)[:, None] * f / seq
    feats = jnp.concatenate([t, jnp.cos(ang), -jnp.sin(ang)], axis=-1)
    max_decay = math.log(HYENA_DECAY_TARGET) / HYENA_SHORT_DECAY_PCT
    min_decay = math.log(HYENA_DECAY_TARGET) / HYENA_LONG_DECAY_PCT
    deltas = jnp.abs(jnp.linspace(min_decay, max_decay, HYENA_WIDTH, dtype=F32))[None, :]
    return t, feats, deltas


def _moe_plan(e4, rank4, counts, n_blocks, n_chunks):
    nblk = (counts + MOE_BLOCK - 1) // MOE_BLOCK
    pend = jnp.cumsum(nblk * MOE_BLOCK)
    pstart = pend - nblk * MOE_BLOCK
    dest = jnp.take(pstart, e4) + rank4
    blk_start = jnp.cumsum(nblk) - nblk
    steps = n_chunks * nblk
    step_end = jnp.cumsum(steps)
    n_used = step_end[-1]
    s = jnp.minimum(jnp.arange(n_chunks * n_blocks, dtype=I32), n_used - 1)
    e_s = jnp.minimum(jnp.searchsorted(step_end, s, side='right'), N_EXPERTS - 1).astype(I32)
    loc = s - jnp.take(step_end - steps, e_s)
    nb = jnp.maximum(jnp.take(nblk, e_s), 1)
    r_s = loc % nb
    plan = (e_s, (loc // nb).astype(I32), (jnp.take(blk_start, e_s) + r_s).astype(I32),
            (r_s == 0).astype(I32), n_used.reshape(1).astype(I32))
    return dest.astype(I32), pstart.astype(I32), plan


def kernel(x, c, ctx, c_ctx, ln_in_g, ln_in_b, w_mod, b_mod, w_in, b_in, mlstm_conv_w, mlstm_conv_b,
           w_qh, w_kh, hyena_conv_w, hyena_conv_b, filt_w1, filt_b1, filt_wh, filt_bh, filt_freq, filt_wout,
           hyena_skip, w_proj_a, w_proj_h, w_out, ln1_g, ln1_b, w_router, b_router, w_gate, b_gate,
           w_up, b_up, w_down, b_down, ln2_g, ln2_b):
    bsz, seq, dm = x.shape
    ctx_len = ctx.shape[1]
    t = bsz * seq
    assert w_mod.shape[0] == DEPTH and dm == D_MODEL and ctx_len == CHUNK and bsz + 1 <= 16
    row = lambda v: v.reshape(1, -1)

    cond = jnp.concatenate([c, c_ctx[None], jnp.zeros((16 - bsz - 1, dm), F32)], axis=0)
    mod = _mod(cond, w_mod[0], row(b_mod[0]))
    sh1, sc1, g1, sh2, sc2, g2 = [m[:, None, :] for m in jnp.split(mod, 6, axis=-1)]

    w_main = jnp.concatenate([w_in[0][:, :IN_GATES], w_in[0][:, IN_O:]], axis=1).astype(BF16)
    b_main = row(jnp.concatenate([b_in[0][:IN_GATES], b_in[0][IN_O:]]))
    w_g = jnp.pad(w_in[0][:, IN_GATES:IN_O], ((0, 0), (0, LANES - 4 * HEADS))).astype(BF16)
    b_g = row(jnp.pad(b_in[0][IN_GATES:IN_O], (0, LANES - 4 * HEADS)))
    lng, lnb = row(ln_in_g), row(ln_in_b)
    x2d = x.reshape(t, dm)
    z, gates = _in_proj(x2d, lng, lnb, sc1[:bsz], sh1[:bsz], w_main, b_main, w_g, b_g, seq)
    zc, gates_c = _in_proj(ctx.reshape(bsz * ctx_len, dm), lng, lnb, sc1[bsz:bsz + 1], sh1[bsz:bsz + 1],
                           w_main[:, :IN_GATES], b_main[:, :IN_GATES], w_g, b_g, bsz * ctx_len)

    g_all = jnp.concatenate([gates_c[:, :4 * HEADS].reshape(bsz, ctx_len, 4, HEADS),
                             gates[:, :4 * HEADS].reshape(bsz, seq, 4, HEADS)], axis=1)
    gates_t = g_all.transpose(0, 3, 2, 1)

    oh = _mlstm(z, zc, gates_t, mlstm_conv_w[0].reshape(9, MLSTM_WIDTH), row(mlstm_conv_b[0]),
                w_qh[0], w_kh[0], bsz, seq, ctx_len)

    tcol, feats, deltas = _filter_features(seq)
    feats = jnp.pad(feats, ((0, 0), (0, LANES - HYENA_EMB)))
    w1 = jnp.pad(filt_w1[0], ((0, LANES - HYENA_EMB), (0, 0)))
    a = _filt(feats, w1, row(filt_b1[0]), filt_wh[0], filt_bh[0], filt_freq[0])
    fh = a.shape[1]
    w_fout4 = filt_wout[0].reshape(fh, 4, HYENA_WIDTH).transpose(1, 0, 2)
    cos_m, sin_m = _dft_tables(seq)
    hh = _hyena(z, hyena_conv_w[0].reshape(9, 3 * HYENA_WIDTH), row(hyena_conv_b[0]), a, w_fout4, deltas, tcol,
                hyena_skip[0], cos_m, sin_m, bsz, seq)

    w_r = jnp.pad(w_router[0], ((0, 0), (0, LANES - N_EXPERTS)))
    b_r = row(jnp.pad(b_router[0], (0, LANES - N_EXPERTS), constant_values=-1e30))
    x1, tok, logits = _merge(oh, hh, z, x2d, lng, lnb, g1[:bsz], row(ln1_g[0]), row(ln1_b[0]), sc2[:bsz], sh2[:bsz],
                             w_proj_a[0].astype(BF16), w_proj_h[0].astype(BF16), w_out[0].astype(BF16), w_r, b_r, seq)

    e4, w4, rank4, counts = _route(logits)
    n_blocks = -(-(t * TOP_K + N_EXPERTS * (MOE_BLOCK - 1)) // MOE_BLOCK)
    n_chunks = dm // FF_CHUNK
    dest, pstart, plan = _moe_plan(e4[:, :TOP_K], rank4[:, :TOP_K], counts[0, :N_EXPERTS], n_blocks, n_chunks)
    dest_flat = dest.reshape(t * TOP_K)
    xb = _scatter(tok, dest_flat, counts[0, :N_EXPERTS], pstart, n_blocks * MOE_BLOCK)
    n_steps = n_chunks * n_blocks
    act = _ffn1(plan, xb, w_gate[0], w_up[0], b_gate[0][:, None, :], b_up[0][:, None, :], n_steps)
    yb = _ffn2(plan, act, w_down[0], b_down[0][:, None, :], n_steps)
    out = _combine(dest_flat, w4, x1, g2[:bsz], row(ln2_g[0]), row(ln2_b[0]), yb, seq)
    return out.reshape(bsz, seq, dm)
```

```python
import functools
import math

import numpy as np
import jax
import jax.numpy as jnp
from jax import lax
from jax.experimental import pallas as pl
from jax.experimental.pallas import tpu as pltpu

F32 = jnp.float32
BF16 = jnp.bfloat16
I32 = jnp.int32
HIGHEST = lax.Precision.HIGHEST

D_MODEL = 2048
GRID_W = 64
HEADS = 4
HEAD_DIM = 256
MLSTM_WIDTH = HEADS * HEAD_DIM
HYENA_WIDTH = D_MODEL // 2
HYENA_EMB = 33
HYENA_DECAY_TARGET = 1e-2
HYENA_SHORT_DECAY_PCT = 0.3
HYENA_LONG_DECAY_PCT = 1.5
N_EXPERTS = 32
TOP_K = 4
SWIGLU_LIMIT = 7.0
SWIGLU_ALPHA = 1.702
LN_EPS = 1e-5
DEPTH = 1
DEEPNORM_ALPHA = (2.0 * DEPTH) ** 0.25

IN_V = MLSTM_WIDTH
IN_GATES = 2 * MLSTM_WIDTH
IN_O = IN_GATES + 4 * HEADS
Z_QK, Z_V, Z_O, Z_HY, Z_BG = 0, 1024, 2048, 3072, 6144

LANES = 128
MXU = 256
CHUNK = 256
MOE_BLOCK = 512
FF_CHUNK = 1024
MIB = 1024 * 1024


def _cparams(semantics, vmem_mib):
    return pltpu.CompilerParams(dimension_semantics=semantics, vmem_limit_bytes=vmem_mib * MIB)


def _const_spec(shape):
    nd = len(shape)
    return pl.BlockSpec(shape, lambda *_: (0,) * nd, pipeline_mode=pl.Buffered(1))


def _layer_norm(x, g, b):
    mu = jnp.mean(x, axis=-1, keepdims=True)
    xc = x - mu
    var = jnp.mean(xc * xc, axis=-1, keepdims=True)
    return xc * lax.rsqrt(var + LN_EPS) * g + b


def _sigmoid(x):
    return 1.0 / (1.0 + jnp.exp(-x))


def _silu(x):
    return x * _sigmoid(x)


def _log_sigmoid(x):
    return jnp.minimum(x, 0.0) - jnp.log(1.0 + jnp.exp(-jnp.abs(x)))


def _mod_body(c_ref, w_ref, b_ref, o_ref):
    s = _silu(c_ref[...])
    o_ref[...] = jnp.dot(s.astype(BF16), w_ref[...].astype(BF16), preferred_element_type=F32) + b_ref[...]


def _mod(cond, w, b):
    rows, dm = cond.shape
    n = w.shape[1]
    tn = 1024
    return pl.pallas_call(
        _mod_body,
        grid=(n // tn,),
        in_specs=[pl.BlockSpec((rows, dm), lambda j: (0, 0)),
                  pl.BlockSpec((dm, tn), lambda j: (0, j)),
                  pl.BlockSpec((1, tn), lambda j: (0, j))],
        out_specs=pl.BlockSpec((rows, tn), lambda j: (0, j)),
        out_shape=jax.ShapeDtypeStruct((rows, n), F32),
        compiler_params=_cparams(("arbitrary",), 40),
        name="mod",
    )(cond, w, b)


def _in_proj_body(x_ref, lng_ref, lnb_ref, sc_ref, sh_ref, w_ref, b_ref, wg_ref, bg_ref,
                  z_ref, g_ref, *rest, tm, n_plain):
    hx_s = rest[-1]

    @pl.when(pl.program_id(1) == 0)
    def _():
        def rows(r, carry):
            sl = pl.ds(pl.multiple_of(r * 128, 128), 128)
            xn = _layer_norm(x_ref[sl, :], lng_ref[...], lnb_ref[...])
            hx_s[sl, :] = (xn * (1.0 + sc_ref[0]) + sh_ref[0]).astype(BF16)
            return carry
        lax.fori_loop(0, tm // 128, rows, 0)
        g_ref[...] = jnp.dot(hx_s[...], wg_ref[...], preferred_element_type=F32) + bg_ref[...]

    acc = jnp.dot(hx_s[...], w_ref[...], preferred_element_type=F32) + b_ref[...]
    if len(rest) == 1:
        z_ref[...] = acc
    else:
        @pl.when(pl.program_id(1) < n_plain)
        def _():
            z_ref[...] = acc

        @pl.when(pl.program_id(1) >= n_plain)
        def _():
            rest[0][...] = _sigmoid(acc).astype(BF16)


def _in_proj(x2d, ln_g, ln_b, scale, shift, w, b, w_gates, b_gates, rows_per_mod, plain_cols):
    t, dm = x2d.shape
    n = w.shape[1]
    tm, tn = min(1024, t), 512
    n_plain = plain_cols // tn
    out_specs = [pl.BlockSpec((tm, tn), lambda i, j: (i, jnp.minimum(j, n_plain - 1))),
                 pl.BlockSpec((tm, LANES), lambda i, j: (i, 0))]
    out_shape = [jax.ShapeDtypeStruct((t, plain_cols), F32), jax.ShapeDtypeStruct((t, LANES), F32)]
    if n > plain_cols:
        out_specs.append(pl.BlockSpec((tm, tn), lambda i, j: (i, jnp.maximum(j - n_plain, 0))))
        out_shape.append(jax.ShapeDtypeStruct((t, n - plain_cols), BF16))
    return pl.pallas_call(
        functools.partial(_in_proj_body, tm=tm, n_plain=n_plain),
        grid=(t // tm, n // tn),
        in_specs=[pl.BlockSpec((tm, dm), lambda i, j: (i, 0)),
                  pl.BlockSpec((1, dm), lambda i, j: (0, 0)),
                  pl.BlockSpec((1, dm), lambda i, j: (0, 0)),
                  pl.BlockSpec((1, 1, dm), lambda i, j: (i * tm // rows_per_mod, 0, 0)),
                  pl.BlockSpec((1, 1, dm), lambda i, j: (i * tm // rows_per_mod, 0, 0)),
                  pl.BlockSpec((dm, tn), lambda i, j: (0, j)),
                  pl.BlockSpec((1, tn), lambda i, j: (0, j)),
                  pl.BlockSpec((dm, LANES), lambda i, j: (0, 0)),
                  pl.BlockSpec((1, LANES), lambda i, j: (0, 0))],
        out_specs=out_specs,
        out_shape=out_shape,
        scratch_shapes=[pltpu.VMEM((tm, dm), BF16)],
        compiler_params=_cparams(("parallel", "arbitrary"), 48),
        name="in_proj",
    )(x2d, ln_g, ln_b, scale, shift, w, b, w_gates, b_gates)


def _dwconv(u, w9, bias, width, single_row):
    length, ch = u.shape
    col = lax.broadcasted_iota(I32, (length, ch), 0) % width
    if not single_row:
        zpad = jnp.zeros((width, ch), F32)
        up = jnp.concatenate([zpad, u[:length - width]], axis=0)
        dn = jnp.concatenate([u[width:], zpad], axis=0)
    out = None
    for dc in (-1, 0, 1):
        a = u * w9[4 + dc:5 + dc]
        if not single_row:
            a = a + up * w9[1 + dc:2 + dc] + dn * w9[7 + dc:8 + dc]
        if dc == -1:
            a = jnp.where(col == 0, 0.0, pltpu.roll(a, 1, 0))
        elif dc == 1:
            a = jnp.where(col == width - 1, 0.0, pltpu.roll(a, length - 1, 0))
        out = a if out is None else out + a
    return out + bias


def _mlstm_body(zqk_ref, zv_ref, zo_ref, cqk_ref, cv_ref, gt_ref, cw_ref, cb_ref, wq_ref, wkt_ref,
                o_ref, q_s, k_s, kt_s, v_s, hf_s, hb_s, r_s, c_s, ct_s, *, ctx_len, seq):
    n_chunks = (ctx_len + seq) // CHUNK
    total = ctx_len + seq
    cw = cw_ref[...]
    cb = cb_ref[...]
    wq = wq_ref[0].astype(BF16)
    wkt = wkt_ref[0].astype(BF16)
    nt = (((1,), (1,)), ((), ()))
    scale = HEAD_DIM ** -0.5

    def project(u, off, n):
        ub = u.astype(BF16)
        q_s[off:off + n, :] = jnp.dot(ub, wq, preferred_element_type=F32).astype(BF16)
        k_s[off:off + n, :] = (lax.dot_general(ub, wkt, nt, preferred_element_type=F32) * scale).astype(BF16)
        kt_s[:, off:off + n] = (lax.dot_general(wkt, ub, nt, preferred_element_type=F32) * scale).astype(BF16)

    project(_silu(_dwconv(cqk_ref[...], cw, cb, ctx_len, True)), 0, ctx_len)
    project(_silu(_dwconv(zqk_ref[...], cw, cb, GRID_W, False)), ctx_len, seq)
    v_s[0:ctx_len, :] = cv_ref[...]
    v_s[ctx_len:total, :] = zv_ref[...]

    gt = gt_ref[0, 0]
    lf = _log_sigmoid(gt)
    pos = lax.broadcasted_iota(I32, (4, total), 1) % CHUNK
    pre, suf = lf, lf
    s = 1
    while s < CHUNK:
        pre = pre + jnp.where(pos >= s, pltpu.roll(pre, s, 1), 0.0)
        suf = suf + jnp.where(pos < CHUNK - s, pltpu.roll(suf, total - s, 1), 0.0)
        s *= 2
    b_f, li_f, b_b, li_b = pre[1:2], gt[0:1], suf[3:4], gt[2:3]
    pm_f, pm_b = li_f - b_f, li_b - b_b
    pos1 = pos[0:1]
    s = 1
    while s < CHUNK:
        pm_f = jnp.maximum(pm_f, jnp.where(pos1 >= s, pltpu.roll(pm_f, s, 1), -jnp.inf))
        pm_b = jnp.maximum(pm_b, jnp.where(pos1 < CHUNK - s, pltpu.roll(pm_b, total - s, 1), -jnp.inf))
        s *= 2
    r_s[...] = jnp.concatenate([b_f, li_f, pm_f, b_b, li_b, pm_b, jnp.zeros((2, total), F32)], axis=0)
    zfill = jnp.zeros((CHUNK - 8, CHUNK), F32)
    for c in range(n_chunks):
        blk = jnp.concatenate([r_s[:, c * CHUNK:(c + 1) * CHUNK], zfill], axis=0).T
        c_s[c * CHUNK:(c + 1) * CHUNK, :] = blk[:, :LANES]

    ct_s[...] = jnp.zeros_like(ct_s)
    row_i = lax.broadcasted_iota(I32, (CHUNK, CHUNK), 0)
    col_i = lax.broadcasted_iota(I32, (CHUNK, CHUNK), 1)
    masks = (row_i >= col_i, row_i <= col_i)

    def chunk_step(c, d, n_vec, m):
        off = c * CHUNK if isinstance(c, int) else pl.multiple_of(c * CHUNK, CHUNK)
        rows = r_s[:, pl.ds(off, CHUNK)]
        cols = c_s[pl.ds(off, CHUNK), :]
        b_row, li_row, pm_row = rows[3 * d:3 * d + 1], rows[3 * d + 1:3 * d + 2], rows[3 * d + 2:3 * d + 3]
        b_col, li_col, pm_col = cols[:, 3 * d:3 * d + 1], cols[:, 3 * d + 1:3 * d + 2], cols[:, 3 * d + 2:3 * d + 3]
        last = slice(CHUNK - 1, CHUNK) if d == 0 else slice(0, 1)
        b_end, pm_end = b_row[:, last], pm_row[:, last]
        qc = q_s[pl.ds(off, CHUNK), :]
        kc = k_s[pl.ds(off, CHUNK), :]
        ktc = kt_s[:, pl.ds(off, CHUNK)]
        vc = v_s[pl.ds(off, CHUNK), :]
        ct = ct_s[d]

        inter = b_col + m
        m_t = jnp.maximum(inter, b_col + pm_col)
        wts = jnp.exp(jnp.where(masks[d], b_col - b_row + li_row, -jnp.inf) - m_t)
        s_inter = jnp.exp(inter - m_t)
        scores = jnp.dot(qc, ktc, preferred_element_type=F32) * wts
        num = (s_inter * jnp.dot(qc, ct.astype(BF16), preferred_element_type=F32)
               + jnp.dot(scores.astype(BF16), vc.astype(BF16), preferred_element_type=F32))
        den = (s_inter * jnp.sum(qc.astype(F32) * n_vec, axis=-1, keepdims=True)
               + jnp.sum(scores, axis=-1, keepdims=True))
        h = num / jnp.maximum(jnp.abs(den), jnp.exp(-m_t))

        m_new = jnp.maximum(b_end + m, b_end + pm_end)
        decay = jnp.exp(b_end + m - m_new)
        w = jnp.exp(b_end - b_col + li_col - m_new)
        ct_s[d] = decay * ct + jnp.dot(ktc, (vc * w).astype(BF16), preferred_element_type=F32)
        n_new = decay * n_vec + jnp.sum(kc.astype(F32) * w, axis=0, keepdims=True)
        return h, n_new, m_new

    n0 = jnp.zeros((1, HEAD_DIM), F32)
    m0 = jnp.zeros((1, 1), F32)
    _, nf, mf = chunk_step(0, 0, n0, m0)
    _, nb, mb = chunk_step(0, 1, n0, m0)

    def body(i, carry):
        nf, mf, nb, mb = carry
        hf, nf, mf = chunk_step(i, 0, nf, mf)
        hf_s[pl.ds(pl.multiple_of(i * CHUNK - ctx_len, CHUNK), CHUNK), :] = hf
        j = n_chunks - i
        hb, nb, mb = chunk_step(j, 1, nb, mb)
        hb_s[pl.ds(pl.multiple_of(j * CHUNK - ctx_len, CHUNK), CHUNK), :] = hb
        return nf, mf, nb, mb

    lax.fori_loop(1, n_chunks, body, (nf, mf, nb, mb))
    o_ref[...] = (_sigmoid(zo_ref[...]) * (hf_s[...] + hb_s[...])).astype(BF16)


def _mlstm(z, zc, gates_t, conv_w9, conv_b, w_qh, w_kh_t, bsz, seq, ctx_len):
    total = ctx_len + seq
    hd = HEAD_DIM
    qk_blk, v_blk, o_blk = Z_QK // hd, Z_V // hd, Z_O // hd
    return pl.pallas_call(
        functools.partial(_mlstm_body, ctx_len=ctx_len, seq=seq),
        grid=(bsz, HEADS),
        in_specs=[pl.BlockSpec((seq, hd), lambda b, h: (b, qk_blk + h)),
                  pl.BlockSpec((seq, hd), lambda b, h: (b, v_blk + h)),
                  pl.BlockSpec((seq, hd), lambda b, h: (b, o_blk + h)),
                  pl.BlockSpec((ctx_len, hd), lambda b, h: (b, qk_blk + h)),
                  pl.BlockSpec((ctx_len, hd), lambda b, h: (b, v_blk + h)),
                  pl.BlockSpec((1, 1, 4, total), lambda b, h: (b, h, 0, 0)),
                  pl.BlockSpec((9, hd), lambda b, h: (0, h)),
                  pl.BlockSpec((1, hd), lambda b, h: (0, h)),
                  pl.BlockSpec((1, hd, hd), lambda b, h: (h, 0, 0)),
                  pl.BlockSpec((1, hd, hd), lambda b, h: (h, 0, 0))],
        out_specs=pl.BlockSpec((seq, hd), lambda b, h: (b, h)),
        out_shape=jax.ShapeDtypeStruct((bsz * seq, MLSTM_WIDTH), BF16),
        scratch_shapes=[pltpu.VMEM((total, hd), BF16), pltpu.VMEM((total, hd), BF16), pltpu.VMEM((hd, total), BF16),
                        pltpu.VMEM((total, hd), F32), pltpu.VMEM((seq, hd), F32), pltpu.VMEM((seq, hd), F32),
                        pltpu.VMEM((8, total), F32), pltpu.VMEM((total, LANES), F32), pltpu.VMEM((2, hd, hd), F32)],
        compiler_params=_cparams(("parallel", "arbitrary"), 48),
        name="mlstm",
    )(z, z, z, zc, zc, gates_t, conv_w9, conv_b, w_qh, w_kh_t)


def _filt_body(z_ref, w1_ref, b1_ref, wh_ref, bh_ref, fr_ref, a_ref):
    fr = fr_ref[...]
    a = jnp.sin(fr[0:1] * (jnp.dot(z_ref[...], w1_ref[...], precision=HIGHEST, preferred_element_type=F32)
                           + b1_ref[...]))
    for i in range(2):
        a = jnp.sin(fr[i + 1:i + 2] * (jnp.dot(a, wh_ref[i], precision=HIGHEST, preferred_element_type=F32)
                                       + bh_ref[i:i + 1]))
    a_ref[...] = a


def _filt(feats, w1, b1, wh, bh, freq):
    length = feats.shape[0]
    fh = w1.shape[1]
    return pl.pallas_call(
        _filt_body,
        out_shape=jax.ShapeDtypeStruct((length, fh), F32),
        name="filt",
    )(feats, w1, b1, wh, bh, freq)


def _hyena_body(zx1_ref, zx2_ref, zv_ref, cw1_ref, cw2_ref, cwv_ref, cb1_ref, cb2_ref, cbv_ref,
                a_ref, wf_ref, dl_ref, t_ref, skip_ref, cos_ref, sin_ref, o_ref,
                hc_s, hs_s, hn_s, v_s, vb_s, zc_s, zs_s, g_s, *, seq):
    n_fft = 2 * seq
    mrows = 512
    n_m = seq // mrows
    sign = jnp.where(lax.broadcasted_iota(I32, (seq, 1), 0) % 2 == 0, 1.0, -1.0)
    row0 = lax.broadcasted_iota(I32, (seq, 1), 0) == 0

    @pl.when(pl.program_id(1) == 0)
    def _():
        window = jnp.exp(-t_ref[...] * dl_ref[...])
        for o in range(2):
            fwd = jnp.dot(a_ref[...], wf_ref[2 * o], precision=HIGHEST, preferred_element_type=F32) * window
            bwd = jnp.dot(a_ref[...], wf_ref[2 * o + 1], precision=HIGHEST, preferred_element_type=F32) * window
            bwd = jnp.where(row0, 0.0, bwd)
            even = fwd + bwd
            hc_s[o] = jnp.dot(cos_ref[...], even.astype(BF16), preferred_element_type=F32)
            hs_s[o] = jnp.dot(sin_ref[...], (fwd - bwd).astype(BF16), preferred_element_type=F32)
            hn_s[o] = jnp.sum(even * sign, axis=0, keepdims=True)

    def conv_to(dst_ref, z_ref, cw_ref, cb_ref):
        for lo in range(0, z_ref.shape[1], LANES):
            ls = slice(lo, lo + LANES)
            dst_ref[:, ls] = _dwconv(z_ref[:, ls], cw_ref[:, ls], cb_ref[:, ls], GRID_W, False)

    conv_to(v_s, zv_ref, cwv_ref, cbv_ref)

    for o, (zg_ref, cwg_ref, cbg_ref) in enumerate(((zx1_ref, cw1_ref, cb1_ref), (zx2_ref, cw2_ref, cb2_ref))):
        vb_s[...] = v_s[...].astype(BF16)
        x_nyq = jnp.sum(v_s[...] * sign, axis=0, keepdims=True)
        nyq = x_nyq * hn_s[o] * (1.0 / n_fft)

        def fwd_rows(m, carry, o=o):
            sl = pl.ds(pl.multiple_of(m * mrows, mrows), mrows)
            xc = jnp.dot(cos_ref[sl, :], vb_s[...], preferred_element_type=F32)
            xs = jnp.dot(sin_ref[sl, :], vb_s[...], preferred_element_type=F32)
            hc = hc_s[o, sl, :]
            hs = hs_s[o, sl, :]
            first = (lax.broadcasted_iota(I32, (mrows, 1), 0) + m * mrows) == 0
            scale = jnp.where(first, 1.0 / n_fft, 2.0 / n_fft)
            zc_s[sl, :] = (scale * (xc * hc - xs * hs)).astype(BF16)
            zs_s[sl, :] = (scale * (xc * hs + xs * hc)).astype(BF16)
            return carry
        lax.fori_loop(0, n_m, fwd_rows, 0)

        conv_to(g_s, zg_ref, cwg_ref, cbg_ref)

        def inv_rows(m, carry, o=o, nyq=nyq):
            sl = pl.ds(pl.multiple_of(m * mrows, mrows), mrows)
            y = (jnp.dot(cos_ref[sl, :], zc_s[...], preferred_element_type=F32)
                 + jnp.dot(sin_ref[sl, :], zs_s[...], preferred_element_type=F32))
            sgn = jnp.where((lax.broadcasted_iota(I32, (mrows, 1), 0) + m * mrows) % 2 == 0, 1.0, -1.0)
            v_s[sl, :] = g_s[sl, :] * (y + sgn * nyq + v_s[sl, :] * skip_ref[o:o + 1, :])
            return carry
        lax.fori_loop(0, n_m, inv_rows, 0)

    o_ref[...] = v_s[...].astype(BF16)


def _hyena(z, conv_w9, conv_b, a, w_fout4, deltas, tcol, skip, cos_m, sin_m, bsz, seq):
    ct = MXU
    n_ct = HYENA_WIDTH // ct
    hy = Z_HY // ct
    zspec = lambda off: pl.BlockSpec((seq, ct), lambda j, b: (b, hy + off * n_ct + j))
    wspec = lambda off: pl.BlockSpec((9, ct), lambda j, b: (0, off * n_ct + j))
    bspec = lambda off: pl.BlockSpec((1, ct), lambda j, b: (0, off * n_ct + j))
    fh = a.shape[1]
    return pl.pallas_call(
        functools.partial(_hyena_body, seq=seq),
        grid=(n_ct, bsz),
        in_specs=[zspec(0), zspec(1), zspec(2), wspec(0), wspec(1), wspec(2), bspec(0), bspec(1), bspec(2),
                  _const_spec((seq, fh)),
                  pl.BlockSpec((4, fh, ct), lambda j, b: (0, 0, j)),
                  pl.BlockSpec((1, ct), lambda j, b: (0, j)),
                  _const_spec((seq, 1)),
                  pl.BlockSpec((2, ct), lambda j, b: (0, j)),
                  _const_spec((seq, seq)), _const_spec((seq, seq))],
        out_specs=pl.BlockSpec((seq, ct), lambda j, b: (b, j)),
        out_shape=jax.ShapeDtypeStruct((bsz * seq, HYENA_WIDTH), BF16),
        scratch_shapes=[pltpu.VMEM((2, seq, ct), F32), pltpu.VMEM((2, seq, ct), F32), pltpu.VMEM((2, 1, ct), F32),
                        pltpu.VMEM((seq, ct), F32), pltpu.VMEM((seq, ct), BF16),
                        pltpu.VMEM((seq, ct), BF16), pltpu.VMEM((seq, ct), BF16), pltpu.VMEM((seq, ct), F32)],
        compiler_params=_cparams(("arbitrary", "arbitrary"), 60),
        name="hyena",
    )(z, z, z, conv_w9, conv_w9, conv_w9, conv_b, conv_b, conv_b, a, w_fout4, deltas, tcol, skip, cos_m, sin_m)


def _mix_body(oh_ref, hh_ref, ga_ref, gh_ref, wa_ref, wh_ref, m_ref):
    y_a = jnp.dot(oh_ref[...], wa_ref[...], preferred_element_type=F32)
    y_h = jnp.dot(hh_ref[...], wh_ref[...], preferred_element_type=F32)
    m_ref[...] = (ga_ref[...].astype(F32) * y_a + gh_ref[...].astype(F32) * y_h).astype(BF16)


def _mix(oh, hh, sg, w_a, w_h):
    t = oh.shape[0]
    dm = w_a.shape[1]
    tm = 512
    row = lambda i: (i, 0)
    return pl.pallas_call(
        _mix_body,
        grid=(t // tm,),
        in_specs=[pl.BlockSpec((tm, MLSTM_WIDTH), row), pl.BlockSpec((tm, HYENA_WIDTH), row),
                  pl.BlockSpec((tm, dm), row), pl.BlockSpec((tm, dm), lambda i: (i, 1)),
                  _const_spec((MLSTM_WIDTH, dm)), _const_spec((HYENA_WIDTH, dm))],
        out_specs=pl.BlockSpec((tm, dm), row),
        out_shape=jax.ShapeDtypeStruct((t, dm), BF16),
        compiler_params=_cparams(("parallel",), 48),
        name="mix",
    )(oh, hh, sg, sg, w_a, w_h)


def _merge_body(m_ref, x_ref, lng_ref, lnb_ref, g1_ref, l1g_ref, l1b_ref, sc2_ref, sh2_ref, wo_ref, wr_ref, br_ref,
                x1_ref, lg_ref):
    mix = jnp.dot(m_ref[...], wo_ref[...], preferred_element_type=F32)
    x0 = _layer_norm(x_ref[...], lng_ref[...], lnb_ref[...])
    x1 = _layer_norm(DEEPNORM_ALPHA * x0 + g1_ref[0] * mix, l1g_ref[...], l1b_ref[...])
    x1_ref[...] = x1
    tok = x1 * (1.0 + sc2_ref[0]) + sh2_ref[0]
    lg_ref[...] = jnp.dot(tok, wr_ref[...], precision=HIGHEST, preferred_element_type=F32) + br_ref[...]


def _merge(mix, x2d, ln_g, ln_b, g1, ln1_g, ln1_b, sc2, sh2, w_o, w_r, b_r, seq):
    t, dm = x2d.shape
    tm = 512
    per_b = seq // tm
    row = lambda i: (i, 0)
    mod = lambda i: (i // per_b, 0, 0)
    return pl.pallas_call(
        _merge_body,
        grid=(t // tm,),
        in_specs=[pl.BlockSpec((tm, dm), row), pl.BlockSpec((tm, dm), row),
                  _const_spec((1, dm)), _const_spec((1, dm)),
                  pl.BlockSpec((1, 1, dm), mod),
                  _const_spec((1, dm)), _const_spec((1, dm)),
                  pl.BlockSpec((1, 1, dm), mod), pl.BlockSpec((1, 1, dm), mod),
                  _const_spec((dm, dm)), _const_spec((dm, LANES)), _const_spec((1, LANES))],
        out_specs=[pl.BlockSpec((tm, dm), row), pl.BlockSpec((tm, LANES), row)],
        out_shape=[jax.ShapeDtypeStruct((t, dm), F32), jax.ShapeDtypeStruct((t, LANES), F32)],
        compiler_params=_cparams(("parallel",), 56),
        name="merge",
    )(mix, x2d, ln_g, ln_b, g1, ln1_g, ln1_b, sc2, sh2, w_o, w_r, b_r)


def _route_body(lg_ref, w_ref, d_ref, cnt_ref, run_s, tot_s, *, tr):
    phase = pl.program_id(0)

    @pl.when(pl.program_id(1) == 0)
    def _():
        @pl.when(phase == 1)
        def _():
            tot_s[...] = run_s[...]
        run_s[...] = jnp.zeros_like(run_s)

    lane = lax.broadcasted_iota(I32, (tr, LANES), 1)
    lane_f = lane.astype(F32)
    logit = lg_ref[...]
    hot, val = [], []
    for _ in range(TOP_K):
        mk = jnp.max(logit, axis=-1, keepdims=True)
        ik = jnp.min(jnp.where(logit == mk, lane_f, float(LANES)), axis=-1, keepdims=True)
        hk = lane_f == ik
        logit = jnp.where(hk, -jnp.inf, logit)
        hot.append(hk)
        val.append(mk)
    cnt = jnp.zeros((tr, LANES), F32)
    for hk in hot:
        cnt = cnt + jnp.where(hk, 1.0, 0.0)
    run = run_s[...] + jnp.sum(cnt, axis=0, keepdims=True)

    @pl.when(phase == 0)
    def _():
        cnt_ref[...] = run.astype(I32)

    @pl.when(phase == 1)
    def _():
        total = tot_s[...]
        padded = jnp.floor((total + (MOE_BLOCK - 1.0)) * (1.0 / MOE_BLOCK)) * MOE_BLOCK
        lane8 = lax.broadcasted_iota(I32, (8, LANES), 1)
        incl = padded
        sft = 1
        while sft < LANES:
            incl = incl + jnp.where(lane8 >= sft, pltpu.roll(incl, sft, 1), 0.0)
            sft *= 2
        pstart = (incl - padded)[0:1, :]
        lower = (lax.broadcasted_iota(I32, (tr, tr), 0) > lax.broadcasted_iota(I32, (tr, tr), 1))
        before = jnp.dot(jnp.where(lower, 1.0, 0.0).astype(BF16), cnt.astype(BF16),
                         preferred_element_type=F32) + (run_s[0:1, :] + pstart)
        ex = [jnp.exp(v - val[0]) for v in val]
        denom = ex[0] + ex[1] + ex[2] + ex[3]
        w_out = jnp.zeros((tr, LANES), F32)
        d_out = jnp.zeros((tr, LANES), I32)
        for k in range(TOP_K):
            dest = jnp.sum(jnp.where(hot[k], before, 0.0), axis=-1, keepdims=True)
            w_out = jnp.where(lane == k, ex[k] / denom, w_out)
            d_out = jnp.where(lane == k, dest.astype(I32), d_out)
        w_ref[...] = w_out
        d_ref[...] = d_out[:, :TOP_K]
        cnt_ref[...] = total.astype(I32)

    run_s[...] = run


def _route(logits):
    t = logits.shape[0]
    tr = 512
    row = lambda p, i: (i, 0)
    out_row = lambda p, i: (i * p, 0)
    return pl.pallas_call(
        functools.partial(_route_body, tr=tr),
        grid=(2, t // tr),
        in_specs=[pl.BlockSpec((tr, LANES), row)],
        out_specs=[pl.BlockSpec((tr, LANES), out_row), pl.BlockSpec((tr, TOP_K), out_row),
                   pl.BlockSpec((8, LANES), lambda p, i: (0, 0))],
        out_shape=[jax.ShapeDtypeStruct((t, LANES), F32), jax.ShapeDtypeStruct((t, TOP_K), I32),
                   jax.ShapeDtypeStruct((8, LANES), I32)],
        scratch_shapes=[pltpu.VMEM((8, LANES), F32), pltpu.VMEM((8, LANES), F32)],
        compiler_params=_cparams(("arbitrary", "arbitrary"), 32),
        name="route",
    )(logits)


def _scatter_body(cnt_ref, pstart_ref, used_ref, dest_ref, x1_ref, sc_ref, sh_ref, xb_ref, tok_ref, zero_s, sem, pad_sem,
                  *, ts, n_blocks):
    def row_copy(src, r_src, r_dst, s):
        return pltpu.make_async_copy(src.at[pl.ds(r_src, 1)], xb_ref.at[pl.ds(r_dst, 1)], s)

    def block_copy(blk):
        return pltpu.make_async_copy(zero_s, xb_ref.at[pl.ds(pl.multiple_of(blk * MOE_BLOCK, MOE_BLOCK), MOE_BLOCK)],
                                     pad_sem)

    @pl.when(pl.program_id(0) == 0)
    def _():
        zero_s[...] = jnp.zeros_like(zero_s)

        def per_expert(e, carry):
            cnt = cnt_ref[e]
            lo = pstart_ref[e] + cnt
            hi = pstart_ref[e] + (cnt + MOE_BLOCK - 1) // MOE_BLOCK * MOE_BLOCK

            def start(r, c):
                row_copy(zero_s, 0, r, pad_sem).start()
                return c

            def wait(r, c):
                row_copy(zero_s, 0, r, pad_sem).wait()
                return c
            lax.fori_loop(lo, hi, start, 0)
            lax.fori_loop(lo, hi, wait, 0)
            return carry
        lax.fori_loop(0, N_EXPERTS, per_expert, 0)

        def tail_start(blk, c):
            block_copy(blk).start()
            return c

        def tail_wait(blk, c):
            block_copy(blk).wait()
            return c
        lax.fori_loop(used_ref[0], n_blocks, tail_start, 0)
        lax.fori_loop(used_ref[0], n_blocks, tail_wait, 0)

    tok_ref[...] = x1_ref[...] * (1.0 + sc_ref[0]) + sh_ref[0]

    def row(r, carry):
        for k in range(TOP_K):
            row_copy(tok_ref, r, dest_ref[r * TOP_K + k], sem).start()
        return carry
    lax.fori_loop(0, ts, row, 0)
    for _ in range(TOP_K):
        pltpu.make_async_copy(tok_ref, xb_ref.at[pl.ds(0, ts)], sem).wait()


def _scatter(x1, sc2, sh2, dest_flat, counts, pstart, used, n_blocks, seq):
    t, dm = x1.shape
    ts = 256
    per_b = seq // ts
    grid_spec = pltpu.PrefetchScalarGridSpec(
        num_scalar_prefetch=3,
        grid=(t // ts,),
        in_specs=[pl.BlockSpec((ts * TOP_K,), lambda i, *_: (i,), memory_space=pltpu.SMEM),
                  pl.BlockSpec((ts, dm), lambda i, *_: (i, 0)),
                  pl.BlockSpec((1, 1, dm), lambda i, *_: (i // per_b, 0, 0)),
                  pl.BlockSpec((1, 1, dm), lambda i, *_: (i // per_b, 0, 0))],
        out_specs=pl.BlockSpec(memory_space=pl.ANY),
        scratch_shapes=[pltpu.VMEM((ts, dm), F32), pltpu.VMEM((MOE_BLOCK, dm), F32),
                        pltpu.SemaphoreType.DMA(()), pltpu.SemaphoreType.DMA(())],
    )
    return pl.pallas_call(
        functools.partial(_scatter_body, ts=ts, n_blocks=n_blocks),
        grid_spec=grid_spec,
        out_shape=jax.ShapeDtypeStruct((n_blocks * MOE_BLOCK, dm), F32),
        compiler_params=_cparams(("arbitrary",), 32),
        name="scatter",
    )(counts, pstart, used, dest_flat, x1, sc2, sh2)


def _ffn1_body(e_ref, c_ref, blk_ref, oblk_ref, oc_ref, first_ref, n_ref,
               x_ref, wg_ref, wu_ref, bg_ref, bu_ref, a_ref, wg_s, wu_s):
    s = pl.program_id(0)

    @pl.when(first_ref[s] == 1)
    def _():
        wg_s[...] = wg_ref[0].astype(BF16)
        wu_s[...] = wu_ref[0].astype(BF16)

    @pl.when(s < n_ref[0])
    def _():
        x = x_ref[...].astype(BF16)
        g = jnp.minimum(jnp.dot(x, wg_s[...], preferred_element_type=F32) + bg_ref[0], SWIGLU_LIMIT)
        u = jnp.clip(jnp.dot(x, wu_s[...], preferred_element_type=F32) + bu_ref[0], -SWIGLU_LIMIT, SWIGLU_LIMIT)
        a_ref[...] = (g * _sigmoid(SWIGLU_ALPHA * g) * (u + 1.0)).astype(BF16)

    @pl.when(s >= n_ref[0])
    def _():
        a_ref[...] = jnp.zeros_like(a_ref)


def _ffn2_body(e_ref, c_ref, blk_ref, oblk_ref, oc_ref, first_ref, n_ref, a_ref, wd_ref, bd_ref, y_ref, wd_s):
    s = pl.program_id(0)

    @pl.when(first_ref[s] == 1)
    def _():
        wd_s[...] = wd_ref[0].astype(BF16)

    @pl.when(s < n_ref[0])
    def _():
        y_ref[...] = jnp.dot(a_ref[...], wd_s[...], preferred_element_type=F32) + bd_ref[0]

    @pl.when(s >= n_ref[0])
    def _():
        y_ref[...] = jnp.zeros_like(y_ref)


def _ffn1(plan, xb, w_gate, w_up, b_gate, b_up, n_steps):
    rows, dm = xb.shape
    de = w_gate.shape[2]
    wspec = pl.BlockSpec((1, dm, FF_CHUNK), lambda s, e, c, *_: (e[s], 0, c[s]))
    bspec = pl.BlockSpec((1, 1, FF_CHUNK), lambda s, e, c, *_: (e[s], 0, c[s]))
    grid_spec = pltpu.PrefetchScalarGridSpec(
        num_scalar_prefetch=7,
        grid=(n_steps,),
        in_specs=[pl.BlockSpec((MOE_BLOCK, dm), lambda s, e, c, blk, *_: (blk[s], 0)), wspec, wspec, bspec, bspec],
        out_specs=pl.BlockSpec((MOE_BLOCK, FF_CHUNK), lambda s, e, c, blk, oblk, oc, *_: (oblk[s], oc[s])),
        scratch_shapes=[pltpu.VMEM((dm, FF_CHUNK), BF16), pltpu.VMEM((dm, FF_CHUNK), BF16)],
    )
    return pl.pallas_call(
        _ffn1_body, grid_spec=grid_spec,
        out_shape=jax.ShapeDtypeStruct((rows, de), BF16),
        compiler_params=_cparams(("arbitrary",), 60),
        name="ffn1",
    )(*plan, xb, w_gate, w_up, b_gate, b_up)


def _ffn2(plan, act, w_down, b_down, n_steps):
    rows, de = act.shape
    dm = w_down.shape[2]
    grid_spec = pltpu.PrefetchScalarGridSpec(
        num_scalar_prefetch=7,
        grid=(n_steps,),
        in_specs=[pl.BlockSpec((MOE_BLOCK, de), lambda s, e, c, blk, *_: (blk[s], 0)),
                  pl.BlockSpec((1, de, FF_CHUNK), lambda s, e, c, *_: (e[s], 0, c[s])),
                  pl.BlockSpec((1, 1, FF_CHUNK), lambda s, e, c, *_: (e[s], 0, c[s]))],
        out_specs=pl.BlockSpec((MOE_BLOCK, FF_CHUNK), lambda s, e, c, blk, oblk, oc, *_: (oblk[s], oc[s])),
        scratch_shapes=[pltpu.VMEM((de, FF_CHUNK), BF16)],
    )
    return pl.pallas_call(
        _ffn2_body, grid_spec=grid_spec,
        out_shape=jax.ShapeDtypeStruct((rows, dm), F32),
        compiler_params=_cparams(("arbitrary",), 48),
        name="ffn2",
    )(*plan, act, w_down, b_down)


def _combine_body(dcur_ref, dnxt_ref, w_ref, x1_ref, g2_ref, lg_ref, lb_ref, yb_ref, o_ref, buf, sem, *, tc, n_tiles):
    i = pl.program_id(0)

    def issue(d_ref, slot):
        def row(r, carry):
            for k in range(TOP_K):
                pltpu.make_async_copy(yb_ref.at[pl.ds(d_ref[r * TOP_K + k], 1)],
                                      buf.at[slot, k, pl.ds(r, 1)], sem.at[slot]).start()
            return carry
        lax.fori_loop(0, tc, row, 0)

    @pl.when(i == 0)
    def _():
        issue(dcur_ref, 0)

    @pl.when(i + 1 < n_tiles)
    def _():
        issue(dnxt_ref, (i + 1) % 2)

    slot = i % 2
    for k in range(TOP_K):
        pltpu.make_async_copy(yb_ref.at[pl.ds(0, tc)], buf.at[slot, k], sem.at[slot]).wait()
    w = w_ref[...]
    y = w[:, 0:1] * buf[slot, 0]
    for k in range(1, TOP_K):
        y = y + w[:, k:k + 1] * buf[slot, k]
    o_ref[...] = _layer_norm(DEEPNORM_ALPHA * x1_ref[...] + g2_ref[0] * y, lg_ref[...], lb_ref[...])


def _combine(dest_flat, w4, x1, g2, ln_g, ln_b, yb, seq):
    t, dm = x1.shape
    tc = 256
    n_tiles = t // tc
    per_b = seq // tc
    row = lambda i: (i, 0)
    return pl.pallas_call(
        functools.partial(_combine_body, tc=tc, n_tiles=n_tiles),
        grid=(n_tiles,),
        in_specs=[pl.BlockSpec((tc * TOP_K,), lambda i: (i,), memory_space=pltpu.SMEM),
                  pl.BlockSpec((tc * TOP_K,), lambda i: (jnp.minimum(i + 1, n_tiles - 1),), memory_space=pltpu.SMEM),
                  pl.BlockSpec((tc, LANES), row),
                  pl.BlockSpec((tc, dm), row),
                  pl.BlockSpec((1, 1, dm), lambda i: (i // per_b, 0, 0)),
                  _const_spec((1, dm)), _const_spec((1, dm)),
                  pl.BlockSpec(memory_space=pl.ANY)],
        out_specs=pl.BlockSpec((tc, dm), row),
        out_shape=jax.ShapeDtypeStruct((t, dm), F32),
        scratch_shapes=[pltpu.VMEM((2, TOP_K, tc, dm), F32), pltpu.SemaphoreType.DMA((2,))],
        compiler_params=_cparams(("arbitrary",), 40),
        name="combine",
    )(dest_flat, dest_flat, w4, x1, g2, ln_g, ln_b, yb)


def _dft_tables(seq):
    n = 2 * seq
    k = np.arange(seq, dtype=np.int64)
    ang = (2.0 * np.pi / n) * ((k[:, None] * k[None, :]) % n).astype(np.float64)
    return (jnp.asarray(np.cos(ang), F32).astype(BF16), jnp.asarray(np.sin(ang), F32).astype(BF16))


def _filter_features(seq):
    t = jnp.linspace(0.0, 1.0, seq, dtype=F32)[:, None]
    bands = (HYENA_EMB - 1) // 2
    f = jnp.linspace(1e-4, bands - 1, bands, dtype=F32)[None, :]
    ang = 2.0 * math.pi * jnp.arange(seq, dtype=F32)[:, None] * f / seq
    feats = jnp.concatenate([t, jnp.cos(ang), -jnp.sin(ang)], axis=-1)
    max_decay = math.log(HYENA_DECAY_TARGET) / HYENA_SHORT_DECAY_PCT
    min_decay = math.log(HYENA_DECAY_TARGET) / HYENA_LONG_DECAY_PCT
    deltas = jnp.abs(jnp.linspace(min_decay, max_decay, HYENA_WIDTH, dtype=F32))[None, :]
    return t, feats, deltas


def _moe_plan(counts, n_blocks, n_chunks):
    ids = jnp.arange(N_EXPERTS, dtype=I32)
    nblk = (counts + MOE_BLOCK - 1) // MOE_BLOCK
    blk_end = jnp.cumsum(nblk)
    blk_start = blk_end - nblk
    used = blk_end[-1]
    steps = n_chunks * nblk
    step_end = jnp.cumsum(steps)
    n_used = step_end[-1]
    s_all = jnp.arange(n_chunks * n_blocks, dtype=I32)
    s = jnp.minimum(s_all, n_used - 1)
    e_s = jnp.minimum(jnp.sum((s[:, None] >= step_end[None, :]).astype(I32), axis=1), N_EXPERTS - 1)
    onehot = e_s[:, None] == ids[None, :]
    pick = lambda table: jnp.sum(jnp.where(onehot, table[None, :], 0), axis=1)
    loc = s - pick(step_end - steps)
    nb = jnp.maximum(pick(nblk), 1)
    c_s = loc // nb
    r_s = loc % nb
    blk = pick(blk_start) + r_s
    tail = s_all >= n_used
    j = jnp.maximum(s_all - n_used, 0)
    n_tail = jnp.maximum(n_blocks - used, 1)
    oblk = jnp.where(tail, used + j % n_tail, blk)
    oc = jnp.where(tail, j // n_tail, c_s)
    first = jnp.logical_and(r_s == 0, jnp.logical_not(tail))
    as_i32 = lambda v: v.astype(I32)
    plan = tuple(map(as_i32, (e_s, c_s, blk, oblk, oc, first, n_used.reshape(1))))
    return as_i32(blk_start * MOE_BLOCK), as_i32(used.reshape(1)), plan


def kernel(x, c, ctx, c_ctx, ln_in_g, ln_in_b, w_mod, b_mod, w_in, b_in, mlstm_conv_w, mlstm_conv_b,
           w_qh, w_kh, hyena_conv_w, hyena_conv_b, filt_w1, filt_b1, filt_wh, filt_bh, filt_freq, filt_wout,
           hyena_skip, w_proj_a, w_proj_h, w_out, ln1_g, ln1_b, w_router, b_router, w_gate, b_gate,
           w_up, b_up, w_down, b_down, ln2_g, ln2_b):
    bsz, seq, dm = x.shape
    ctx_len = ctx.shape[1]
    t = bsz * seq
    assert w_mod.shape[0] == DEPTH and dm == D_MODEL and ctx_len == CHUNK and bsz + 1 <= 16
    row = lambda v: v.reshape(1, -1)

    cond = jnp.concatenate([c, c_ctx[None], jnp.zeros((16 - bsz - 1, dm), F32)], axis=0)
    mod = _mod(cond, w_mod[0], row(b_mod[0]))
    sh1, sc1, g1, sh2, sc2, g2 = [m[:, None, :] for m in jnp.split(mod, 6, axis=-1)]

    w_main = jnp.concatenate([w_in[0][:, :IN_GATES], w_in[0][:, IN_O:]], axis=1).astype(BF16)
    b_main = row(jnp.concatenate([b_in[0][:IN_GATES], b_in[0][IN_O:]]))
    w_g = jnp.pad(w_in[0][:, IN_GATES:IN_O], ((0, 0), (0, LANES - 4 * HEADS))).astype(BF16)
    b_g = row(jnp.pad(b_in[0][IN_GATES:IN_O], (0, LANES - 4 * HEADS)))
    lng, lnb = row(ln_in_g), row(ln_in_b)
    x2d = x.reshape(t, dm)
    z, gates, sg = _in_proj(x2d, lng, lnb, sc1[:bsz], sh1[:bsz], w_main, b_main, w_g, b_g, seq, Z_BG)
    zc, gates_c = _in_proj(ctx.reshape(bsz * ctx_len, dm), lng, lnb, sc1[bsz:bsz + 1], sh1[bsz:bsz + 1],
                           w_main[:, :IN_GATES], b_main[:, :IN_GATES], w_g, b_g, bsz * ctx_len, IN_GATES)

    g_all = jnp.concatenate([gates_c[:, :4 * HEADS].reshape(bsz, ctx_len, 4, HEADS),
                             gates[:, :4 * HEADS].reshape(bsz, seq, 4, HEADS)], axis=1)
    gates_t = g_all.transpose(0, 3, 2, 1)

    oh = _mlstm(z, zc, gates_t, mlstm_conv_w[0].reshape(9, MLSTM_WIDTH), row(mlstm_conv_b[0]),
                w_qh[0], w_kh[0].transpose(0, 2, 1), bsz, seq, ctx_len)

    tcol, feats, deltas = _filter_features(seq)
    feats = jnp.pad(feats, ((0, 0), (0, LANES - HYENA_EMB)))
    w1 = jnp.pad(filt_w1[0], ((0, LANES - HYENA_EMB), (0, 0)))
    a = _filt(feats, w1, row(filt_b1[0]), filt_wh[0], filt_bh[0], filt_freq[0])
    fh = a.shape[1]
    w_fout4 = filt_wout[0].reshape(fh, 4, HYENA_WIDTH).transpose(1, 0, 2)
    cos_m, sin_m = _dft_tables(seq)
    hh = _hyena(z, hyena_conv_w[0].reshape(9, 3 * HYENA_WIDTH), row(hyena_conv_b[0]), a, w_fout4, deltas, tcol,
                hyena_skip[0], cos_m, sin_m, bsz, seq)

    w_r = jnp.pad(w_router[0], ((0, 0), (0, LANES - N_EXPERTS)))
    b_r = row(jnp.pad(b_router[0], (0, LANES - N_EXPERTS), constant_values=-1e30))
    mix = _mix(oh, hh, sg, w_proj_a[0].astype(BF16), w_proj_h[0].astype(BF16))
    x1, logits = _merge(mix, x2d, lng, lnb, g1[:bsz], row(ln1_g[0]), row(ln1_b[0]), sc2[:bsz], sh2[:bsz],
                        w_out[0].astype(BF16), w_r, b_r, seq)

    w4, dest, counts = _route(logits)
    n_blocks = -(-(t * TOP_K + N_EXPERTS * (MOE_BLOCK - 1)) // MOE_BLOCK)
    n_chunks = dm // FF_CHUNK
    counts = counts[0, :N_EXPERTS]
    pstart, used, plan = _moe_plan(counts, n_blocks, n_chunks)
    dest_flat = dest.reshape(t * TOP_K)
    xb = _scatter(x1, sc2[:bsz], sh2[:bsz], dest_flat, counts, pstart, used, n_blocks, seq)
    n_steps = n_chunks * n_blocks
    act = _ffn1(plan, xb, w_gate[0], w_up[0], b_gate[0][:, None, :], b_up[0][:, None, :], n_steps)
    yb = _ffn2(plan, act, w_down[0], b_down[0][:, None, :], n_steps)
    out = _combine(dest_flat, w4, x1, g2[:bsz], row(ln2_g[0]), row(ln2_b[0]), yb, seq)
    return out.reshape(bsz, seq, dm)
```

```python
import functools
import math

import numpy as np
import jax
import jax.numpy as jnp
from jax import lax
from jax.experimental import pallas as pl
from jax.experimental.pallas import tpu as pltpu

F32 = jnp.float32
BF16 = jnp.bfloat16
I32 = jnp.int32
HIGHEST = lax.Precision.HIGHEST

D_MODEL = 2048
GRID_W = 64
HEADS = 4
HEAD_DIM = 256
MLSTM_WIDTH = HEADS * HEAD_DIM
HYENA_WIDTH = D_MODEL // 2
HYENA_EMB = 33
HYENA_DECAY_TARGET = 1e-2
HYENA_SHORT_DECAY_PCT = 0.3
HYENA_LONG_DECAY_PCT = 1.5
N_EXPERTS = 32
TOP_K = 4
SWIGLU_LIMIT = 7.0
SWIGLU_ALPHA = 1.702
LN_EPS = 1e-5
DEPTH = 1
DEEPNORM_ALPHA = (2.0 * DEPTH) ** 0.25

IN_V = MLSTM_WIDTH
IN_GATES = 2 * MLSTM_WIDTH
IN_O = IN_GATES + 4 * HEADS
Z_QK, Z_V, Z_O, Z_HY, Z_BG = 0, 1024, 2048, 3072, 6144

LANES = 128
MXU = 256
CHUNK = 256
MOE_BLOCK = 512
FF_CHUNK = 1024
MIB = 1024 * 1024


def _cparams(semantics, vmem_mib):
    return pltpu.CompilerParams(dimension_semantics=semantics, vmem_limit_bytes=vmem_mib * MIB)


def _const_spec(shape):
    nd = len(shape)
    return pl.BlockSpec(shape, lambda *_: (0,) * nd, pipeline_mode=pl.Buffered(1))


def _layer_norm(x, g, b):
    mu = jnp.mean(x, axis=-1, keepdims=True)
    xc = x - mu
    var = jnp.mean(xc * xc, axis=-1, keepdims=True)
    return xc * lax.rsqrt(var + LN_EPS) * g + b


def _sigmoid(x):
    return 1.0 / (1.0 + jnp.exp(-x))


def _silu(x):
    return x * _sigmoid(x)


def _log_sigmoid(x):
    return jnp.minimum(x, 0.0) - jnp.log(1.0 + jnp.exp(-jnp.abs(x)))


def _mod_body(c_ref, w_ref, b_ref, o_ref):
    s = _silu(c_ref[...])
    o_ref[...] = jnp.dot(s.astype(BF16), w_ref[...].astype(BF16), preferred_element_type=F32) + b_ref[...]


def _mod(cond, w, b):
    rows, dm = cond.shape
    n = w.shape[1]
    tn = 1024
    return pl.pallas_call(
        _mod_body,
        grid=(n // tn,),
        in_specs=[pl.BlockSpec((rows, dm), lambda j: (0, 0)),
                  pl.BlockSpec((dm, tn), lambda j: (0, j)),
                  pl.BlockSpec((1, tn), lambda j: (0, j))],
        out_specs=pl.BlockSpec((rows, tn), lambda j: (0, j)),
        out_shape=jax.ShapeDtypeStruct((rows, n), F32),
        compiler_params=_cparams(("arbitrary",), 40),
        name="mod",
    )(cond, w, b)


def _in_proj_body(x_ref, lng_ref, lnb_ref, sc_ref, sh_ref, w_ref, b_ref, wg_ref, bg_ref,
                  z_ref, g_ref, *rest, tm, n_plain):
    hx_s = rest[-1]

    @pl.when(pl.program_id(1) == 0)
    def _():
        def rows(r, carry):
            sl = pl.ds(pl.multiple_of(r * 128, 128), 128)
            xn = _layer_norm(x_ref[sl, :], lng_ref[...], lnb_ref[...])
            hx_s[sl, :] = (xn * (1.0 + sc_ref[0]) + sh_ref[0]).astype(BF16)
            return carry
        lax.fori_loop(0, tm // 128, rows, 0)
        g_ref[...] = jnp.dot(hx_s[...], wg_ref[...], preferred_element_type=F32) + bg_ref[...]

    acc = jnp.dot(hx_s[...], w_ref[...], preferred_element_type=F32) + b_ref[...]
    if len(rest) == 1:
        z_ref[...] = acc
    else:
        @pl.when(pl.program_id(1) < n_plain)
        def _():
            z_ref[...] = acc

        @pl.when(pl.program_id(1) >= n_plain)
        def _():
            rest[0][...] = _sigmoid(acc).astype(BF16)


def _in_proj(x2d, ln_g, ln_b, scale, shift, w, b, w_gates, b_gates, rows_per_mod, plain_cols):
    t, dm = x2d.shape
    n = w.shape[1]
    tm, tn = min(1024, t), 512
    n_plain = plain_cols // tn
    out_specs = [pl.BlockSpec((tm, tn), lambda i, j: (i, jnp.minimum(j, n_plain - 1))),
                 pl.BlockSpec((tm, LANES), lambda i, j: (i, 0))]
    out_shape = [jax.ShapeDtypeStruct((t, plain_cols), F32), jax.ShapeDtypeStruct((t, LANES), F32)]
    if n > plain_cols:
        out_specs.append(pl.BlockSpec((tm, tn), lambda i, j: (i, jnp.maximum(j - n_plain, 0))))
        out_shape.append(jax.ShapeDtypeStruct((t, n - plain_cols), BF16))
    return pl.pallas_call(
        functools.partial(_in_proj_body, tm=tm, n_plain=n_plain),
        grid=(t // tm, n // tn),
        in_specs=[pl.BlockSpec((tm, dm), lambda i, j: (i, 0)),
                  pl.BlockSpec((1, dm), lambda i, j: (0, 0)),
                  pl.BlockSpec((1, dm), lambda i, j: (0, 0)),
                  pl.BlockSpec((1, 1, dm), lambda i, j: (i * tm // rows_per_mod, 0, 0)),
                  pl.BlockSpec((1, 1, dm), lambda i, j: (i * tm // rows_per_mod, 0, 0)),
                  pl.BlockSpec((dm, tn), lambda i, j: (0, j)),
                  pl.BlockSpec((1, tn), lambda i, j: (0, j)),
                  pl.BlockSpec((dm, LANES), lambda i, j: (0, 0)),
                  pl.BlockSpec((1, LANES), lambda i, j: (0, 0))],
        out_specs=out_specs,
        out_shape=out_shape,
        scratch_shapes=[pltpu.VMEM((tm, dm), BF16)],
        compiler_params=_cparams(("parallel", "arbitrary"), 48),
        name="in_proj",
    )(x2d, ln_g, ln_b, scale, shift, w, b, w_gates, b_gates)


def _dwconv(u, w9, bias, width, single_row):
    length, ch = u.shape
    col = lax.broadcasted_iota(I32, (length, ch), 0) % width
    if not single_row:
        zpad = jnp.zeros((width, ch), F32)
        up = jnp.concatenate([zpad, u[:length - width]], axis=0)
        dn = jnp.concatenate([u[width:], zpad], axis=0)
    out = None
    for dc in (-1, 0, 1):
        a = u * w9[4 + dc:5 + dc]
        if not single_row:
            a = a + up * w9[1 + dc:2 + dc] + dn * w9[7 + dc:8 + dc]
        if dc == -1:
            a = jnp.where(col == 0, 0.0, pltpu.roll(a, 1, 0))
        elif dc == 1:
            a = jnp.where(col == width - 1, 0.0, pltpu.roll(a, length - 1, 0))
        out = a if out is None else out + a
    return out + bias


def _mlstm_body(zqk_ref, zv_ref, zo_ref, cqk_ref, cv_ref, gt_ref, cw_ref, cb_ref, wq_ref, wkt_ref,
                o_ref, q_s, k_s, kt_s, v_s, hf_s, hb_s, r_s, c_s, ct_s, *, ctx_len, seq):
    n_chunks = (ctx_len + seq) // CHUNK
    total = ctx_len + seq
    cw = cw_ref[...]
    cb = cb_ref[...]
    wq = wq_ref[0].astype(BF16)
    wkt = wkt_ref[0].astype(BF16)
    nt = (((1,), (1,)), ((), ()))
    scale = HEAD_DIM ** -0.5

    def project(u, off, n):
        ub = u.astype(BF16)
        q_s[off:off + n, :] = jnp.dot(ub, wq, preferred_element_type=F32).astype(BF16)
        k_s[off:off + n, :] = (lax.dot_general(ub, wkt, nt, preferred_element_type=F32) * scale).astype(BF16)
        kt_s[:, off:off + n] = (lax.dot_general(wkt, ub, nt, preferred_element_type=F32) * scale).astype(BF16)

    project(_silu(_dwconv(cqk_ref[...], cw, cb, ctx_len, True)), 0, ctx_len)
    project(_silu(_dwconv(zqk_ref[...], cw, cb, GRID_W, False)), ctx_len, seq)
    v_s[0:ctx_len, :] = cv_ref[...]
    v_s[ctx_len:total, :] = zv_ref[...]

    gt = gt_ref[0, 0]
    lf = _log_sigmoid(gt)
    pos = lax.broadcasted_iota(I32, (4, total), 1) % CHUNK
    pre, suf = lf, lf
    s = 1
    while s < CHUNK:
        pre = pre + jnp.where(pos >= s, pltpu.roll(pre, s, 1), 0.0)
        suf = suf + jnp.where(pos < CHUNK - s, pltpu.roll(suf, total - s, 1), 0.0)
        s *= 2
    b_f, li_f, b_b, li_b = pre[1:2], gt[0:1], suf[3:4], gt[2:3]
    pm_f, pm_b = li_f - b_f, li_b - b_b
    pos1 = pos[0:1]
    s = 1
    while s < CHUNK:
        pm_f = jnp.maximum(pm_f, jnp.where(pos1 >= s, pltpu.roll(pm_f, s, 1), -jnp.inf))
        pm_b = jnp.maximum(pm_b, jnp.where(pos1 < CHUNK - s, pltpu.roll(pm_b, total - s, 1), -jnp.inf))
        s *= 2
    r_s[...] = jnp.concatenate([b_f, li_f, pm_f, b_b, li_b, pm_b, jnp.zeros((2, total), F32)], axis=0)
    zfill = jnp.zeros((CHUNK - 8, CHUNK), F32)
    for c in range(n_chunks):
        blk = jnp.concatenate([r_s[:, c * CHUNK:(c + 1) * CHUNK], zfill], axis=0).T
        c_s[c * CHUNK:(c + 1) * CHUNK, :] = blk[:, :LANES]

    ct_s[...] = jnp.zeros_like(ct_s)
    row_i = lax.broadcasted_iota(I32, (CHUNK, CHUNK), 0)
    col_i = lax.broadcasted_iota(I32, (CHUNK, CHUNK), 1)
    masks = (row_i >= col_i, row_i <= col_i)

    def chunk_step(c, d, n_vec, m):
        off = c * CHUNK if isinstance(c, int) else pl.multiple_of(c * CHUNK, CHUNK)
        rows = r_s[:, pl.ds(off, CHUNK)]
        cols = c_s[pl.ds(off, CHUNK), :]
        b_row, li_row, pm_row = rows[3 * d:3 * d + 1], rows[3 * d + 1:3 * d + 2], rows[3 * d + 2:3 * d + 3]
        b_col, li_col, pm_col = cols[:, 3 * d:3 * d + 1], cols[:, 3 * d + 1:3 * d + 2], cols[:, 3 * d + 2:3 * d + 3]
        last = slice(CHUNK - 1, CHUNK) if d == 0 else slice(0, 1)
        b_end, pm_end = b_row[:, last], pm_row[:, last]
        qc = q_s[pl.ds(off, CHUNK), :]
        kc = k_s[pl.ds(off, CHUNK), :]
        ktc = kt_s[:, pl.ds(off, CHUNK)]
        vc = v_s[pl.ds(off, CHUNK), :]
        ct = ct_s[d]

        inter = b_col + m
        m_t = jnp.maximum(inter, b_col + pm_col)
        wts = jnp.exp(jnp.where(masks[d], b_col - b_row + li_row, -jnp.inf) - m_t)
        s_inter = jnp.exp(inter - m_t)
        scores = jnp.dot(qc, ktc, preferred_element_type=F32) * wts
        num = (s_inter * jnp.dot(qc, ct.astype(BF16), preferred_element_type=F32)
               + jnp.dot(scores.astype(BF16), vc.astype(BF16), preferred_element_type=F32))
        den = (s_inter * jnp.sum(qc.astype(F32) * n_vec, axis=-1, keepdims=True)
               + jnp.sum(scores, axis=-1, keepdims=True))
        h = num / jnp.maximum(jnp.abs(den), jnp.exp(-m_t))

        m_new = jnp.maximum(b_end + m, b_end + pm_end)
        decay = jnp.exp(b_end + m - m_new)
        w = jnp.exp(b_end - b_col + li_col - m_new)
        ct_s[d] = decay * ct + jnp.dot(ktc, (vc * w).astype(BF16), preferred_element_type=F32)
        n_new = decay * n_vec + jnp.sum(kc.astype(F32) * w, axis=0, keepdims=True)
        return h, n_new, m_new

    n0 = jnp.zeros((1, HEAD_DIM), F32)
    m0 = jnp.zeros((1, 1), F32)
    _, nf, mf = chunk_step(0, 0, n0, m0)
    _, nb, mb = chunk_step(0, 1, n0, m0)

    def body(i, carry):
        nf, mf, nb, mb = carry
        hf, nf, mf = chunk_step(i, 0, nf, mf)
        hf_s[pl.ds(pl.multiple_of(i * CHUNK - ctx_len, CHUNK), CHUNK), :] = hf
        j = n_chunks - i
        hb, nb, mb = chunk_step(j, 1, nb, mb)
        hb_s[pl.ds(pl.multiple_of(j * CHUNK - ctx_len, CHUNK), CHUNK), :] = hb
        return nf, mf, nb, mb

    lax.fori_loop(1, n_chunks, body, (nf, mf, nb, mb))
    o_ref[...] = (_sigmoid(zo_ref[...]) * (hf_s[...] + hb_s[...])).astype(BF16)


def _mlstm(z, zc, gates_t, conv_w9, conv_b, w_qh, w_kh_t, bsz, seq, ctx_len):
    total = ctx_len + seq
    hd = HEAD_DIM
    qk_blk, v_blk, o_blk = Z_QK // hd, Z_V // hd, Z_O // hd
    return pl.pallas_call(
        functools.partial(_mlstm_body, ctx_len=ctx_len, seq=seq),
        grid=(bsz, HEADS),
        in_specs=[pl.BlockSpec((seq, hd), lambda b, h: (b, qk_blk + h)),
                  pl.BlockSpec((seq, hd), lambda b, h: (b, v_blk + h)),
                  pl.BlockSpec((seq, hd), lambda b, h: (b, o_blk + h)),
                  pl.BlockSpec((ctx_len, hd), lambda b, h: (b, qk_blk + h)),
                  pl.BlockSpec((ctx_len, hd), lambda b, h: (b, v_blk + h)),
                  pl.BlockSpec((1, 1, 4, total), lambda b, h: (b, h, 0, 0)),
                  pl.BlockSpec((9, hd), lambda b, h: (0, h)),
                  pl.BlockSpec((1, hd), lambda b, h: (0, h)),
                  pl.BlockSpec((1, hd, hd), lambda b, h: (h, 0, 0)),
                  pl.BlockSpec((1, hd, hd), lambda b, h: (h, 0, 0))],
        out_specs=pl.BlockSpec((seq, hd), lambda b, h: (b, h)),
        out_shape=jax.ShapeDtypeStruct((bsz * seq, MLSTM_WIDTH), BF16),
        scratch_shapes=[pltpu.VMEM((total, hd), BF16), pltpu.VMEM((total, hd), BF16), pltpu.VMEM((hd, total), BF16),
                        pltpu.VMEM((total, hd), F32), pltpu.VMEM((seq, hd), F32), pltpu.VMEM((seq, hd), F32),
                        pltpu.VMEM((8, total), F32), pltpu.VMEM((total, LANES), F32), pltpu.VMEM((2, hd, hd), F32)],
        compiler_params=_cparams(("parallel", "arbitrary"), 48),
        name="mlstm",
    )(z, z, z, zc, zc, gates_t, conv_w9, conv_b, w_qh, w_kh_t)


def _filt_body(z_ref, w1_ref, b1_ref, wh_ref, bh_ref, fr_ref, a_ref):
    fr = fr_ref[...]
    a = jnp.sin(fr[0:1] * (jnp.dot(z_ref[...], w1_ref[...], precision=HIGHEST, preferred_element_type=F32)
                           + b1_ref[...]))
    for i in range(2):
        a = jnp.sin(fr[i + 1:i + 2] * (jnp.dot(a, wh_ref[i], precision=HIGHEST, preferred_element_type=F32)
                                       + bh_ref[i:i + 1]))
    a_ref[...] = a


def _filt(feats, w1, b1, wh, bh, freq):
    length = feats.shape[0]
    fh = w1.shape[1]
    return pl.pallas_call(
        _filt_body,
        out_shape=jax.ShapeDtypeStruct((length, fh), F32),
        name="filt",
    )(feats, w1, b1, wh, bh, freq)


def _hyena_body(zx1_ref, zx2_ref, zv_ref, cw1_ref, cw2_ref, cwv_ref, cb1_ref, cb2_ref, cbv_ref,
                a_ref, wf_ref, dl_ref, t_ref, skip_ref, cos_ref, sin_ref, o_ref,
                hc_s, hs_s, hn_s, v_s, vb_s, zc_s, zs_s, g_s, *, seq):
    n_fft = 2 * seq
    mrows = 512
    n_m = seq // mrows
    sign = jnp.where(lax.broadcasted_iota(I32, (seq, 1), 0) % 2 == 0, 1.0, -1.0)
    row0 = lax.broadcasted_iota(I32, (seq, 1), 0) == 0

    @pl.when(pl.program_id(1) == 0)
    def _():
        window = jnp.exp(-t_ref[...] * dl_ref[...])
        for o in range(2):
            fwd = jnp.dot(a_ref[...], wf_ref[2 * o], precision=HIGHEST, preferred_element_type=F32) * window
            bwd = jnp.dot(a_ref[...], wf_ref[2 * o + 1], precision=HIGHEST, preferred_element_type=F32) * window
            bwd = jnp.where(row0, 0.0, bwd)
            even = fwd + bwd
            hc_s[o] = jnp.dot(cos_ref[...], even.astype(BF16), preferred_element_type=F32)
            hs_s[o] = jnp.dot(sin_ref[...], (fwd - bwd).astype(BF16), preferred_element_type=F32)
            hn_s[o] = jnp.sum(even * sign, axis=0, keepdims=True)

    def conv_to(dst_ref, z_ref, cw_ref, cb_ref):
        for lo in range(0, z_ref.shape[1], LANES):
            ls = slice(lo, lo + LANES)
            dst_ref[:, ls] = _dwconv(z_ref[:, ls], cw_ref[:, ls], cb_ref[:, ls], GRID_W, False)

    conv_to(v_s, zv_ref, cwv_ref, cbv_ref)

    for o, (zg_ref, cwg_ref, cbg_ref) in enumerate(((zx1_ref, cw1_ref, cb1_ref), (zx2_ref, cw2_ref, cb2_ref))):
        vb_s[...] = v_s[...].astype(BF16)
        x_nyq = jnp.sum(v_s[...] * sign, axis=0, keepdims=True)
        nyq = x_nyq * hn_s[o] * (1.0 / n_fft)

        def fwd_rows(m, carry, o=o):
            sl = pl.ds(pl.multiple_of(m * mrows, mrows), mrows)
            xc = jnp.dot(cos_ref[sl, :], vb_s[...], preferred_element_type=F32)
            xs = jnp.dot(sin_ref[sl, :], vb_s[...], preferred_element_type=F32)
            hc = hc_s[o, sl, :]
            hs = hs_s[o, sl, :]
            first = (lax.broadcasted_iota(I32, (mrows, 1), 0) + m * mrows) == 0
            scale = jnp.where(first, 1.0 / n_fft, 2.0 / n_fft)
            zc_s[sl, :] = (scale * (xc * hc - xs * hs)).astype(BF16)
            zs_s[sl, :] = (scale * (xc * hs + xs * hc)).astype(BF16)
            return carry
        lax.fori_loop(0, n_m, fwd_rows, 0)

        conv_to(g_s, zg_ref, cwg_ref, cbg_ref)

        def inv_rows(m, carry, o=o, nyq=nyq):
            sl = pl.ds(pl.multiple_of(m * mrows, mrows), mrows)
            y = (jnp.dot(cos_ref[sl, :], zc_s[...], preferred_element_type=F32)
                 + jnp.dot(sin_ref[sl, :], zs_s[...], preferred_element_type=F32))
            sgn = jnp.where((lax.broadcasted_iota(I32, (mrows, 1), 0) + m * mrows) % 2 == 0, 1.0, -1.0)
            v_s[sl, :] = g_s[sl, :] * (y + sgn * nyq + v_s[sl, :] * skip_ref[o:o + 1, :])
            return carry
        lax.fori_loop(0, n_m, inv_rows, 0)

    o_ref[...] = v_s[...].astype(BF16)


def _hyena(z, conv_w9, conv_b, a, w_fout4, deltas, tcol, skip, cos_m, sin_m, bsz, seq):
    ct = MXU
    n_ct = HYENA_WIDTH // ct
    hy = Z_HY // ct
    zspec = lambda off: pl.BlockSpec((seq, ct), lambda j, b: (b, hy + off * n_ct + j))
    wspec = lambda off: pl.BlockSpec((9, ct), lambda j, b: (0, off * n_ct + j))
    bspec = lambda off: pl.BlockSpec((1, ct), lambda j, b: (0, off * n_ct + j))
    fh = a.shape[1]
    return pl.pallas_call(
        functools.partial(_hyena_body, seq=seq),
        grid=(n_ct, bsz),
        in_specs=[zspec(0), zspec(1), zspec(2), wspec(0), wspec(1), wspec(2), bspec(0), bspec(1), bspec(2),
                  _const_spec((seq, fh)),
                  pl.BlockSpec((4, fh, ct), lambda j, b: (0, 0, j)),
                  pl.BlockSpec((1, ct), lambda j, b: (0, j)),
                  _const_spec((seq, 1)),
                  pl.BlockSpec((2, ct), lambda j, b: (0, j)),
                  _const_spec((seq, seq)), _const_spec((seq, seq))],
        out_specs=pl.BlockSpec((seq, ct), lambda j, b: (b, j)),
        out_shape=jax.ShapeDtypeStruct((bsz * seq, HYENA_WIDTH), BF16),
        scratch_shapes=[pltpu.VMEM((2, seq, ct), F32), pltpu.VMEM((2, seq, ct), F32), pltpu.VMEM((2, 1, ct), F32),
                        pltpu.VMEM((seq, ct), F32), pltpu.VMEM((seq, ct), BF16),
                        pltpu.VMEM((seq, ct), BF16), pltpu.VMEM((seq, ct), BF16), pltpu.VMEM((seq, ct), F32)],
        compiler_params=_cparams(("arbitrary", "arbitrary"), 60),
        name="hyena",
    )(z, z, z, conv_w9, conv_w9, conv_w9, conv_b, conv_b, conv_b, a, w_fout4, deltas, tcol, skip, cos_m, sin_m)


def _mix_body(oh_ref, hh_ref, ga_ref, gh_ref, wa_ref, wh_ref, m_ref):
    y_a = jnp.dot(oh_ref[...], wa_ref[...], preferred_element_type=F32)
    y_h = jnp.dot(hh_ref[...], wh_ref[...], preferred_element_type=F32)
    m_ref[...] = (ga_ref[...].astype(F32) * y_a + gh_ref[...].astype(F32) * y_h).astype(BF16)


def _mix(oh, hh, sg, w_a, w_h):
    t = oh.shape[0]
    dm = w_a.shape[1]
    tm = 512
    row = lambda i: (i, 0)
    return pl.pallas_call(
        _mix_body,
        grid=(t // tm,),
        in_specs=[pl.BlockSpec((tm, MLSTM_WIDTH), row), pl.BlockSpec((tm, HYENA_WIDTH), row),
                  pl.BlockSpec((tm, dm), row), pl.BlockSpec((tm, dm), lambda i: (i, 1)),
                  _const_spec((MLSTM_WIDTH, dm)), _const_spec((HYENA_WIDTH, dm))],
        out_specs=pl.BlockSpec((tm, dm), row),
        out_shape=jax.ShapeDtypeStruct((t, dm), BF16),
        compiler_params=_cparams(("parallel",), 48),
        name="mix",
    )(oh, hh, sg, sg, w_a, w_h)


def _merge_body(m_ref, x_ref, lng_ref, lnb_ref, g1_ref, l1g_ref, l1b_ref, sc2_ref, sh2_ref, wo_ref, wr_ref, br_ref,
                x1_ref, lg_ref, mo_s, *, tm):
    mo_s[...] = jnp.dot(m_ref[...], wo_ref[...], preferred_element_type=F32)
    wr = wr_ref[...]
    wr_hi = wr.astype(BF16)
    wr_lo = (wr - wr_hi.astype(F32)).astype(BF16)

    def rows(r, carry):
        sl = pl.ds(pl.multiple_of(r * 128, 128), 128)
        x0 = _layer_norm(x_ref[sl, :], lng_ref[...], lnb_ref[...])
        x1 = _layer_norm(DEEPNORM_ALPHA * x0 + g1_ref[0] * mo_s[sl, :], l1g_ref[...], l1b_ref[...])
        x1_ref[sl, :] = x1
        tok = x1 * (1.0 + sc2_ref[0]) + sh2_ref[0]
        t_hi = tok.astype(BF16)
        t_lo = (tok - t_hi.astype(F32)).astype(BF16)
        lg_ref[sl, :] = (jnp.dot(t_hi, wr_hi, preferred_element_type=F32)
                         + jnp.dot(t_lo, wr_hi, preferred_element_type=F32)
                         + jnp.dot(t_hi, wr_lo, preferred_element_type=F32) + br_ref[...])
        return carry
    lax.fori_loop(0, tm // 128, rows, 0)


def _merge(mix, x2d, ln_g, ln_b, g1, ln1_g, ln1_b, sc2, sh2, w_o, w_r, b_r, seq):
    t, dm = x2d.shape
    tm = 512
    per_b = seq // tm
    row = lambda i: (i, 0)
    mod = lambda i: (i // per_b, 0, 0)
    return pl.pallas_call(
        functools.partial(_merge_body, tm=tm),
        grid=(t // tm,),
        scratch_shapes=[pltpu.VMEM((tm, dm), F32)],
        in_specs=[pl.BlockSpec((tm, dm), row), pl.BlockSpec((tm, dm), row),
                  _const_spec((1, dm)), _const_spec((1, dm)),
                  pl.BlockSpec((1, 1, dm), mod),
                  _const_spec((1, dm)), _const_spec((1, dm)),
                  pl.BlockSpec((1, 1, dm), mod), pl.BlockSpec((1, 1, dm), mod),
                  _const_spec((dm, dm)), _const_spec((dm, LANES)), _const_spec((1, LANES))],
        out_specs=[pl.BlockSpec((tm, dm), row), pl.BlockSpec((tm, LANES), row)],
        out_shape=[jax.ShapeDtypeStruct((t, dm), F32), jax.ShapeDtypeStruct((t, LANES), F32)],
        compiler_params=_cparams(("parallel",), 56),
        name="merge",
    )(mix, x2d, ln_g, ln_b, g1, ln1_g, ln1_b, sc2, sh2, w_o, w_r, b_r)


def _route_body(lg_ref, w_ref, d_ref, cnt_ref, run_s, tot_s, *, tr):
    phase = pl.program_id(0)

    @pl.when(pl.program_id(1) == 0)
    def _():
        @pl.when(phase == 1)
        def _():
            tot_s[...] = run_s[...]
        run_s[...] = jnp.zeros_like(run_s)

    lane = lax.broadcasted_iota(I32, (tr, LANES), 1)
    lane_f = lane.astype(F32)
    logit = lg_ref[...]
    hot, val = [], []
    for _ in range(TOP_K):
        mk = jnp.max(logit, axis=-1, keepdims=True)
        ik = jnp.min(jnp.where(logit == mk, lane_f, float(LANES)), axis=-1, keepdims=True)
        hk = lane_f == ik
        logit = jnp.where(hk, -jnp.inf, logit)
        hot.append(hk)
        val.append(mk)
    cnt = jnp.zeros((tr, LANES), F32)
    for hk in hot:
        cnt = cnt + jnp.where(hk, 1.0, 0.0)
    run = run_s[...] + jnp.sum(cnt, axis=0, keepdims=True)

    @pl.when(phase == 0)
    def _():
        cnt_ref[...] = run.astype(I32)

    @pl.when(phase == 1)
    def _():
        total = tot_s[...]
        padded = jnp.floor((total + (MOE_BLOCK - 1.0)) * (1.0 / MOE_BLOCK)) * MOE_BLOCK
        lane8 = lax.broadcasted_iota(I32, (8, LANES), 1)
        incl = padded
        sft = 1
        while sft < LANES:
            incl = incl + jnp.where(lane8 >= sft, pltpu.roll(incl, sft, 1), 0.0)
            sft *= 2
        pstart = (incl - padded)[0:1, :]
        lower = (lax.broadcasted_iota(I32, (tr, tr), 0) > lax.broadcasted_iota(I32, (tr, tr), 1))
        before = jnp.dot(jnp.where(lower, 1.0, 0.0).astype(BF16), cnt.astype(BF16),
                         preferred_element_type=F32) + (run_s[0:1, :] + pstart)
        ex = [jnp.exp(v - val[0]) for v in val]
        denom = ex[0] + ex[1] + ex[2] + ex[3]
        w_out = jnp.zeros((tr, LANES), F32)
        d_out = jnp.zeros((tr, LANES), I32)
        for k in range(TOP_K):
            dest = jnp.sum(jnp.where(hot[k], before, 0.0), axis=-1, keepdims=True)
            w_out = jnp.where(lane == k, ex[k] / denom, w_out)
            d_out = jnp.where(lane == k, dest.astype(I32), d_out)
        w_ref[...] = w_out
        d_ref[...] = d_out[:, :TOP_K]
        cnt_ref[...] = total.astype(I32)

    run_s[...] = run


def _route(logits):
    t = logits.shape[0]
    tr = 512
    row = lambda p, i: (i, 0)
    out_row = lambda p, i: (i * p, 0)
    return pl.pallas_call(
        functools.partial(_route_body, tr=tr),
        grid=(2, t // tr),
        in_specs=[pl.BlockSpec((tr, LANES), row)],
        out_specs=[pl.BlockSpec((tr, LANES), out_row), pl.BlockSpec((tr, TOP_K), out_row),
                   pl.BlockSpec((8, LANES), lambda p, i: (0, 0))],
        out_shape=[jax.ShapeDtypeStruct((t, LANES), F32), jax.ShapeDtypeStruct((t, TOP_K), I32),
                   jax.ShapeDtypeStruct((8, LANES), I32)],
        scratch_shapes=[pltpu.VMEM((8, LANES), F32), pltpu.VMEM((8, LANES), F32)],
        compiler_params=_cparams(("arbitrary", "arbitrary"), 32),
        name="route",
    )(logits)


def _scatter_body(cnt_ref, pstart_ref, used_ref, dest_ref, x1_ref, sc_ref, sh_ref, xb_ref, tok_ref, zero_s, sem, pad_sem,
                  *, ts, n_blocks):
    def row_copy(src, r_src, r_dst, s):
        return pltpu.make_async_copy(src.at[pl.ds(r_src, 1)], xb_ref.at[pl.ds(r_dst, 1)], s)

    def block_copy(blk):
        return pltpu.make_async_copy(zero_s, xb_ref.at[pl.ds(pl.multiple_of(blk * MOE_BLOCK, MOE_BLOCK), MOE_BLOCK)],
                                     pad_sem)

    @pl.when(pl.program_id(0) == 0)
    def _():
        zero_s[...] = jnp.zeros_like(zero_s)

        def per_expert(e, carry):
            cnt = cnt_ref[e]
            lo = pstart_ref[e] + cnt
            hi = pstart_ref[e] + (cnt + MOE_BLOCK - 1) // MOE_BLOCK * MOE_BLOCK

            def start(r, c):
                row_copy(zero_s, 0, r, pad_sem).start()
                return c

            def wait(r, c):
                row_copy(zero_s, 0, r, pad_sem).wait()
                return c
            lax.fori_loop(lo, hi, start, 0)
            lax.fori_loop(lo, hi, wait, 0)
            return carry
        lax.fori_loop(0, N_EXPERTS, per_expert, 0)

        def tail_start(blk, c):
            block_copy(blk).start()
            return c

        def tail_wait(blk, c):
            block_copy(blk).wait()
            return c
        lax.fori_loop(used_ref[0], n_blocks, tail_start, 0)
        lax.fori_loop(used_ref[0], n_blocks, tail_wait, 0)

    tok_ref[...] = x1_ref[...] * (1.0 + sc_ref[0]) + sh_ref[0]

    def row(r, carry):
        for k in range(TOP_K):
            row_copy(tok_ref, r, dest_ref[r * TOP_K + k], sem).start()
        return carry
    lax.fori_loop(0, ts, row, 0)
    for _ in range(TOP_K):
        pltpu.make_async_copy(tok_ref, xb_ref.at[pl.ds(0, ts)], sem).wait()


def _scatter(x1, sc2, sh2, dest_flat, counts, pstart, used, n_blocks, seq):
    t, dm = x1.shape
    ts = 256
    per_b = seq // ts
    grid_spec = pltpu.PrefetchScalarGridSpec(
        num_scalar_prefetch=3,
        grid=(t // ts,),
        in_specs=[pl.BlockSpec((ts * TOP_K,), lambda i, *_: (i,), memory_space=pltpu.SMEM),
                  pl.BlockSpec((ts, dm), lambda i, *_: (i, 0)),
                  pl.BlockSpec((1, 1, dm), lambda i, *_: (i // per_b, 0, 0)),
                  pl.BlockSpec((1, 1, dm), lambda i, *_: (i // per_b, 0, 0))],
        out_specs=pl.BlockSpec(memory_space=pl.ANY),
        scratch_shapes=[pltpu.VMEM((ts, dm), F32), pltpu.VMEM((MOE_BLOCK, dm), F32),
                        pltpu.SemaphoreType.DMA(()), pltpu.SemaphoreType.DMA(())],
    )
    return pl.pallas_call(
        functools.partial(_scatter_body, ts=ts, n_blocks=n_blocks),
        grid_spec=grid_spec,
        out_shape=jax.ShapeDtypeStruct((n_blocks * MOE_BLOCK, dm), F32),
        compiler_params=_cparams(("arbitrary",), 32),
        name="scatter",
    )(counts, pstart, used, dest_flat, x1, sc2, sh2)


def _ffn1_body(e_ref, c_ref, blk_ref, oblk_ref, oc_ref, first_ref, n_ref,
               x_ref, wg_ref, wu_ref, bg_ref, bu_ref, a_ref, wg_s, wu_s):
    s = pl.program_id(0)

    @pl.when(first_ref[s] == 1)
    def _():
        wg_s[...] = wg_ref[0].astype(BF16)
        wu_s[...] = wu_ref[0].astype(BF16)

    @pl.when(s < n_ref[0])
    def _():
        x = x_ref[...].astype(BF16)
        g = jnp.minimum(jnp.dot(x, wg_s[...], preferred_element_type=F32) + bg_ref[0], SWIGLU_LIMIT)
        u = jnp.clip(jnp.dot(x, wu_s[...], preferred_element_type=F32) + bu_ref[0], -SWIGLU_LIMIT, SWIGLU_LIMIT)
        a_ref[...] = (g * _sigmoid(SWIGLU_ALPHA * g) * (u + 1.0)).astype(BF16)

    @pl.when(s >= n_ref[0])
    def _():
        a_ref[...] = jnp.zeros_like(a_ref)


def _ffn2_body(e_ref, c_ref, blk_ref, oblk_ref, oc_ref, first_ref, n_ref, a_ref, wd_ref, bd_ref, y_ref, wd_s):
    s = pl.program_id(0)

    @pl.when(first_ref[s] == 1)
    def _():
        wd_s[...] = wd_ref[0].astype(BF16)

    @pl.when(s < n_ref[0])
    def _():
        y_ref[...] = jnp.dot(a_ref[...], wd_s[...], preferred_element_type=F32) + bd_ref[0]

    @pl.when(s >= n_ref[0])
    def _():
        y_ref[...] = jnp.zeros_like(y_ref)


def _ffn1(plan, xb, w_gate, w_up, b_gate, b_up, n_steps):
    rows, dm = xb.shape
    de = w_gate.shape[2]
    wspec = pl.BlockSpec((1, dm, FF_CHUNK), lambda s, e, c, *_: (e[s], 0, c[s]))
    bspec = pl.BlockSpec((1, 1, FF_CHUNK), lambda s, e, c, *_: (e[s], 0, c[s]))
    grid_spec = pltpu.PrefetchScalarGridSpec(
        num_scalar_prefetch=7,
        grid=(n_steps,),
        in_specs=[pl.BlockSpec((MOE_BLOCK, dm), lambda s, e, c, blk, *_: (blk[s], 0)), wspec, wspec, bspec, bspec],
        out_specs=pl.BlockSpec((MOE_BLOCK, FF_CHUNK), lambda s, e, c, blk, oblk, oc, *_: (oblk[s], oc[s])),
        scratch_shapes=[pltpu.VMEM((dm, FF_CHUNK), BF16), pltpu.VMEM((dm, FF_CHUNK), BF16)],
    )
    return pl.pallas_call(
        _ffn1_body, grid_spec=grid_spec,
        out_shape=jax.ShapeDtypeStruct((rows, de), BF16),
        compiler_params=_cparams(("arbitrary",), 60),
        name="ffn1",
    )(*plan, xb, w_gate, w_up, b_gate, b_up)


def _ffn2(plan, act, w_down, b_down, n_steps, chunk):
    rows, de = act.shape
    dm = w_down.shape[2]
    grid_spec = pltpu.PrefetchScalarGridSpec(
        num_scalar_prefetch=7,
        grid=(n_steps,),
        in_specs=[pl.BlockSpec((MOE_BLOCK, de), lambda s, e, c, blk, *_: (blk[s], 0)),
                  pl.BlockSpec((1, de, chunk), lambda s, e, c, *_: (e[s], 0, c[s])),
                  pl.BlockSpec((1, 1, chunk), lambda s, e, c, *_: (e[s], 0, c[s]))],
        out_specs=pl.BlockSpec((MOE_BLOCK, chunk), lambda s, e, c, blk, oblk, oc, *_: (oblk[s], oc[s])),
        scratch_shapes=[pltpu.VMEM((de, chunk), BF16)],
    )
    return pl.pallas_call(
        _ffn2_body, grid_spec=grid_spec,
        out_shape=jax.ShapeDtypeStruct((rows, dm), F32),
        compiler_params=_cparams(("arbitrary",), 60),
        name="ffn2",
    )(*plan, act, w_down, b_down)


def _combine_body(dcur_ref, dnxt_ref, w_ref, x1_ref, g2_ref, lg_ref, lb_ref, yb_ref, o_ref, buf, sem, *, tc, n_tiles):
    i = pl.program_id(0)

    def issue(d_ref, slot):
        def row(r, carry):
            for k in range(TOP_K):
                pltpu.make_async_copy(yb_ref.at[pl.ds(d_ref[r * TOP_K + k], 1)],
                                      buf.at[slot, k, pl.ds(r, 1)], sem.at[slot]).start()
            return carry
        lax.fori_loop(0, tc, row, 0)

    @pl.when(i == 0)
    def _():
        issue(dcur_ref, 0)

    @pl.when(i + 1 < n_tiles)
    def _():
        issue(dnxt_ref, (i + 1) % 2)

    slot = i % 2
    for k in range(TOP_K):
        pltpu.make_async_copy(yb_ref.at[pl.ds(0, tc)], buf.at[slot, k], sem.at[slot]).wait()
    w = w_ref[...]
    y = w[:, 0:1] * buf[slot, 0]
    for k in range(1, TOP_K):
        y = y + w[:, k:k + 1] * buf[slot, k]
    o_ref[...] = _layer_norm(DEEPNORM_ALPHA * x1_ref[...] + g2_ref[0] * y, lg_ref[...], lb_ref[...])


def _combine(dest_flat, w4, x1, g2, ln_g, ln_b, yb, seq):
    t, dm = x1.shape
    tc = 256
    n_tiles = t // tc
    per_b = seq // tc
    row = lambda i: (i, 0)
    return pl.pallas_call(
        functools.partial(_combine_body, tc=tc, n_tiles=n_tiles),
        grid=(n_tiles,),
        in_specs=[pl.BlockSpec((tc * TOP_K,), lambda i: (i,), memory_space=pltpu.SMEM),
                  pl.BlockSpec((tc * TOP_K,), lambda i: (jnp.minimum(i + 1, n_tiles - 1),), memory_space=pltpu.SMEM),
                  pl.BlockSpec((tc, LANES), row),
                  pl.BlockSpec((tc, dm), row),
                  pl.BlockSpec((1, 1, dm), lambda i: (i // per_b, 0, 0)),
                  _const_spec((1, dm)), _const_spec((1, dm)),
                  pl.BlockSpec(memory_space=pl.ANY)],
        out_specs=pl.BlockSpec((tc, dm), row),
        out_shape=jax.ShapeDtypeStruct((t, dm), F32),
        scratch_shapes=[pltpu.VMEM((2, TOP_K, tc, dm), F32), pltpu.SemaphoreType.DMA((2,))],
        compiler_params=_cparams(("arbitrary",), 40),
        name="combine",
    )(dest_flat, dest_flat, w4, x1, g2, ln_g, ln_b, yb)


def _dft_tables(seq):
    n = 2 * seq
    k = np.arange(seq, dtype=np.int64)
    ang = (2.0 * np.pi / n) * ((k[:, None] * k[None, :]) % n).astype(np.float64)
    return (jnp.asarray(np.cos(ang), F32).astype(BF16), jnp.asarray(np.sin(ang), F32).astype(BF16))


def _filter_features(seq):
    t = jnp.linspace(0.0, 1.0, seq, dtype=F32)[:, None]
    bands = (HYENA_EMB - 1) // 2
    f = jnp.linspace(1e-4, bands - 1, bands, dtype=F32)[None, :]
    ang = 2.0 * math.pi * jnp.arange(seq, dtype=F32)[:, None] * f / seq
    feats = jnp.concatenate([t, jnp.cos(ang), -jnp.sin(ang)], axis=-1)
    max_decay = math.log(HYENA_DECAY_TARGET) / HYENA_SHORT_DECAY_PCT
    min_decay = math.log(HYENA_DECAY_TARGET) / HYENA_LONG_DECAY_PCT
    deltas = jnp.abs(jnp.linspace(min_decay, max_decay, HYENA_WIDTH, dtype=F32))[None, :]
    return t, feats, deltas


def _moe_plan(counts, n_blocks, n_chunks):
    ids = jnp.arange(N_EXPERTS, dtype=I32)
    nblk = (counts + MOE_BLOCK - 1) // MOE_BLOCK
    blk_end = jnp.cumsum(nblk)
    blk_start = blk_end - nblk
    used = blk_end[-1]
    steps = n_chunks * nblk
    step_end = jnp.cumsum(steps)
    n_used = step_end[-1]
    s_all = jnp.arange(n_chunks * n_blocks, dtype=I32)
    s = jnp.minimum(s_all, n_used - 1)
    e_s = jnp.minimum(jnp.sum((s[:, None] >= step_end[None, :]).astype(I32), axis=1), N_EXPERTS - 1)
    onehot = e_s[:, None] == ids[None, :]
    pick = lambda table: jnp.sum(jnp.where(onehot, table[None, :], 0), axis=1)
    loc = s - pick(step_end - steps)
    nb = jnp.maximum(pick(nblk), 1)
    c_s = loc // nb
    r_s = loc % nb
    blk = pick(blk_start) + r_s
    tail = s_all >= n_used
    j = jnp.maximum(s_all - n_used, 0)
    n_tail = jnp.maximum(n_blocks - used, 1)
    oblk = jnp.where(tail, used + j % n_tail, blk)
    oc = jnp.where(tail, j // n_tail, c_s)
    first = jnp.logical_and(r_s == 0, jnp.logical_not(tail))
    as_i32 = lambda v: v.astype(I32)
    plan = tuple(map(as_i32, (e_s, c_s, blk, oblk, oc, first, n_used.reshape(1))))
    return as_i32(blk_start * MOE_BLOCK), as_i32(used.reshape(1)), plan


def kernel(x, c, ctx, c_ctx, ln_in_g, ln_in_b, w_mod, b_mod, w_in, b_in, mlstm_conv_w, mlstm_conv_b,
           w_qh, w_kh, hyena_conv_w, hyena_conv_b, filt_w1, filt_b1, filt_wh, filt_bh, filt_freq, filt_wout,
           hyena_skip, w_proj_a, w_proj_h, w_out, ln1_g, ln1_b, w_router, b_router, w_gate, b_gate,
           w_up, b_up, w_down, b_down, ln2_g, ln2_b):
    bsz, seq, dm = x.shape
    ctx_len = ctx.shape[1]
    t = bsz * seq
    assert w_mod.shape[0] == DEPTH and dm == D_MODEL and ctx_len == CHUNK and bsz + 1 <= 16
    row = lambda v: v.reshape(1, -1)

    cond = jnp.concatenate([c, c_ctx[None], jnp.zeros((16 - bsz - 1, dm), F32)], axis=0)
    mod = _mod(cond, w_mod[0], row(b_mod[0]))
    sh1, sc1, g1, sh2, sc2, g2 = [m[:, None, :] for m in jnp.split(mod, 6, axis=-1)]

    w_main = jnp.concatenate([w_in[0][:, :IN_GATES], w_in[0][:, IN_O:]], axis=1).astype(BF16)
    b_main = row(jnp.concatenate([b_in[0][:IN_GATES], b_in[0][IN_O:]]))
    w_g = jnp.pad(w_in[0][:, IN_GATES:IN_O], ((0, 0), (0, LANES - 4 * HEADS))).astype(BF16)
    b_g = row(jnp.pad(b_in[0][IN_GATES:IN_O], (0, LANES - 4 * HEADS)))
    lng, lnb = row(ln_in_g), row(ln_in_b)
    x2d = x.reshape(t, dm)
    z, gates, sg = _in_proj(x2d, lng, lnb, sc1[:bsz], sh1[:bsz], w_main, b_main, w_g, b_g, seq, Z_BG)
    zc, gates_c = _in_proj(ctx.reshape(bsz * ctx_len, dm), lng, lnb, sc1[bsz:bsz + 1], sh1[bsz:bsz + 1],
                           w_main[:, :IN_GATES], b_main[:, :IN_GATES], w_g, b_g, bsz * ctx_len, IN_GATES)

    g_all = jnp.concatenate([gates_c[:, :4 * HEADS].reshape(bsz, ctx_len, 4, HEADS),
                             gates[:, :4 * HEADS].reshape(bsz, seq, 4, HEADS)], axis=1)
    gates_t = g_all.transpose(0, 3, 2, 1)

    oh = _mlstm(z, zc, gates_t, mlstm_conv_w[0].reshape(9, MLSTM_WIDTH), row(mlstm_conv_b[0]),
                w_qh[0], w_kh[0].transpose(0, 2, 1), bsz, seq, ctx_len)

    tcol, feats, deltas = _filter_features(seq)
    feats = jnp.pad(feats, ((0, 0), (0, LANES - HYENA_EMB)))
    w1 = jnp.pad(filt_w1[0], ((0, LANES - HYENA_EMB), (0, 0)))
    a = _filt(feats, w1, row(filt_b1[0]), filt_wh[0], filt_bh[0], filt_freq[0])
    fh = a.shape[1]
    w_fout4 = filt_wout[0].reshape(fh, 4, HYENA_WIDTH).transpose(1, 0, 2)
    cos_m, sin_m = _dft_tables(seq)
    hh = _hyena(z, hyena_conv_w[0].reshape(9, 3 * HYENA_WIDTH), row(hyena_conv_b[0]), a, w_fout4, deltas, tcol,
                hyena_skip[0], cos_m, sin_m, bsz, seq)

    w_r = jnp.pad(w_router[0], ((0, 0), (0, LANES - N_EXPERTS)))
    b_r = row(jnp.pad(b_router[0], (0, LANES - N_EXPERTS), constant_values=-1e30))
    mix = _mix(oh, hh, sg, w_proj_a[0].astype(BF16), w_proj_h[0].astype(BF16))
    x1, logits = _merge(mix, x2d, lng, lnb, g1[:bsz], row(ln1_g[0]), row(ln1_b[0]), sc2[:bsz], sh2[:bsz],
                        w_out[0].astype(BF16), w_r, b_r, seq)

    w4, dest, counts = _route(logits)
    n_blocks = -(-(t * TOP_K + N_EXPERTS * (MOE_BLOCK - 1)) // MOE_BLOCK)
    n_chunks = dm // FF_CHUNK
    counts = counts[0, :N_EXPERTS]
    pstart, used, plan = _moe_plan(counts, n_blocks, n_chunks)
    dest_flat = dest.reshape(t * TOP_K)
    xb = _scatter(x1, sc2[:bsz], sh2[:bsz], dest_flat, counts, pstart, used, n_blocks, seq)
    act = _ffn1(plan, xb, w_gate[0], w_up[0], b_gate[0][:, None, :], b_up[0][:, None, :], n_chunks * n_blocks)
    _, _, plan_down = _moe_plan(counts, n_blocks, 1)
    yb = _ffn2(plan_down, act, w_down[0], b_down[0][:, None, :], n_blocks, dm)
    out = _combine(dest_flat, w4, x1, g2[:bsz], row(ln2_g[0]), row(ln2_b[0]), yb, seq)
    return out.reshape(bsz, seq, dm)
```

```python
import functools
import math

import numpy as np
import jax
import jax.numpy as jnp
from jax import lax
from jax.experimental import pallas as pl
from jax.experimental.pallas import tpu as pltpu

F32 = jnp.float32
BF16 = jnp.bfloat16
I32 = jnp.int32
HIGHEST = lax.Precision.HIGHEST

D_MODEL = 2048
GRID_W = 64
HEADS = 4
HEAD_DIM = 256
MLSTM_WIDTH = HEADS * HEAD_DIM
HYENA_WIDTH = D_MODEL // 2
HYENA_EMB = 33
HYENA_DECAY_TARGET = 1e-2
HYENA_SHORT_DECAY_PCT = 0.3
HYENA_LONG_DECAY_PCT = 1.5
N_EXPERTS = 32
TOP_K = 4
SWIGLU_LIMIT = 7.0
SWIGLU_ALPHA = 1.702
LN_EPS = 1e-5
DEPTH = 1
DEEPNORM_ALPHA = (2.0 * DEPTH) ** 0.25

IN_V = MLSTM_WIDTH
IN_GATES = 2 * MLSTM_WIDTH
IN_O = IN_GATES + 4 * HEADS
Z_QK, Z_V, Z_O, Z_HY, Z_BG = 0, 1024, 2048, 3072, 6144

LANES = 128
MXU = 256
CHUNK = 256
MOE_BLOCK = 512
FF_CHUNK = 1024
ROW_SLABS = D_MODEL // LANES
MIB = 1024 * 1024


def _cparams(semantics, vmem_mib):
    return pltpu.CompilerParams(dimension_semantics=semantics, vmem_limit_bytes=vmem_mib * MIB)


def _const_spec(shape):
    nd = len(shape)
    return pl.BlockSpec(shape, lambda *_: (0,) * nd, pipeline_mode=pl.Buffered(1))


def _slab_rows(r):
    start = r * ROW_SLABS
    return pl.ds(start if isinstance(r, int) else pl.multiple_of(start, ROW_SLABS), ROW_SLABS)


def _layer_norm(x, g, b):
    mu = jnp.mean(x, axis=-1, keepdims=True)
    xc = x - mu
    var = jnp.mean(xc * xc, axis=-1, keepdims=True)
    return xc * lax.rsqrt(var + LN_EPS) * g + b


def _sigmoid(x):
    return 1.0 / (1.0 + jnp.exp(-x))


def _silu(x):
    return x * _sigmoid(x)


def _log_sigmoid(x):
    return jnp.minimum(x, 0.0) - jnp.log(1.0 + jnp.exp(-jnp.abs(x)))


def _mod_body(c_ref, w_ref, b_ref, o_ref):
    s = _silu(c_ref[...])
    o_ref[...] = jnp.dot(s.astype(BF16), w_ref[...].astype(BF16), preferred_element_type=F32) + b_ref[...]


def _mod(cond, w, b):
    rows, dm = cond.shape
    n = w.shape[1]
    tn = 1024
    return pl.pallas_call(
        _mod_body,
        grid=(n // tn,),
        in_specs=[pl.BlockSpec((rows, dm), lambda j: (0, 0)),
                  pl.BlockSpec((dm, tn), lambda j: (0, j)),
                  pl.BlockSpec((1, tn), lambda j: (0, j))],
        out_specs=pl.BlockSpec((rows, tn), lambda j: (0, j)),
        out_shape=jax.ShapeDtypeStruct((rows, n), F32),
        compiler_params=_cparams(("arbitrary",), 40),
        name="mod",
    )(cond, w, b)


def _in_proj_body(x_ref, lng_ref, lnb_ref, sc_ref, sh_ref, w_ref, b_ref, wg_ref, bg_ref,
                  z_ref, g_ref, *rest, tm, n_plain):
    hx_s = rest[-1]

    @pl.when(pl.program_id(1) == 0)
    def _():
        def rows(r, carry):
            sl = pl.ds(pl.multiple_of(r * 128, 128), 128)
            xn = _layer_norm(x_ref[sl, :], lng_ref[...], lnb_ref[...])
            hx_s[sl, :] = (xn * (1.0 + sc_ref[0]) + sh_ref[0]).astype(BF16)
            return carry
        lax.fori_loop(0, tm // 128, rows, 0)
        g_ref[...] = jnp.dot(hx_s[...], wg_ref[...], preferred_element_type=F32) + bg_ref[...]

    acc = jnp.dot(hx_s[...], w_ref[...], preferred_element_type=F32) + b_ref[...]
    if len(rest) == 1:
        z_ref[...] = acc
    else:
        @pl.when(pl.program_id(1) < n_plain)
        def _():
            z_ref[...] = acc

        @pl.when(pl.program_id(1) >= n_plain)
        def _():
            rest[0][...] = _sigmoid(acc).astype(BF16)


def _in_proj(x2d, ln_g, ln_b, scale, shift, w, b, w_gates, b_gates, rows_per_mod, plain_cols):
    t, dm = x2d.shape
    n = w.shape[1]
    tm, tn = min(1024, t), 1024
    n_plain = plain_cols // tn
    out_specs = [pl.BlockSpec((tm, tn), lambda i, j: (i, jnp.minimum(j, n_plain - 1))),
                 pl.BlockSpec((tm, LANES), lambda i, j: (i, 0))]
    out_shape = [jax.ShapeDtypeStruct((t, plain_cols), F32), jax.ShapeDtypeStruct((t, LANES), F32)]
    if n > plain_cols:
        out_specs.append(pl.BlockSpec((tm, tn), lambda i, j: (i, jnp.maximum(j - n_plain, 0))))
        out_shape.append(jax.ShapeDtypeStruct((t, n - plain_cols), BF16))
    return pl.pallas_call(
        functools.partial(_in_proj_body, tm=tm, n_plain=n_plain),
        grid=(t // tm, n // tn),
        in_specs=[pl.BlockSpec((tm, dm), lambda i, j: (i, 0)),
                  pl.BlockSpec((1, dm), lambda i, j: (0, 0)),
                  pl.BlockSpec((1, dm), lambda i, j: (0, 0)),
                  pl.BlockSpec((1, 1, dm), lambda i, j: (i * tm // rows_per_mod, 0, 0)),
                  pl.BlockSpec((1, 1, dm), lambda i, j: (i * tm // rows_per_mod, 0, 0)),
                  pl.BlockSpec((dm, tn), lambda i, j: (0, j)),
                  pl.BlockSpec((1, tn), lambda i, j: (0, j)),
                  pl.BlockSpec((dm, LANES), lambda i, j: (0, 0)),
                  pl.BlockSpec((1, LANES), lambda i, j: (0, 0))],
        out_specs=out_specs,
        out_shape=out_shape,
        scratch_shapes=[pltpu.VMEM((tm, dm), BF16)],
        compiler_params=_cparams(("parallel", "arbitrary"), 48),
        name="in_proj",
    )(x2d, ln_g, ln_b, scale, shift, w, b, w_gates, b_gates)


def _dwconv(u, w9, bias, width, single_row):
    length, ch = u.shape
    col = lax.broadcasted_iota(I32, (length, ch), 0) % width
    if not single_row:
        zpad = jnp.zeros((width, ch), F32)
        up = jnp.concatenate([zpad, u[:length - width]], axis=0)
        dn = jnp.concatenate([u[width:], zpad], axis=0)
    out = None
    for dc in (-1, 0, 1):
        a = u * w9[4 + dc:5 + dc]
        if not single_row:
            a = a + up * w9[1 + dc:2 + dc] + dn * w9[7 + dc:8 + dc]
        if dc == -1:
            a = jnp.where(col == 0, 0.0, pltpu.roll(a, 1, 0))
        elif dc == 1:
            a = jnp.where(col == width - 1, 0.0, pltpu.roll(a, length - 1, 0))
        out = a if out is None else out + a
    return out + bias


def _mlstm_body(zqk_ref, zv_ref, zo_ref, cqk_ref, cv_ref, gt_ref, cw_ref, cb_ref, wq_ref, wkt_ref,
                o_ref, q_s, k_s, kt_s, v_s, hf_s, hb_s, r_s, c_s, ct_s, *, ctx_len, seq):
    n_chunks = (ctx_len + seq) // CHUNK
    total = ctx_len + seq
    cw = cw_ref[...]
    cb = cb_ref[...]
    wq = wq_ref[0].astype(BF16)
    wkt = wkt_ref[0].astype(BF16)
    nt = (((1,), (1,)), ((), ()))
    scale = HEAD_DIM ** -0.5

    def project(u, off, n):
        ub = u.astype(BF16)
        q_s[off:off + n, :] = jnp.dot(ub, wq, preferred_element_type=F32).astype(BF16)
        k_s[off:off + n, :] = (lax.dot_general(ub, wkt, nt, preferred_element_type=F32) * scale).astype(BF16)
        kt_s[:, off:off + n] = (lax.dot_general(wkt, ub, nt, preferred_element_type=F32) * scale).astype(BF16)

    project(_silu(_dwconv(cqk_ref[...], cw, cb, ctx_len, True)), 0, ctx_len)
    project(_silu(_dwconv(zqk_ref[...], cw, cb, GRID_W, False)), ctx_len, seq)
    v_s[0:ctx_len, :] = cv_ref[...]
    v_s[ctx_len:total, :] = zv_ref[...]

    gt = gt_ref[0, 0]
    lf = _log_sigmoid(gt)
    pos = lax.broadcasted_iota(I32, (4, total), 1) % CHUNK
    pre, suf = lf, lf
    s = 1
    while s < CHUNK:
        pre = pre + jnp.where(pos >= s, pltpu.roll(pre, s, 1), 0.0)
        suf = suf + jnp.where(pos < CHUNK - s, pltpu.roll(suf, total - s, 1), 0.0)
        s *= 2
    b_f, li_f, b_b, li_b = pre[1:2], gt[0:1], suf[3:4], gt[2:3]
    pm_f, pm_b = li_f - b_f, li_b - b_b
    pos1 = pos[0:1]
    s = 1
    while s < CHUNK:
        pm_f = jnp.maximum(pm_f, jnp.where(pos1 >= s, pltpu.roll(pm_f, s, 1), -jnp.inf))
        pm_b = jnp.maximum(pm_b, jnp.where(pos1 < CHUNK - s, pltpu.roll(pm_b, total - s, 1), -jnp.inf))
        s *= 2
    r_s[...] = jnp.concatenate([b_f, li_f, pm_f, b_b, li_b, pm_b, jnp.zeros((2, total), F32)], axis=0)
    zfill = jnp.zeros((CHUNK - 8, CHUNK), F32)
    for c in range(n_chunks):
        blk = jnp.concatenate([r_s[:, c * CHUNK:(c + 1) * CHUNK], zfill], axis=0).T
        c_s[c * CHUNK:(c + 1) * CHUNK, :] = blk[:, :LANES]

    ct_s[...] = jnp.zeros_like(ct_s)
    row_i = lax.broadcasted_iota(I32, (CHUNK, CHUNK), 0)
    col_i = lax.broadcasted_iota(I32, (CHUNK, CHUNK), 1)
    masks = (row_i >= col_i, row_i <= col_i)

    def chunk_step(c, d, n_vec, m):
        off = c * CHUNK if isinstance(c, int) else pl.multiple_of(c * CHUNK, CHUNK)
        rows = r_s[:, pl.ds(off, CHUNK)]
        cols = c_s[pl.ds(off, CHUNK), :]
        b_row, li_row, pm_row = rows[3 * d:3 * d + 1], rows[3 * d + 1:3 * d + 2], rows[3 * d + 2:3 * d + 3]
        b_col, li_col, pm_col = cols[:, 3 * d:3 * d + 1], cols[:, 3 * d + 1:3 * d + 2], cols[:, 3 * d + 2:3 * d + 3]
        last = slice(CHUNK - 1, CHUNK) if d == 0 else slice(0, 1)
        b_end, pm_end = b_row[:, last], pm_row[:, last]
        qc = q_s[pl.ds(off, CHUNK), :]
        kc = k_s[pl.ds(off, CHUNK), :]
        ktc = kt_s[:, pl.ds(off, CHUNK)]
        vc = v_s[pl.ds(off, CHUNK), :]
        ct = ct_s[d]

        inter = b_col + m
        m_t = jnp.maximum(inter, b_col + pm_col)
        wts = jnp.exp(jnp.where(masks[d], b_col - b_row + li_row, -jnp.inf) - m_t)
        s_inter = jnp.exp(inter - m_t)
        scores = jnp.dot(qc, ktc, preferred_element_type=F32) * wts
        num = (s_inter * jnp.dot(qc, ct.astype(BF16), preferred_element_type=F32)
               + jnp.dot(scores.astype(BF16), vc.astype(BF16), preferred_element_type=F32))
        den = (s_inter * jnp.sum(qc.astype(F32) * n_vec, axis=-1, keepdims=True)
               + jnp.sum(scores, axis=-1, keepdims=True))
        h = num / jnp.maximum(jnp.abs(den), jnp.exp(-m_t))

        m_new = jnp.maximum(b_end + m, b_end + pm_end)
        decay = jnp.exp(b_end + m - m_new)
        w = jnp.exp(b_end - b_col + li_col - m_new)
        ct_s[d] = decay * ct + jnp.dot(ktc, (vc * w).astype(BF16), preferred_element_type=F32)
        n_new = decay * n_vec + jnp.sum(kc.astype(F32) * w, axis=0, keepdims=True)
        return h, n_new, m_new

    n0 = jnp.zeros((1, HEAD_DIM), F32)
    m0 = jnp.zeros((1, 1), F32)
    _, nf, mf = chunk_step(0, 0, n0, m0)
    _, nb, mb = chunk_step(0, 1, n0, m0)

    def body(i, carry):
        nf, mf, nb, mb = carry
        hf, nf, mf = chunk_step(i, 0, nf, mf)
        hf_s[pl.ds(pl.multiple_of(i * CHUNK - ctx_len, CHUNK), CHUNK), :] = hf
        j = n_chunks - i
        hb, nb, mb = chunk_step(j, 1, nb, mb)
        hb_s[pl.ds(pl.multiple_of(j * CHUNK - ctx_len, CHUNK), CHUNK), :] = hb
        return nf, mf, nb, mb

    lax.fori_loop(1, n_chunks, body, (nf, mf, nb, mb))
    o_ref[...] = (_sigmoid(zo_ref[...]) * (hf_s[...] + hb_s[...])).astype(BF16)


def _mlstm(z, zc, gates_t, conv_w9, conv_b, w_qh, w_kh_t, bsz, seq, ctx_len):
    total = ctx_len + seq
    hd = HEAD_DIM
    qk_blk, v_blk, o_blk = Z_QK // hd, Z_V // hd, Z_O // hd
    return pl.pallas_call(
        functools.partial(_mlstm_body, ctx_len=ctx_len, seq=seq),
        grid=(bsz, HEADS),
        in_specs=[pl.BlockSpec((seq, hd), lambda b, h: (b, qk_blk + h)),
                  pl.BlockSpec((seq, hd), lambda b, h: (b, v_blk + h)),
                  pl.BlockSpec((seq, hd), lambda b, h: (b, o_blk + h)),
                  pl.BlockSpec((ctx_len, hd), lambda b, h: (b, qk_blk + h)),
                  pl.BlockSpec((ctx_len, hd), lambda b, h: (b, v_blk + h)),
                  pl.BlockSpec((1, 1, 4, total), lambda b, h: (b, h, 0, 0)),
                  pl.BlockSpec((9, hd), lambda b, h: (0, h)),
                  pl.BlockSpec((1, hd), lambda b, h: (0, h)),
                  pl.BlockSpec((1, hd, hd), lambda b, h: (h, 0, 0)),
                  pl.BlockSpec((1, hd, hd), lambda b, h: (h, 0, 0))],
        out_specs=pl.BlockSpec((seq, hd), lambda b, h: (b, h)),
        out_shape=jax.ShapeDtypeStruct((bsz * seq, MLSTM_WIDTH), BF16),
        scratch_shapes=[pltpu.VMEM((total, hd), BF16), pltpu.VMEM((total, hd), BF16), pltpu.VMEM((hd, total), BF16),
                        pltpu.VMEM((total, hd), F32), pltpu.VMEM((seq, hd), F32), pltpu.VMEM((seq, hd), F32),
                        pltpu.VMEM((8, total), F32), pltpu.VMEM((total, LANES), F32), pltpu.VMEM((2, hd, hd), F32)],
        compiler_params=_cparams(("parallel", "arbitrary"), 48),
        name="mlstm",
    )(z, z, z, zc, zc, gates_t, conv_w9, conv_b, w_qh, w_kh_t)


def _filt_body(z_ref, w1_ref, b1_ref, wh_ref, bh_ref, fr_ref, a_ref):
    fr = fr_ref[...]
    a = jnp.sin(fr[0:1] * (jnp.dot(z_ref[...], w1_ref[...], precision=HIGHEST, preferred_element_type=F32)
                           + b1_ref[...]))
    for i in range(2):
        a = jnp.sin(fr[i + 1:i + 2] * (jnp.dot(a, wh_ref[i], precision=HIGHEST, preferred_element_type=F32)
                                       + bh_ref[i:i + 1]))
    a_ref[...] = a


def _filt(feats, w1, b1, wh, bh, freq):
    length = feats.shape[0]
    fh = w1.shape[1]
    return pl.pallas_call(
        _filt_body,
        out_shape=jax.ShapeDtypeStruct((length, fh), F32),
        name="filt",
    )(feats, w1, b1, wh, bh, freq)


def _hyena_body(zx1_ref, zx2_ref, zv_ref, cw1_ref, cw2_ref, cwv_ref, cb1_ref, cb2_ref, cbv_ref,
                a_ref, wf_ref, dl_ref, t_ref, skip_ref, cos_ref, sin_ref, o_ref,
                hc_s, hs_s, hn_s, v_s, vb_s, zc_s, zs_s, g_s, *, seq):
    n_fft = 2 * seq
    mrows = 512
    n_m = seq // mrows
    sign = jnp.where(lax.broadcasted_iota(I32, (seq, 1), 0) % 2 == 0, 1.0, -1.0)
    row0 = lax.broadcasted_iota(I32, (seq, 1), 0) == 0

    @pl.when(pl.program_id(1) == 0)
    def _():
        window = jnp.exp(-t_ref[...] * dl_ref[...])
        for o in range(2):
            fwd = jnp.dot(a_ref[...], wf_ref[2 * o], precision=HIGHEST, preferred_element_type=F32) * window
            bwd = jnp.dot(a_ref[...], wf_ref[2 * o + 1], precision=HIGHEST, preferred_element_type=F32) * window
            bwd = jnp.where(row0, 0.0, bwd)
            even = fwd + bwd
            hc_s[o] = jnp.dot(cos_ref[...], even.astype(BF16), preferred_element_type=F32)
            hs_s[o] = jnp.dot(sin_ref[...], (fwd - bwd).astype(BF16), preferred_element_type=F32)
            hn_s[o] = jnp.sum(even * sign, axis=0, keepdims=True)

    def conv_to(dst_ref, z_ref, cw_ref, cb_ref):
        for lo in range(0, z_ref.shape[1], LANES):
            ls = slice(lo, lo + LANES)
            dst_ref[:, ls] = _dwconv(z_ref[:, ls], cw_ref[:, ls], cb_ref[:, ls], GRID_W, False)

    conv_to(v_s, zv_ref, cwv_ref, cbv_ref)

    for o, (zg_ref, cwg_ref, cbg_ref) in enumerate(((zx1_ref, cw1_ref, cb1_ref), (zx2_ref, cw2_ref, cb2_ref))):
        vb_s[...] = v_s[...].astype(BF16)
        x_nyq = jnp.sum(v_s[...] * sign, axis=0, keepdims=True)
        nyq = x_nyq * hn_s[o] * (1.0 / n_fft)

        def fwd_rows(m, carry, o=o):
            sl = pl.ds(pl.multiple_of(m * mrows, mrows), mrows)
            xc = jnp.dot(cos_ref[sl, :], vb_s[...], preferred_element_type=F32)
            xs = jnp.dot(sin_ref[sl, :], vb_s[...], preferred_element_type=F32)
            hc = hc_s[o, sl, :]
            hs = hs_s[o, sl, :]
            first = (lax.broadcasted_iota(I32, (mrows, 1), 0) + m * mrows) == 0
            scale = jnp.where(first, 1.0 / n_fft, 2.0 / n_fft)
            zc_s[sl, :] = (scale * (xc * hc - xs * hs)).astype(BF16)
            zs_s[sl, :] = (scale * (xc * hs + xs * hc)).astype(BF16)
            return carry
        lax.fori_loop(0, n_m, fwd_rows, 0)

        conv_to(g_s, zg_ref, cwg_ref, cbg_ref)

        def inv_rows(m, carry, o=o, nyq=nyq):
            sl = pl.ds(pl.multiple_of(m * mrows, mrows), mrows)
            y = (jnp.dot(cos_ref[sl, :], zc_s[...], preferred_element_type=F32)
                 + jnp.dot(sin_ref[sl, :], zs_s[...], preferred_element_type=F32))
            sgn = jnp.where((lax.broadcasted_iota(I32, (mrows, 1), 0) + m * mrows) % 2 == 0, 1.0, -1.0)
            v_s[sl, :] = g_s[sl, :] * (y + sgn * nyq + v_s[sl, :] * skip_ref[o:o + 1, :])
            return carry
        lax.fori_loop(0, n_m, inv_rows, 0)

    o_ref[...] = v_s[...].astype(BF16)


def _hyena(z, conv_w9, conv_b, a, w_fout4, deltas, tcol, skip, cos_m, sin_m, bsz, seq):
    ct = MXU
    n_ct = HYENA_WIDTH // ct
    hy = Z_HY // ct
    zspec = lambda off: pl.BlockSpec((seq, ct), lambda j, b: (b, hy + off * n_ct + j))
    wspec = lambda off: pl.BlockSpec((9, ct), lambda j, b: (0, off * n_ct + j))
    bspec = lambda off: pl.BlockSpec((1, ct), lambda j, b: (0, off * n_ct + j))
    fh = a.shape[1]
    return pl.pallas_call(
        functools.partial(_hyena_body, seq=seq),
        grid=(n_ct, bsz),
        in_specs=[zspec(0), zspec(1), zspec(2), wspec(0), wspec(1), wspec(2), bspec(0), bspec(1), bspec(2),
                  _const_spec((seq, fh)),
                  pl.BlockSpec((4, fh, ct), lambda j, b: (0, 0, j)),
                  pl.BlockSpec((1, ct), lambda j, b: (0, j)),
                  _const_spec((seq, 1)),
                  pl.BlockSpec((2, ct), lambda j, b: (0, j)),
                  _const_spec((seq, seq)), _const_spec((seq, seq))],
        out_specs=pl.BlockSpec((seq, ct), lambda j, b: (b, j)),
        out_shape=jax.ShapeDtypeStruct((bsz * seq, HYENA_WIDTH), BF16),
        scratch_shapes=[pltpu.VMEM((2, seq, ct), F32), pltpu.VMEM((2, seq, ct), F32), pltpu.VMEM((2, 1, ct), F32),
                        pltpu.VMEM((seq, ct), F32), pltpu.VMEM((seq, ct), BF16),
                        pltpu.VMEM((seq, ct), BF16), pltpu.VMEM((seq, ct), BF16), pltpu.VMEM((seq, ct), F32)],
        compiler_params=_cparams(("arbitrary", "arbitrary"), 60),
        name="hyena",
    )(z, z, z, conv_w9, conv_w9, conv_w9, conv_b, conv_b, conv_b, a, w_fout4, deltas, tcol, skip, cos_m, sin_m)


def _mix_body(oh_ref, hh_ref, ga_ref, gh_ref, wa_ref, wh_ref, m_ref):
    y_a = jnp.dot(oh_ref[...], wa_ref[...], preferred_element_type=F32)
    y_h = jnp.dot(hh_ref[...], wh_ref[...], preferred_element_type=F32)
    m_ref[...] = (ga_ref[...].astype(F32) * y_a + gh_ref[...].astype(F32) * y_h).astype(BF16)


def _mix(oh, hh, sg, w_a, w_h):
    t = oh.shape[0]
    dm = w_a.shape[1]
    tm = 512
    row = lambda i: (i, 0)
    return pl.pallas_call(
        _mix_body,
        grid=(t // tm,),
        in_specs=[pl.BlockSpec((tm, MLSTM_WIDTH), row), pl.BlockSpec((tm, HYENA_WIDTH), row),
                  pl.BlockSpec((tm, dm), row), pl.BlockSpec((tm, dm), lambda i: (i, 1)),
                  _const_spec((MLSTM_WIDTH, dm)), _const_spec((HYENA_WIDTH, dm))],
        out_specs=pl.BlockSpec((tm, dm), row),
        out_shape=jax.ShapeDtypeStruct((t, dm), BF16),
        compiler_params=_cparams(("parallel",), 48),
        name="mix",
    )(oh, hh, sg, sg, w_a, w_h)


def _merge_body(m_ref, x_ref, lng_ref, lnb_ref, g1_ref, l1g_ref, l1b_ref, sc2_ref, sh2_ref, wo_ref, wr_ref, br_ref,
                x1_ref, lg_ref, mo_s, *, tm):
    mo_s[...] = jnp.dot(m_ref[...], wo_ref[...], preferred_element_type=F32)
    wr = wr_ref[...]
    wr_hi = wr.astype(BF16)
    wr_lo = (wr - wr_hi.astype(F32)).astype(BF16)

    def rows(r, carry):
        sl = pl.ds(pl.multiple_of(r * 128, 128), 128)
        x0 = _layer_norm(x_ref[sl, :], lng_ref[...], lnb_ref[...])
        x1 = _layer_norm(DEEPNORM_ALPHA * x0 + g1_ref[0] * mo_s[sl, :], l1g_ref[...], l1b_ref[...])
        x1_ref[sl, :] = x1
        tok = x1 * (1.0 + sc2_ref[0]) + sh2_ref[0]
        t_hi = tok.astype(BF16)
        t_lo = (tok - t_hi.astype(F32)).astype(BF16)
        lg_ref[sl, :] = (jnp.dot(t_hi, wr_hi, preferred_element_type=F32)
                         + jnp.dot(t_lo, wr_hi, preferred_element_type=F32)
                         + jnp.dot(t_hi, wr_lo, preferred_element_type=F32) + br_ref[...])
        return carry
    lax.fori_loop(0, tm // 128, rows, 0)


def _merge(mix, x2d, ln_g, ln_b, g1, ln1_g, ln1_b, sc2, sh2, w_o, w_r, b_r, seq):
    t, dm = x2d.shape
    tm = 512
    per_b = seq // tm
    row = lambda i: (i, 0)
    mod = lambda i: (i // per_b, 0, 0)
    return pl.pallas_call(
        functools.partial(_merge_body, tm=tm),
        grid=(t // tm,),
        scratch_shapes=[pltpu.VMEM((tm, dm), F32)],
        in_specs=[pl.BlockSpec((tm, dm), row), pl.BlockSpec((tm, dm), row),
                  _const_spec((1, dm)), _const_spec((1, dm)),
                  pl.BlockSpec((1, 1, dm), mod),
                  _const_spec((1, dm)), _const_spec((1, dm)),
                  pl.BlockSpec((1, 1, dm), mod), pl.BlockSpec((1, 1, dm), mod),
                  _const_spec((dm, dm)), _const_spec((dm, LANES)), _const_spec((1, LANES))],
        out_specs=[pl.BlockSpec((tm, dm), row), pl.BlockSpec((tm, LANES), row)],
        out_shape=[jax.ShapeDtypeStruct((t, dm), F32), jax.ShapeDtypeStruct((t, LANES), F32)],
        compiler_params=_cparams(("parallel",), 56),
        name="merge",
    )(mix, x2d, ln_g, ln_b, g1, ln1_g, ln1_b, sc2, sh2, w_o, w_r, b_r)


def _route_body(lg_ref, w_ref, d_ref, cnt_ref, run_s, tot_s, *, tr):
    phase = pl.program_id(0)

    @pl.when(pl.program_id(1) == 0)
    def _():
        @pl.when(phase == 1)
        def _():
            tot_s[...] = run_s[...]
        run_s[...] = jnp.zeros_like(run_s)

    lane = lax.broadcasted_iota(I32, (tr, LANES), 1)
    lane_f = lane.astype(F32)
    logit = lg_ref[...]
    hot, val = [], []
    for _ in range(TOP_K):
        mk = jnp.max(logit, axis=-1, keepdims=True)
        ik = jnp.min(jnp.where(logit == mk, lane_f, float(LANES)), axis=-1, keepdims=True)
        hk = lane_f == ik
        logit = jnp.where(hk, -jnp.inf, logit)
        hot.append(hk)
        val.append(mk)
    cnt = jnp.zeros((tr, LANES), F32)
    for hk in hot:
        cnt = cnt + jnp.where(hk, 1.0, 0.0)
    run = run_s[...] + jnp.sum(cnt, axis=0, keepdims=True)

    @pl.when(phase == 0)
    def _():
        cnt_ref[...] = run.astype(I32)

    @pl.when(phase == 1)
    def _():
        total = tot_s[...]
        padded = jnp.floor((total + (MOE_BLOCK - 1.0)) * (1.0 / MOE_BLOCK)) * MOE_BLOCK
        lane8 = lax.broadcasted_iota(I32, (8, LANES), 1)
        incl = padded
        sft = 1
        while sft < LANES:
            incl = incl + jnp.where(lane8 >= sft, pltpu.roll(incl, sft, 1), 0.0)
            sft *= 2
        pstart = (incl - padded)[0:1, :]
        lower = (lax.broadcasted_iota(I32, (tr, tr), 0) > lax.broadcasted_iota(I32, (tr, tr), 1))
        before = jnp.dot(jnp.where(lower, 1.0, 0.0).astype(BF16), cnt.astype(BF16),
                         preferred_element_type=F32) + (run_s[0:1, :] + pstart)
        ex = [jnp.exp(v - val[0]) for v in val]
        denom = ex[0] + ex[1] + ex[2] + ex[3]
        w_out = jnp.zeros((tr, LANES), F32)
        d_out = jnp.zeros((tr, LANES), I32)
        for k in range(TOP_K):
            dest = jnp.sum(jnp.where(hot[k], before, 0.0), axis=-1, keepdims=True)
            w_out = jnp.where(lane == k, ex[k] / denom, w_out)
            d_out = jnp.where(lane == k, dest.astype(I32), d_out)
        w_ref[...] = w_out
        d_ref[...] = d_out[:, :TOP_K]
        cnt_ref[...] = total.astype(I32)

    run_s[...] = run


def _route(logits):
    t = logits.shape[0]
    tr = 512
    row = lambda p, i: (i, 0)
    out_row = lambda p, i: (i * p, 0)
    return pl.pallas_call(
        functools.partial(_route_body, tr=tr),
        grid=(2, t // tr),
        in_specs=[pl.BlockSpec((tr, LANES), row)],
        out_specs=[pl.BlockSpec((tr, LANES), out_row), pl.BlockSpec((tr, TOP_K), out_row),
                   pl.BlockSpec((8, LANES), lambda p, i: (0, 0))],
        out_shape=[jax.ShapeDtypeStruct((t, LANES), F32), jax.ShapeDtypeStruct((t, TOP_K), I32),
                   jax.ShapeDtypeStruct((8, LANES), I32)],
        scratch_shapes=[pltpu.VMEM((8, LANES), F32), pltpu.VMEM((8, LANES), F32)],
        compiler_params=_cparams(("arbitrary", "arbitrary"), 32),
        name="route",
    )(logits)


def _scatter_body(cnt_ref, pstart_ref, used_ref, dest_ref, x1_ref, sc_ref, sh_ref, xb_ref, tok_s, zero_s, sem, pad_sem,
                  *, ts, n_blocks):
    def row_copy(src, r_src, r_dst, s):
        return pltpu.make_async_copy(src.at[_slab_rows(r_src)], xb_ref.at[_slab_rows(r_dst)], s)

    def block_copy(blk):
        rows = MOE_BLOCK * ROW_SLABS
        return pltpu.make_async_copy(zero_s, xb_ref.at[pl.ds(pl.multiple_of(blk * rows, rows), rows)], pad_sem)

    @pl.when(pl.program_id(0) == 0)
    def _():
        zero_s[...] = jnp.zeros_like(zero_s)

        def per_expert(e, carry):
            cnt = cnt_ref[e]
            lo = pstart_ref[e] + cnt
            hi = pstart_ref[e] + (cnt + MOE_BLOCK - 1) // MOE_BLOCK * MOE_BLOCK

            def start(r, c):
                row_copy(zero_s, 0, r, pad_sem).start()
                return c

            def wait(r, c):
                row_copy(zero_s, 0, r, pad_sem).wait()
                return c
            lax.fori_loop(lo, hi, start, 0)
            lax.fori_loop(lo, hi, wait, 0)
            return carry
        lax.fori_loop(0, N_EXPERTS, per_expert, 0)

        def tail_start(blk, c):
            block_copy(blk).start()
            return c

        def tail_wait(blk, c):
            block_copy(blk).wait()
            return c
        lax.fori_loop(used_ref[0], n_blocks, tail_start, 0)
        lax.fori_loop(used_ref[0], n_blocks, tail_wait, 0)

    tok = x1_ref[...] * (1.0 + sc_ref[0]) + sh_ref[0]
    for j in range(ROW_SLABS):
        tok_s[pl.ds(j, ts, stride=ROW_SLABS), :] = tok[:, j * LANES:(j + 1) * LANES]

    def row(r, carry):
        for k in range(TOP_K):
            row_copy(tok_s, r, dest_ref[r * TOP_K + k], sem).start()
        return carry
    lax.fori_loop(0, ts, row, 0)
    for _ in range(TOP_K):
        pltpu.make_async_copy(tok_s, xb_ref.at[pl.ds(0, ts * ROW_SLABS)], sem).wait()


def _scatter(x1, sc2, sh2, dest_flat, counts, pstart, used, n_blocks, seq):
    t, dm = x1.shape
    ts = 256
    per_b = seq // ts
    grid_spec = pltpu.PrefetchScalarGridSpec(
        num_scalar_prefetch=3,
        grid=(t // ts,),
        in_specs=[pl.BlockSpec((ts * TOP_K,), lambda i, *_: (i,), memory_space=pltpu.SMEM),
                  pl.BlockSpec((ts, dm), lambda i, *_: (i, 0)),
                  pl.BlockSpec((1, 1, dm), lambda i, *_: (i // per_b, 0, 0)),
                  pl.BlockSpec((1, 1, dm), lambda i, *_: (i // per_b, 0, 0))],
        out_specs=pl.BlockSpec(memory_space=pl.ANY),
        scratch_shapes=[pltpu.VMEM((ts * ROW_SLABS, LANES), F32), pltpu.VMEM((MOE_BLOCK * ROW_SLABS, LANES), F32),
                        pltpu.SemaphoreType.DMA(()), pltpu.SemaphoreType.DMA(())],
    )
    return pl.pallas_call(
        functools.partial(_scatter_body, ts=ts, n_blocks=n_blocks),
        grid_spec=grid_spec,
        out_shape=jax.ShapeDtypeStruct((n_blocks * MOE_BLOCK * ROW_SLABS, LANES), F32),
        compiler_params=_cparams(("arbitrary",), 32),
        name="scatter",
    )(counts, pstart, used, dest_flat, x1, sc2, sh2)


def _ffn1_body(e_ref, c_ref, blk_ref, oblk_ref, oc_ref, first_ref, n_ref,
               x_ref, wg_ref, wu_ref, bg_ref, bu_ref, a_ref, wg_s, wu_s, x_s):
    s = pl.program_id(0)

    @pl.when(first_ref[s] == 1)
    def _():
        wg_s[...] = wg_ref[0].astype(BF16)
        wu_s[...] = wu_ref[0].astype(BF16)

    @pl.when(s < n_ref[0])
    def _():
        for j in range(ROW_SLABS):
            x_s[:, j * LANES:(j + 1) * LANES] = x_ref[pl.ds(j, MOE_BLOCK, stride=ROW_SLABS), :].astype(BF16)
        x = x_s[...]
        g = jnp.minimum(jnp.dot(x, wg_s[...], preferred_element_type=F32) + bg_ref[0], SWIGLU_LIMIT)
        u = jnp.clip(jnp.dot(x, wu_s[...], preferred_element_type=F32) + bu_ref[0], -SWIGLU_LIMIT, SWIGLU_LIMIT)
        a_ref[...] = (g * _sigmoid(SWIGLU_ALPHA * g) * (u + 1.0)).astype(BF16)

    @pl.when(s >= n_ref[0])
    def _():
        a_ref[...] = jnp.zeros_like(a_ref)


def _ffn2_body(e_ref, c_ref, blk_ref, oblk_ref, oc_ref, first_ref, n_ref, a_ref, wd_ref, bd_ref, y_ref, wd_s):
    s = pl.program_id(0)

    @pl.when(first_ref[s] == 1)
    def _():
        wd_s[...] = wd_ref[0].astype(BF16)

    @pl.when(s < n_ref[0])
    def _():
        y = jnp.dot(a_ref[...], wd_s[...], preferred_element_type=F32) + bd_ref[0]
        for j in range(ROW_SLABS):
            y_ref[pl.ds(j, MOE_BLOCK, stride=ROW_SLABS), :] = y[:, j * LANES:(j + 1) * LANES]

    @pl.when(s >= n_ref[0])
    def _():
        y_ref[...] = jnp.zeros_like(y_ref)


def _ffn1(plan, xb, w_gate, w_up, b_gate, b_up, n_steps):
    rows = xb.shape[0] // ROW_SLABS
    dm, de = w_gate.shape[1:]
    wspec = pl.BlockSpec((1, dm, FF_CHUNK), lambda s, e, c, *_: (e[s], 0, c[s]))
    bspec = pl.BlockSpec((1, 1, FF_CHUNK), lambda s, e, c, *_: (e[s], 0, c[s]))
    grid_spec = pltpu.PrefetchScalarGridSpec(
        num_scalar_prefetch=7,
        grid=(n_steps,),
        in_specs=[pl.BlockSpec((MOE_BLOCK * ROW_SLABS, LANES), lambda s, e, c, blk, *_: (blk[s], 0)),
                  wspec, wspec, bspec, bspec],
        out_specs=pl.BlockSpec((MOE_BLOCK, FF_CHUNK), lambda s, e, c, blk, oblk, oc, *_: (oblk[s], oc[s])),
        scratch_shapes=[pltpu.VMEM((dm, FF_CHUNK), BF16), pltpu.VMEM((dm, FF_CHUNK), BF16),
                        pltpu.VMEM((MOE_BLOCK, dm), BF16)],
    )
    return pl.pallas_call(
        _ffn1_body, grid_spec=grid_spec,
        out_shape=jax.ShapeDtypeStruct((rows, de), BF16),
        compiler_params=_cparams(("arbitrary",), 60),
        name="ffn1",
    )(*plan, xb, w_gate, w_up, b_gate, b_up)


def _ffn2(plan, act, w_down, b_down, n_steps, chunk):
    rows, de = act.shape
    dm = w_down.shape[2]
    grid_spec = pltpu.PrefetchScalarGridSpec(
        num_scalar_prefetch=7,
        grid=(n_steps,),
        in_specs=[pl.BlockSpec((MOE_BLOCK, de), lambda s, e, c, blk, *_: (blk[s], 0)),
                  pl.BlockSpec((1, de, chunk), lambda s, e, c, *_: (e[s], 0, c[s])),
                  pl.BlockSpec((1, 1, chunk), lambda s, e, c, *_: (e[s], 0, c[s]))],
        out_specs=pl.BlockSpec((MOE_BLOCK * ROW_SLABS, LANES), lambda s, e, c, blk, oblk, oc, *_: (oblk[s], 0)),
        scratch_shapes=[pltpu.VMEM((de, chunk), BF16)],
    )
    assert chunk == dm == ROW_SLABS * LANES
    return pl.pallas_call(
        _ffn2_body, grid_spec=grid_spec,
        out_shape=jax.ShapeDtypeStruct((rows * ROW_SLABS, LANES), F32),
        compiler_params=_cparams(("arbitrary",), 60),
        name="ffn2",
    )(*plan, act, w_down, b_down)


def _combine_body(dcur_ref, dnxt_ref, w_ref, x1_ref, g2_ref, lg_ref, lb_ref, yb_ref, o_ref, buf, sem, *, tc, n_tiles):
    i = pl.program_id(0)

    def issue(d_ref, slot):
        def row(r, carry):
            for k in range(TOP_K):
                pltpu.make_async_copy(yb_ref.at[_slab_rows(d_ref[r * TOP_K + k])], buf.at[slot, k, _slab_rows(r)],
                                      sem.at[slot]).start()
            return carry
        lax.fori_loop(0, tc, row, 0)

    @pl.when(i == 0)
    def _():
        issue(dcur_ref, 0)

    @pl.when(i + 1 < n_tiles)
    def _():
        issue(dnxt_ref, (i + 1) % 2)

    slot = i % 2
    for k in range(TOP_K):
        pltpu.make_async_copy(yb_ref.at[pl.ds(0, tc * ROW_SLABS)], buf.at[slot, k], sem.at[slot]).wait()
    w = w_ref[...]
    parts = []
    for j in range(ROW_SLABS):
        yj = w[:, 0:1] * buf[slot, 0, pl.ds(j, tc, stride=ROW_SLABS), :]
        for k in range(1, TOP_K):
            yj = yj + w[:, k:k + 1] * buf[slot, k, pl.ds(j, tc, stride=ROW_SLABS), :]
        parts.append(yj)
    y = jnp.concatenate(parts, axis=-1)
    o_ref[...] = _layer_norm(DEEPNORM_ALPHA * x1_ref[...] + g2_ref[0] * y, lg_ref[...], lb_ref[...])


def _combine(dest_flat, w4, x1, g2, ln_g, ln_b, yb, seq):
    t, dm = x1.shape
    tc = 256
    n_tiles = t // tc
    per_b = seq // tc
    row = lambda i: (i, 0)
    return pl.pallas_call(
        functools.partial(_combine_body, tc=tc, n_tiles=n_tiles),
        grid=(n_tiles,),
        in_specs=[pl.BlockSpec((tc * TOP_K,), lambda i: (i,), memory_space=pltpu.SMEM),
                  pl.BlockSpec((tc * TOP_K,), lambda i: (jnp.minimum(i + 1, n_tiles - 1),), memory_space=pltpu.SMEM),
                  pl.BlockSpec((tc, LANES), row),
                  pl.BlockSpec((tc, dm), row),
                  pl.BlockSpec((1, 1, dm), lambda i: (i // per_b, 0, 0)),
                  _const_spec((1, dm)), _const_spec((1, dm)),
                  pl.BlockSpec(memory_space=pl.ANY)],
        out_specs=pl.BlockSpec((tc, dm), row),
        out_shape=jax.ShapeDtypeStruct((t, dm), F32),
        scratch_shapes=[pltpu.VMEM((2, TOP_K, tc * ROW_SLABS, LANES), F32), pltpu.SemaphoreType.DMA((2,))],
        compiler_params=_cparams(("arbitrary",), 40),
        name="combine",
    )(dest_flat, dest_flat, w4, x1, g2, ln_g, ln_b, yb)


def _dft_tables(seq):
    n = 2 * seq
    k = np.arange(seq, dtype=np.int64)
    ang = (2.0 * np.pi / n) * ((k[:, None] * k[None, :]) % n).astype(np.float64)
    return (jnp.asarray(np.cos(ang), F32).astype(BF16), jnp.asarray(np.sin(ang), F32).astype(BF16))


def _filter_features(seq):
    t = jnp.linspace(0.0, 1.0, seq, dtype=F32)[:, None]
    bands = (HYENA_EMB - 1) // 2
    f = jnp.linspace(1e-4, bands - 1, bands, dtype=F32)[None, :]
    ang = 2.0 * math.pi * jnp.arange(seq, dtype=F32)[:, None] * f / seq
    feats = jnp.concatenate([t, jnp.cos(ang), -jnp.sin(ang)], axis=-1)
    max_decay = math.log(HYENA_DECAY_TARGET) / HYENA_SHORT_DECAY_PCT
    min_decay = math.log(HYENA_DECAY_TARGET) / HYENA_LONG_DECAY_PCT
    deltas = jnp.abs(jnp.linspace(min_decay, max_decay, HYENA_WIDTH, dtype=F32))[None, :]
    return t, feats, deltas


def _moe_plan(counts, n_blocks, n_chunks):
    ids = jnp.arange(N_EXPERTS, dtype=I32)
    nblk = (counts + MOE_BLOCK - 1) // MOE_BLOCK
    blk_end = jnp.cumsum(nblk)
    blk_start = blk_end - nblk
    used = blk_end[-1]
    steps = n_chunks * nblk
    step_end = jnp.cumsum(steps)
    n_used = step_end[-1]
    s_all = jnp.arange(n_chunks * n_blocks, dtype=I32)
    s = jnp.minimum(s_all, n_used - 1)
    e_s = jnp.minimum(jnp.sum((s[:, None] >= step_end[None, :]).astype(I32), axis=1), N_EXPERTS - 1)
    onehot = e_s[:, None] == ids[None, :]
    pick = lambda table: jnp.sum(jnp.where(onehot, table[None, :], 0), axis=1)
    loc = s - pick(step_end - steps)
    nb = jnp.maximum(pick(nblk), 1)
    c_s = loc // nb
    r_s = loc % nb
    blk = pick(blk_start) + r_s
    tail = s_all >= n_used
    j = jnp.maximum(s_all - n_used, 0)
    n_tail = jnp.maximum(n_blocks - used, 1)
    oblk = jnp.where(tail, used + j % n_tail, blk)
    oc = jnp.where(tail, j // n_tail, c_s)
    first = jnp.logical_and(r_s == 0, jnp.logical_not(tail))
    as_i32 = lambda v: v.astype(I32)
    plan = tuple(map(as_i32, (e_s, c_s, blk, oblk, oc, first, n_used.reshape(1))))
    return as_i32(blk_start * MOE_BLOCK), as_i32(used.reshape(1)), plan


def kernel(x, c, ctx, c_ctx, ln_in_g, ln_in_b, w_mod, b_mod, w_in, b_in, mlstm_conv_w, mlstm_conv_b,
           w_qh, w_kh, hyena_conv_w, hyena_conv_b, filt_w1, filt_b1, filt_wh, filt_bh, filt_freq, filt_wout,
           hyena_skip, w_proj_a, w_proj_h, w_out, ln1_g, ln1_b, w_router, b_router, w_gate, b_gate,
           w_up, b_up, w_down, b_down, ln2_g, ln2_b):
    bsz, seq, dm = x.shape
    ctx_len = ctx.shape[1]
    t = bsz * seq
    assert w_mod.shape[0] == DEPTH and dm == D_MODEL and ctx_len == CHUNK and bsz + 1 <= 16
    row = lambda v: v.reshape(1, -1)

    cond = jnp.concatenate([c, c_ctx[None], jnp.zeros((16 - bsz - 1, dm), F32)], axis=0)
    mod = _mod(cond, w_mod[0], row(b_mod[0]))
    sh1, sc1, g1, sh2, sc2, g2 = [m[:, None, :] for m in jnp.split(mod, 6, axis=-1)]

    w_main = jnp.concatenate([w_in[0][:, :IN_GATES], w_in[0][:, IN_O:]], axis=1).astype(BF16)
    b_main = row(jnp.concatenate([b_in[0][:IN_GATES], b_in[0][IN_O:]]))
    w_g = jnp.pad(w_in[0][:, IN_GATES:IN_O], ((0, 0), (0, LANES - 4 * HEADS))).astype(BF16)
    b_g = row(jnp.pad(b_in[0][IN_GATES:IN_O], (0, LANES - 4 * HEADS)))
    lng, lnb = row(ln_in_g), row(ln_in_b)
    x2d = x.reshape(t, dm)
    z, gates, sg = _in_proj(x2d, lng, lnb, sc1[:bsz], sh1[:bsz], w_main, b_main, w_g, b_g, seq, Z_BG)
    zc, gates_c = _in_proj(ctx.reshape(bsz * ctx_len, dm), lng, lnb, sc1[bsz:bsz + 1], sh1[bsz:bsz + 1],
                           w_main[:, :IN_GATES], b_main[:, :IN_GATES], w_g, b_g, bsz * ctx_len, IN_GATES)

    g_all = jnp.concatenate([gates_c[:, :4 * HEADS].reshape(bsz, ctx_len, 4, HEADS),
                             gates[:, :4 * HEADS].reshape(bsz, seq, 4, HEADS)], axis=1)
    gates_t = g_all.transpose(0, 3, 2, 1)

    oh = _mlstm(z, zc, gates_t, mlstm_conv_w[0].reshape(9, MLSTM_WIDTH), row(mlstm_conv_b[0]),
                w_qh[0], w_kh[0].transpose(0, 2, 1), bsz, seq, ctx_len)

    tcol, feats, deltas = _filter_features(seq)
    feats = jnp.pad(feats, ((0, 0), (0, LANES - HYENA_EMB)))
    w1 = jnp.pad(filt_w1[0], ((0, LANES - HYENA_EMB), (0, 0)))
    a = _filt(feats, w1, row(filt_b1[0]), filt_wh[0], filt_bh[0], filt_freq[0])
    fh = a.shape[1]
    w_fout4 = filt_wout[0].reshape(fh, 4, HYENA_WIDTH).transpose(1, 0, 2)
    cos_m, sin_m = _dft_tables(seq)
    hh = _hyena(z, hyena_conv_w[0].reshape(9, 3 * HYENA_WIDTH), row(hyena_conv_b[0]), a, w_fout4, deltas, tcol,
                hyena_skip[0], cos_m, sin_m, bsz, seq)

    w_r = jnp.pad(w_router[0], ((0, 0), (0, LANES - N_EXPERTS)))
    b_r = row(jnp.pad(b_router[0], (0, LANES - N_EXPERTS), constant_values=-1e30))
    mix = _mix(oh, hh, sg, w_proj_a[0].astype(BF16), w_proj_h[0].astype(BF16))
    x1, logits = _merge(mix, x2d, lng, lnb, g1[:bsz], row(ln1_g[0]), row(ln1_b[0]), sc2[:bsz], sh2[:bsz],
                        w_out[0].astype(BF16), w_r, b_r, seq)

    w4, dest, counts = _route(logits)
    n_blocks = -(-(t * TOP_K + N_EXPERTS * (MOE_BLOCK - 1)) // MOE_BLOCK)
    n_chunks = dm // FF_CHUNK
    counts = counts[0, :N_EXPERTS]
    pstart, used, plan = _moe_plan(counts, n_blocks, n_chunks)
    dest_flat = dest.reshape(t * TOP_K)
    xb = _scatter(x1, sc2[:bsz], sh2[:bsz], dest_flat, counts, pstart, used, n_blocks, seq)
    act = _ffn1(plan, xb, w_gate[0], w_up[0], b_gate[0][:, None, :], b_up[0][:, None, :], n_chunks * n_blocks)
    _, _, plan_down = _moe_plan(counts, n_blocks, 1)
    yb = _ffn2(plan_down, act, w_down[0], b_down[0][:, None, :], n_blocks, dm)
    out = _combine(dest_flat, w4, x1, g2[:bsz], row(ln2_g[0]), row(ln2_b[0]), yb, seq)
    return out.reshape(bsz, seq, dm)
```

```python
import functools
import math

import numpy as np
import jax
import jax.numpy as jnp
from jax import lax
from jax.experimental import pallas as pl
from jax.experimental.pallas import tpu as pltpu

F32 = jnp.float32
BF16 = jnp.bfloat16
I32 = jnp.int32
HIGHEST = lax.Precision.HIGHEST

D_MODEL = 2048
GRID_W = 64
HEADS = 4
HEAD_DIM = 256
MLSTM_WIDTH = HEADS * HEAD_DIM
HYENA_WIDTH = D_MODEL // 2
HYENA_EMB = 33
HYENA_DECAY_TARGET = 1e-2
HYENA_SHORT_DECAY_PCT = 0.3
HYENA_LONG_DECAY_PCT = 1.5
N_EXPERTS = 32
TOP_K = 4
SWIGLU_LIMIT = 7.0
SWIGLU_ALPHA = 1.702
LN_EPS = 1e-5
DEPTH = 1
DEEPNORM_ALPHA = (2.0 * DEPTH) ** 0.25

IN_V = MLSTM_WIDTH
IN_GATES = 2 * MLSTM_WIDTH
IN_O = IN_GATES + 4 * HEADS
Z_QK, Z_V, Z_O, Z_HY, Z_BG = 0, 1024, 2048, 3072, 6144

LANES = 128
MXU = 256
CHUNK = 256
MOE_BLOCK = 512
FF_CHUNK = 1024
MIB = 1024 * 1024


def _cparams(semantics, vmem_mib):
    return pltpu.CompilerParams(dimension_semantics=semantics, vmem_limit_bytes=vmem_mib * MIB)


def _const_spec(shape):
    nd = len(shape)
    return pl.BlockSpec(shape, lambda *_: (0,) * nd, pipeline_mode=pl.Buffered(1))


def _layer_norm(x, g, b):
    mu = jnp.mean(x, axis=-1, keepdims=True)
    xc = x - mu
    var = jnp.mean(xc * xc, axis=-1, keepdims=True)
    return xc * lax.rsqrt(var + LN_EPS) * g + b


def _sigmoid(x):
    return 1.0 / (1.0 + jnp.exp(-x))


def _silu(x):
    return x * _sigmoid(x)


def _log_sigmoid(x):
    return jnp.minimum(x, 0.0) - jnp.log(1.0 + jnp.exp(-jnp.abs(x)))


def _mod_body(c_ref, w_ref, b_ref, o_ref):
    s = _silu(c_ref[...])
    o_ref[...] = jnp.dot(s.astype(BF16), w_ref[...].astype(BF16), preferred_element_type=F32) + b_ref[...]


def _mod(cond, w, b):
    rows, dm = cond.shape
    n = w.shape[1]
    tn = 1024
    return pl.pallas_call(
        _mod_body,
        grid=(n // tn,),
        in_specs=[pl.BlockSpec((rows, dm), lambda j: (0, 0)),
                  pl.BlockSpec((dm, tn), lambda j: (0, j)),
                  pl.BlockSpec((1, tn), lambda j: (0, j))],
        out_specs=pl.BlockSpec((rows, tn), lambda j: (0, j)),
        out_shape=jax.ShapeDtypeStruct((rows, n), F32),
        compiler_params=_cparams(("arbitrary",), 40),
        name="mod",
    )(cond, w, b)


def _in_proj_body(x_ref, lng_ref, lnb_ref, sc_ref, sh_ref, w_ref, b_ref, wg_ref, bg_ref,
                  z_ref, g_ref, *rest, tm, n_plain):
    hx_s = rest[-1]

    @pl.when(pl.program_id(1) == 0)
    def _():
        def rows(r, carry):
            sl = pl.ds(pl.multiple_of(r * 128, 128), 128)
            xn = _layer_norm(x_ref[sl, :], lng_ref[...], lnb_ref[...])
            hx_s[sl, :] = (xn * (1.0 + sc_ref[0]) + sh_ref[0]).astype(BF16)
            return carry
        lax.fori_loop(0, tm // 128, rows, 0)
        g_ref[...] = jnp.dot(hx_s[...], wg_ref[...], preferred_element_type=F32) + bg_ref[...]

    acc = jnp.dot(hx_s[...], w_ref[...], preferred_element_type=F32) + b_ref[...]
    if len(rest) == 1:
        z_ref[...] = acc
    else:
        @pl.when(pl.program_id(1) < n_plain)
        def _():
            z_ref[...] = acc

        @pl.when(pl.program_id(1) >= n_plain)
        def _():
            rest[0][...] = _sigmoid(acc).astype(BF16)


def _in_proj(x2d, ln_g, ln_b, scale, shift, w, b, w_gates, b_gates, rows_per_mod, plain_cols):
    t, dm = x2d.shape
    n = w.shape[1]
    tm, tn = min(1024, t), 1024
    n_plain = plain_cols // tn
    out_specs = [pl.BlockSpec((tm, tn), lambda i, j: (i, jnp.minimum(j, n_plain - 1))),
                 pl.BlockSpec((tm, LANES), lambda i, j: (i, 0))]
    out_shape = [jax.ShapeDtypeStruct((t, plain_cols), F32), jax.ShapeDtypeStruct((t, LANES), F32)]
    if n > plain_cols:
        out_specs.append(pl.BlockSpec((tm, tn), lambda i, j: (i, jnp.maximum(j - n_plain, 0))))
        out_shape.append(jax.ShapeDtypeStruct((t, n - plain_cols), BF16))
    return pl.pallas_call(
        functools.partial(_in_proj_body, tm=tm, n_plain=n_plain),
        grid=(t // tm, n // tn),
        in_specs=[pl.BlockSpec((tm, dm), lambda i, j: (i, 0)),
                  pl.BlockSpec((1, dm), lambda i, j: (0, 0)),
                  pl.BlockSpec((1, dm), lambda i, j: (0, 0)),
                  pl.BlockSpec((1, 1, dm), lambda i, j: (i * tm // rows_per_mod, 0, 0)),
                  pl.BlockSpec((1, 1, dm), lambda i, j: (i * tm // rows_per_mod, 0, 0)),
                  pl.BlockSpec((dm, tn), lambda i, j: (0, j)),
                  pl.BlockSpec((1, tn), lambda i, j: (0, j)),
                  pl.BlockSpec((dm, LANES), lambda i, j: (0, 0)),
                  pl.BlockSpec((1, LANES), lambda i, j: (0, 0))],
        out_specs=out_specs,
        out_shape=out_shape,
        scratch_shapes=[pltpu.VMEM((tm, dm), BF16)],
        compiler_params=_cparams(("parallel", "arbitrary"), 48),
        name="in_proj",
    )(x2d, ln_g, ln_b, scale, shift, w, b, w_gates, b_gates)


def _dwconv(u, w9, bias, width, single_row):
    length, ch = u.shape
    col = lax.broadcasted_iota(I32, (length, ch), 0) % width
    if not single_row:
        zpad = jnp.zeros((width, ch), F32)
        up = jnp.concatenate([zpad, u[:length - width]], axis=0)
        dn = jnp.concatenate([u[width:], zpad], axis=0)
    out = None
    for dc in (-1, 0, 1):
        a = u * w9[4 + dc:5 + dc]
        if not single_row:
            a = a + up * w9[1 + dc:2 + dc] + dn * w9[7 + dc:8 + dc]
        if dc == -1:
            a = jnp.where(col == 0, 0.0, pltpu.roll(a, 1, 0))
        elif dc == 1:
            a = jnp.where(col == width - 1, 0.0, pltpu.roll(a, length - 1, 0))
        out = a if out is None else out + a
    return out + bias


def _mlstm_body(zqk_ref, zv_ref, zo_ref, cqk_ref, cv_ref, gt_ref, cw_ref, cb_ref, wq_ref, wkt_ref,
                o_ref, q_s, k_s, kt_s, v_s, hf_s, hb_s, r_s, c_s, ct_s, *, ctx_len, seq):
    n_chunks = (ctx_len + seq) // CHUNK
    total = ctx_len + seq
    cw = cw_ref[...]
    cb = cb_ref[...]
    wq = wq_ref[0].astype(BF16)
    wkt = wkt_ref[0].astype(BF16)
    nt = (((1,), (1,)), ((), ()))
    scale = HEAD_DIM ** -0.5

    def project(u, off, n):
        ub = u.astype(BF16)
        q_s[off:off + n, :] = jnp.dot(ub, wq, preferred_element_type=F32).astype(BF16)
        k_s[off:off + n, :] = (lax.dot_general(ub, wkt, nt, preferred_element_type=F32) * scale).astype(BF16)
        kt_s[:, off:off + n] = (lax.dot_general(wkt, ub, nt, preferred_element_type=F32) * scale).astype(BF16)

    project(_silu(_dwconv(cqk_ref[...], cw, cb, ctx_len, True)), 0, ctx_len)
    project(_silu(_dwconv(zqk_ref[...], cw, cb, GRID_W, False)), ctx_len, seq)
    v_s[0:ctx_len, :] = cv_ref[...]
    v_s[ctx_len:total, :] = zv_ref[...]

    gt = gt_ref[0, 0]
    lf = _log_sigmoid(gt)
    pos = lax.broadcasted_iota(I32, (4, total), 1) % CHUNK
    pre, suf = lf, lf
    s = 1
    while s < CHUNK:
        pre = pre + jnp.where(pos >= s, pltpu.roll(pre, s, 1), 0.0)
        suf = suf + jnp.where(pos < CHUNK - s, pltpu.roll(suf, total - s, 1), 0.0)
        s *= 2
    b_f, li_f, b_b, li_b = pre[1:2], gt[0:1], suf[3:4], gt[2:3]
    pm_f, pm_b = li_f - b_f, li_b - b_b
    pos1 = pos[0:1]
    s = 1
    while s < CHUNK:
        pm_f = jnp.maximum(pm_f, jnp.where(pos1 >= s, pltpu.roll(pm_f, s, 1), -jnp.inf))
        pm_b = jnp.maximum(pm_b, jnp.where(pos1 < CHUNK - s, pltpu.roll(pm_b, total - s, 1), -jnp.inf))
        s *= 2
    r_s[...] = jnp.concatenate([b_f, li_f, pm_f, b_b, li_b, pm_b, jnp.zeros((2, total), F32)], axis=0)
    zfill = jnp.zeros((CHUNK - 8, CHUNK), F32)
    for c in range(n_chunks):
        blk = jnp.concatenate([r_s[:, c * CHUNK:(c + 1) * CHUNK], zfill], axis=0).T
        c_s[c * CHUNK:(c + 1) * CHUNK, :] = blk[:, :LANES]

    ct_s[...] = jnp.zeros_like(ct_s)
    row_i = lax.broadcasted_iota(I32, (CHUNK, CHUNK), 0)
    col_i = lax.broadcasted_iota(I32, (CHUNK, CHUNK), 1)
    masks = (row_i >= col_i, row_i <= col_i)

    def chunk_step(c, d, n_vec, m):
        off = c * CHUNK if isinstance(c, int) else pl.multiple_of(c * CHUNK, CHUNK)
        rows = r_s[:, pl.ds(off, CHUNK)]
        cols = c_s[pl.ds(off, CHUNK), :]
        b_row, li_row, pm_row = rows[3 * d:3 * d + 1], rows[3 * d + 1:3 * d + 2], rows[3 * d + 2:3 * d + 3]
        b_col, li_col, pm_col = cols[:, 3 * d:3 * d + 1], cols[:, 3 * d + 1:3 * d + 2], cols[:, 3 * d + 2:3 * d + 3]
        last = slice(CHUNK - 1, CHUNK) if d == 0 else slice(0, 1)
        b_end, pm_end = b_row[:, last], pm_row[:, last]
        qc = q_s[pl.ds(off, CHUNK), :]
        kc = k_s[pl.ds(off, CHUNK), :]
        ktc = kt_s[:, pl.ds(off, CHUNK)]
        vc = v_s[pl.ds(off, CHUNK), :]
        ct = ct_s[d]

        inter = b_col + m
        m_t = jnp.maximum(inter, b_col + pm_col)
        wts = jnp.exp(jnp.where(masks[d], b_col - b_row + li_row, -jnp.inf) - m_t)
        s_inter = jnp.exp(inter - m_t)
        scores = jnp.dot(qc, ktc, preferred_element_type=F32) * wts
        num = (s_inter * jnp.dot(qc, ct.astype(BF16), preferred_element_type=F32)
               + jnp.dot(scores.astype(BF16), vc.astype(BF16), preferred_element_type=F32))
        den = (s_inter * jnp.sum(qc.astype(F32) * n_vec, axis=-1, keepdims=True)
               + jnp.sum(scores, axis=-1, keepdims=True))
        h = num / jnp.maximum(jnp.abs(den), jnp.exp(-m_t))

        m_new = jnp.maximum(b_end + m, b_end + pm_end)
        decay = jnp.exp(b_end + m - m_new)
        w = jnp.exp(b_end - b_col + li_col - m_new)
        ct_s[d] = decay * ct + jnp.dot(ktc, (vc * w).astype(BF16), preferred_element_type=F32)
        n_new = decay * n_vec + jnp.sum(kc.astype(F32) * w, axis=0, keepdims=True)
        return h, n_new, m_new

    n0 = jnp.zeros((1, HEAD_DIM), F32)
    m0 = jnp.zeros((1, 1), F32)
    _, nf, mf = chunk_step(0, 0, n0, m0)
    _, nb, mb = chunk_step(0, 1, n0, m0)

    def body(i, carry):
        nf, mf, nb, mb = carry
        hf, nf, mf = chunk_step(i, 0, nf, mf)
        hf_s[pl.ds(pl.multiple_of(i * CHUNK - ctx_len, CHUNK), CHUNK), :] = hf
        j = n_chunks - i
        hb, nb, mb = chunk_step(j, 1, nb, mb)
        hb_s[pl.ds(pl.multiple_of(j * CHUNK - ctx_len, CHUNK), CHUNK), :] = hb
        return nf, mf, nb, mb

    lax.fori_loop(1, n_chunks, body, (nf, mf, nb, mb))
    o_ref[...] = (_sigmoid(zo_ref[...]) * (hf_s[...] + hb_s[...])).astype(BF16)


def _mlstm(z, zc, gates_t, conv_w9, conv_b, w_qh, w_kh_t, bsz, seq, ctx_len):
    total = ctx_len + seq
    hd = HEAD_DIM
    qk_blk, v_blk, o_blk = Z_QK // hd, Z_V // hd, Z_O // hd
    return pl.pallas_call(
        functools.partial(_mlstm_body, ctx_len=ctx_len, seq=seq),
        grid=(bsz, HEADS),
        in_specs=[pl.BlockSpec((seq, hd), lambda b, h: (b, qk_blk + h)),
                  pl.BlockSpec((seq, hd), lambda b, h: (b, v_blk + h)),
                  pl.BlockSpec((seq, hd), lambda b, h: (b, o_blk + h)),
                  pl.BlockSpec((ctx_len, hd), lambda b, h: (b, qk_blk + h)),
                  pl.BlockSpec((ctx_len, hd), lambda b, h: (b, v_blk + h)),
                  pl.BlockSpec((1, 1, 4, total), lambda b, h: (b, h, 0, 0)),
                  pl.BlockSpec((9, hd), lambda b, h: (0, h)),
                  pl.BlockSpec((1, hd), lambda b, h: (0, h)),
                  pl.BlockSpec((1, hd, hd), lambda b, h: (h, 0, 0)),
                  pl.BlockSpec((1, hd, hd), lambda b, h: (h, 0, 0))],
        out_specs=pl.BlockSpec((seq, hd), lambda b, h: (b, h)),
        out_shape=jax.ShapeDtypeStruct((bsz * seq, MLSTM_WIDTH), BF16),
        scratch_shapes=[pltpu.VMEM((total, hd), BF16), pltpu.VMEM((total, hd), BF16), pltpu.VMEM((hd, total), BF16),
                        pltpu.VMEM((total, hd), F32), pltpu.VMEM((seq, hd), F32), pltpu.VMEM((seq, hd), F32),
                        pltpu.VMEM((8, total), F32), pltpu.VMEM((total, LANES), F32), pltpu.VMEM((2, hd, hd), F32)],
        compiler_params=_cparams(("parallel", "arbitrary"), 48),
        name="mlstm",
    )(z, z, z, zc, zc, gates_t, conv_w9, conv_b, w_qh, w_kh_t)


def _filt_body(z_ref, w1_ref, b1_ref, wh_ref, bh_ref, fr_ref, a_ref):
    fr = fr_ref[...]
    a = jnp.sin(fr[0:1] * (jnp.dot(z_ref[...], w1_ref[...], precision=HIGHEST, preferred_element_type=F32)
                           + b1_ref[...]))
    for i in range(2):
        a = jnp.sin(fr[i + 1:i + 2] * (jnp.dot(a, wh_ref[i], precision=HIGHEST, preferred_element_type=F32)
                                       + bh_ref[i:i + 1]))
    a_ref[...] = a


def _filt(feats, w1, b1, wh, bh, freq):
    length = feats.shape[0]
    fh = w1.shape[1]
    return pl.pallas_call(
        _filt_body,
        out_shape=jax.ShapeDtypeStruct((length, fh), F32),
        name="filt",
    )(feats, w1, b1, wh, bh, freq)


def _hyena_body(zx1_ref, zx2_ref, zv_ref, cw1_ref, cw2_ref, cwv_ref, cb1_ref, cb2_ref, cbv_ref,
                a_ref, wf_ref, dl_ref, t_ref, skip_ref, ce_ref, se_ref, co_ref, so_ref, cot_ref, sot_ref, o_ref,
                hce_s, hse_s, hco_s, hso_s, hn_s, v_s, g_s, *, seq):
    n_fft = 2 * seq
    half = seq // 2
    sign = jnp.where(lax.broadcasted_iota(I32, (seq, 1), 0) % 2 == 0, 1.0, -1.0)
    jrow = lax.broadcasted_iota(I32, (half, 1), 0)
    sgn_j = jnp.where(jrow % 2 == 0, 1.0, -1.0)
    first = jrow == 0
    flip = jnp.where(lax.broadcasted_iota(I32, (MXU, MXU), 0) + lax.broadcasted_iota(I32, (MXU, MXU), 1) == MXU - 1,
                     1.0, 0.0).astype(BF16)

    def mm(m_ref, x):
        return jnp.dot(m_ref[...], x, preferred_element_type=F32)

    def reverse_rows(x):
        hi = x.astype(BF16)
        lo = (x - hi.astype(F32)).astype(BF16)
        blocks = []
        for b in range(half // MXU):
            src = slice(half - MXU * (b + 1), half - MXU * b)
            blocks.append(jnp.dot(flip, hi[src], preferred_element_type=F32)
                          + jnp.dot(flip, lo[src], preferred_element_type=F32))
        return jnp.concatenate(blocks, axis=0)

    def fold(v):
        rolled = pltpu.roll(reverse_rows(v[half:]), 1, 0)
        mid = rolled[0:1]
        vr = jnp.where(first, 0.0, rolled)
        return v[:half] + vr, v[:half] - vr, mid

    def forward(v):
        s, d, mid = fold(v)
        sb, db = s.astype(BF16), d.astype(BF16)
        return mm(ce_ref, sb) + sgn_j * mid, mm(se_ref, db), mm(co_ref, db), mm(so_ref, sb) + sgn_j * mid

    @pl.when(pl.program_id(1) == 0)
    def _():
        window = jnp.exp(-t_ref[...] * dl_ref[...])
        row0 = lax.broadcasted_iota(I32, (seq, 1), 0) == 0
        for o in range(2):
            fwd = jnp.dot(a_ref[...], wf_ref[2 * o], precision=HIGHEST, preferred_element_type=F32) * window
            bwd = jnp.dot(a_ref[...], wf_ref[2 * o + 1], precision=HIGHEST, preferred_element_type=F32) * window
            bwd = jnp.where(row0, 0.0, bwd)
            even = fwd + bwd
            hce_s[o], _, hco_s[o], _ = forward(even)
            _, hse_s[o], _, hso_s[o] = forward(fwd - bwd)
            hn_s[o] = jnp.sum(even * sign, axis=0, keepdims=True)

    def conv_to(dst_ref, z_ref, cw_ref, cb_ref):
        for lo in range(0, z_ref.shape[1], LANES):
            ls = slice(lo, lo + LANES)
            dst_ref[:, ls] = _dwconv(z_ref[:, ls], cw_ref[:, ls], cb_ref[:, ls], GRID_W, False)

    conv_to(v_s, zv_ref, cwv_ref, cbv_ref)

    for o, (zg_ref, cwg_ref, cbg_ref) in enumerate(((zx1_ref, cw1_ref, cb1_ref), (zx2_ref, cw2_ref, cb2_ref))):
        v = v_s[...]
        nyq = jnp.sum(v * sign, axis=0, keepdims=True) * hn_s[o] * (1.0 / n_fft)
        xce, xse, xco, xso = forward(v)
        hce, hse, hco, hso = hce_s[o], hse_s[o], hco_s[o], hso_s[o]
        scale_e = jnp.where(first, 1.0 / n_fft, 2.0 / n_fft)
        zce = scale_e * (xce * hce - xse * hse)
        zse = scale_e * (xce * hse + xse * hce)
        zco = (2.0 / n_fft) * (xco * hco - xso * hso)
        zso = (2.0 / n_fft) * (xco * hso + xso * hco)
        y_mid = jnp.sum(sgn_j * (zce + zso), axis=0, keepdims=True)
        sym = mm(ce_ref, zce.astype(BF16)) + mm(sot_ref, zso.astype(BF16))
        asym = mm(cot_ref, zco.astype(BF16)) + mm(se_ref, zse.astype(BF16))
        rolled = pltpu.roll(reverse_rows(sym - asym), 1, 0)
        y = jnp.concatenate([sym + asym, jnp.where(first, y_mid, rolled)], axis=0)
        conv_to(g_s, zg_ref, cwg_ref, cbg_ref)
        v_s[...] = g_s[...] * (y + sign * nyq + v * skip_ref[o:o + 1, :])

    o_ref[...] = v_s[...].astype(BF16)


def _hyena(z, conv_w9, conv_b, a, w_fout4, deltas, tcol, skip, tables, bsz, seq):
    ct = MXU
    n_ct = HYENA_WIDTH // ct
    hy = Z_HY // ct
    half = seq // 2
    zspec = lambda off: pl.BlockSpec((seq, ct), lambda j, b: (b, hy + off * n_ct + j))
    wspec = lambda off: pl.BlockSpec((9, ct), lambda j, b: (0, off * n_ct + j))
    bspec = lambda off: pl.BlockSpec((1, ct), lambda j, b: (0, off * n_ct + j))
    fh = a.shape[1]
    return pl.pallas_call(
        functools.partial(_hyena_body, seq=seq),
        grid=(n_ct, bsz),
        in_specs=[zspec(0), zspec(1), zspec(2), wspec(0), wspec(1), wspec(2), bspec(0), bspec(1), bspec(2),
                  _const_spec((seq, fh)),
                  pl.BlockSpec((4, fh, ct), lambda j, b: (0, 0, j)),
                  pl.BlockSpec((1, ct), lambda j, b: (0, j)),
                  _const_spec((seq, 1)),
                  pl.BlockSpec((2, ct), lambda j, b: (0, j))] + [_const_spec((half, half))] * len(tables),
        out_specs=pl.BlockSpec((seq, ct), lambda j, b: (b, j)),
        out_shape=jax.ShapeDtypeStruct((bsz * seq, HYENA_WIDTH), BF16),
        scratch_shapes=[pltpu.VMEM((2, half, ct), F32)] * 4 + [pltpu.VMEM((2, 1, ct), F32),
                                                               pltpu.VMEM((seq, ct), F32), pltpu.VMEM((seq, ct), F32)],
        compiler_params=_cparams(("arbitrary", "arbitrary"), 60),
        name="hyena",
    )(z, z, z, conv_w9, conv_w9, conv_w9, conv_b, conv_b, conv_b, a, w_fout4, deltas, tcol, skip, *tables)


def _mix_body(oh_ref, hh_ref, ga_ref, gh_ref, wa_ref, wh_ref, m_ref):
    y_a = jnp.dot(oh_ref[...], wa_ref[...], preferred_element_type=F32)
    y_h = jnp.dot(hh_ref[...], wh_ref[...], preferred_element_type=F32)
    m_ref[...] = (ga_ref[...].astype(F32) * y_a + gh_ref[...].astype(F32) * y_h).astype(BF16)


def _mix(oh, hh, sg, w_a, w_h):
    t = oh.shape[0]
    dm = w_a.shape[1]
    tm = 512
    row = lambda i: (i, 0)
    return pl.pallas_call(
        _mix_body,
        grid=(t // tm,),
        in_specs=[pl.BlockSpec((tm, MLSTM_WIDTH), row), pl.BlockSpec((tm, HYENA_WIDTH), row),
                  pl.BlockSpec((tm, dm), row), pl.BlockSpec((tm, dm), lambda i: (i, 1)),
                  _const_spec((MLSTM_WIDTH, dm)), _const_spec((HYENA_WIDTH, dm))],
        out_specs=pl.BlockSpec((tm, dm), row),
        out_shape=jax.ShapeDtypeStruct((t, dm), BF16),
        compiler_params=_cparams(("parallel",), 48),
        name="mix",
    )(oh, hh, sg, sg, w_a, w_h)


def _merge_body(m_ref, x_ref, lng_ref, lnb_ref, g1_ref, l1g_ref, l1b_ref, sc2_ref, sh2_ref, wo_ref, wr_ref, br_ref,
                x1_ref, lg_ref, mo_s, *, tm):
    mo_s[...] = jnp.dot(m_ref[...], wo_ref[...], preferred_element_type=F32)
    wr = wr_ref[...]
    wr_hi = wr.astype(BF16)
    wr_lo = (wr - wr_hi.astype(F32)).astype(BF16)

    def rows(r, carry):
        sl = pl.ds(pl.multiple_of(r * 128, 128), 128)
        x0 = _layer_norm(x_ref[sl, :], lng_ref[...], lnb_ref[...])
        x1 = _layer_norm(DEEPNORM_ALPHA * x0 + g1_ref[0] * mo_s[sl, :], l1g_ref[...], l1b_ref[...])
        x1_ref[sl, :] = x1
        tok = x1 * (1.0 + sc2_ref[0]) + sh2_ref[0]
        t_hi = tok.astype(BF16)
        t_lo = (tok - t_hi.astype(F32)).astype(BF16)
        lg_ref[sl, :] = (jnp.dot(t_hi, wr_hi, preferred_element_type=F32)
                         + jnp.dot(t_lo, wr_hi, preferred_element_type=F32)
                         + jnp.dot(t_hi, wr_lo, preferred_element_type=F32) + br_ref[...])
        return carry
    lax.fori_loop(0, tm // 128, rows, 0)


def _merge(mix, x2d, ln_g, ln_b, g1, ln1_g, ln1_b, sc2, sh2, w_o, w_r, b_r, seq):
    t, dm = x2d.shape
    tm = 512
    per_b = seq // tm
    row = lambda i: (i, 0)
    mod = lambda i: (i // per_b, 0, 0)
    return pl.pallas_call(
        functools.partial(_merge_body, tm=tm),
        grid=(t // tm,),
        scratch_shapes=[pltpu.VMEM((tm, dm), F32)],
        in_specs=[pl.BlockSpec((tm, dm), row), pl.BlockSpec((tm, dm), row),
                  _const_spec((1, dm)), _const_spec((1, dm)),
                  pl.BlockSpec((1, 1, dm), mod),
                  _const_spec((1, dm)), _const_spec((1, dm)),
                  pl.BlockSpec((1, 1, dm), mod), pl.BlockSpec((1, 1, dm), mod),
                  _const_spec((dm, dm)), _const_spec((dm, LANES)), _const_spec((1, LANES))],
        out_specs=[pl.BlockSpec((tm, dm), row), pl.BlockSpec((tm, LANES), row)],
        out_shape=[jax.ShapeDtypeStruct((t, dm), F32), jax.ShapeDtypeStruct((t, LANES), F32)],
        compiler_params=_cparams(("parallel",), 56),
        name="merge",
    )(mix, x2d, ln_g, ln_b, g1, ln1_g, ln1_b, sc2, sh2, w_o, w_r, b_r)


def _route_body(lg_ref, w_ref, d_ref, cnt_ref, run_s, tot_s, *, tr):
    phase = pl.program_id(0)

    @pl.when(pl.program_id(1) == 0)
    def _():
        @pl.when(phase == 1)
        def _():
            tot_s[...] = run_s[...]
        run_s[...] = jnp.zeros_like(run_s)

    lane = lax.broadcasted_iota(I32, (tr, LANES), 1)
    lane_f = lane.astype(F32)
    logit = lg_ref[...]
    hot, val = [], []
    for _ in range(TOP_K):
        mk = jnp.max(logit, axis=-1, keepdims=True)
        ik = jnp.min(jnp.where(logit == mk, lane_f, float(LANES)), axis=-1, keepdims=True)
        hk = lane_f == ik
        logit = jnp.where(hk, -jnp.inf, logit)
        hot.append(hk)
        val.append(mk)
    cnt = jnp.zeros((tr, LANES), F32)
    for hk in hot:
        cnt = cnt + jnp.where(hk, 1.0, 0.0)
    run = run_s[...] + jnp.sum(cnt, axis=0, keepdims=True)

    @pl.when(phase == 0)
    def _():
        cnt_ref[...] = run.astype(I32)

    @pl.when(phase == 1)
    def _():
        total = tot_s[...]
        padded = jnp.floor((total + (MOE_BLOCK - 1.0)) * (1.0 / MOE_BLOCK)) * MOE_BLOCK
        lane8 = lax.broadcasted_iota(I32, (8, LANES), 1)
        incl = padded
        sft = 1
        while sft < LANES:
            incl = incl + jnp.where(lane8 >= sft, pltpu.roll(incl, sft, 1), 0.0)
            sft *= 2
        pstart = (incl - padded)[0:1, :]
        lower = (lax.broadcasted_iota(I32, (tr, tr), 0) > lax.broadcasted_iota(I32, (tr, tr), 1))
        before = jnp.dot(jnp.where(lower, 1.0, 0.0).astype(BF16), cnt.astype(BF16),
                         preferred_element_type=F32) + (run_s[0:1, :] + pstart)
        ex = [jnp.exp(v - val[0]) for v in val]
        denom = ex[0] + ex[1] + ex[2] + ex[3]
        w_out = jnp.zeros((tr, LANES), F32)
        d_out = jnp.zeros((tr, LANES), I32)
        for k in range(TOP_K):
            dest = jnp.sum(jnp.where(hot[k], before, 0.0), axis=-1, keepdims=True)
            w_out = jnp.where(lane == k, ex[k] / denom, w_out)
            d_out = jnp.where(lane == k, dest.astype(I32), d_out)
        w_ref[...] = w_out
        d_ref[...] = d_out[:, :TOP_K]
        cnt_ref[...] = total.astype(I32)

    run_s[...] = run


def _route(logits):
    t = logits.shape[0]
    tr = 512
    row = lambda p, i: (i, 0)
    out_row = lambda p, i: (i * p, 0)
    return pl.pallas_call(
        functools.partial(_route_body, tr=tr),
        grid=(2, t // tr),
        in_specs=[pl.BlockSpec((tr, LANES), row)],
        out_specs=[pl.BlockSpec((tr, LANES), out_row), pl.BlockSpec((tr, TOP_K), out_row),
                   pl.BlockSpec((8, LANES), lambda p, i: (0, 0))],
        out_shape=[jax.ShapeDtypeStruct((t, LANES), F32), jax.ShapeDtypeStruct((t, TOP_K), I32),
                   jax.ShapeDtypeStruct((8, LANES), I32)],
        scratch_shapes=[pltpu.VMEM((8, LANES), F32), pltpu.VMEM((8, LANES), F32)],
        compiler_params=_cparams(("arbitrary", "arbitrary"), 32),
        name="route",
    )(logits)


def _scatter_body(cnt_ref, pstart_ref, used_ref, dest_ref, x1_ref, sc_ref, sh_ref, xb_ref, tok_ref, zero_s, sem, pad_sem,
                  *, ts, n_blocks):
    def row_copy(src, r_src, r_dst, s):
        return pltpu.make_async_copy(src.at[pl.ds(r_src, 1)], xb_ref.at[pl.ds(r_dst, 1)], s)

    def block_copy(blk):
        return pltpu.make_async_copy(zero_s, xb_ref.at[pl.ds(pl.multiple_of(blk * MOE_BLOCK, MOE_BLOCK), MOE_BLOCK)],
                                     pad_sem)

    @pl.when(pl.program_id(0) == 0)
    def _():
        zero_s[...] = jnp.zeros_like(zero_s)

        def per_expert(e, carry):
            cnt = cnt_ref[e]
            lo = pstart_ref[e] + cnt
            hi = pstart_ref[e] + (cnt + MOE_BLOCK - 1) // MOE_BLOCK * MOE_BLOCK

            def start(r, c):
                row_copy(zero_s, 0, r, pad_sem).start()
                return c

            def wait(r, c):
                row_copy(zero_s, 0, r, pad_sem).wait()
                return c
            lax.fori_loop(lo, hi, start, 0)
            lax.fori_loop(lo, hi, wait, 0)
            return carry
        lax.fori_loop(0, N_EXPERTS, per_expert, 0)

        def tail_start(blk, c):
            block_copy(blk).start()
            return c

        def tail_wait(blk, c):
            block_copy(blk).wait()
            return c
        lax.fori_loop(used_ref[0], n_blocks, tail_start, 0)
        lax.fori_loop(used_ref[0], n_blocks, tail_wait, 0)

    tok_ref[...] = x1_ref[...] * (1.0 + sc_ref[0]) + sh_ref[0]

    def row(r, carry):
        for k in range(TOP_K):
            row_copy(tok_ref, r, dest_ref[r * TOP_K + k], sem).start()
        return carry
    lax.fori_loop(0, ts, row, 0)
    for _ in range(TOP_K):
        pltpu.make_async_copy(tok_ref, xb_ref.at[pl.ds(0, ts)], sem).wait()


def _scatter(x1, sc2, sh2, dest_flat, counts, pstart, used, n_blocks, seq):
    t, dm = x1.shape
    ts = 256
    per_b = seq // ts
    grid_spec = pltpu.PrefetchScalarGridSpec(
        num_scalar_prefetch=3,
        grid=(t // ts,),
        in_specs=[pl.BlockSpec((ts * TOP_K,), lambda i, *_: (i,), memory_space=pltpu.SMEM),
                  pl.BlockSpec((ts, dm), lambda i, *_: (i, 0)),
                  pl.BlockSpec((1, 1, dm), lambda i, *_: (i // per_b, 0, 0)),
                  pl.BlockSpec((1, 1, dm), lambda i, *_: (i // per_b, 0, 0))],
        out_specs=pl.BlockSpec(memory_space=pl.ANY),
        scratch_shapes=[pltpu.VMEM((ts, dm), F32), pltpu.VMEM((MOE_BLOCK, dm), F32),
                        pltpu.SemaphoreType.DMA(()), pltpu.SemaphoreType.DMA(())],
    )
    return pl.pallas_call(
        functools.partial(_scatter_body, ts=ts, n_blocks=n_blocks),
        grid_spec=grid_spec,
        out_shape=jax.ShapeDtypeStruct((n_blocks * MOE_BLOCK, dm), F32),
        compiler_params=_cparams(("arbitrary",), 32),
        name="scatter",
    )(counts, pstart, used, dest_flat, x1, sc2, sh2)


def _ffn1_body(e_ref, c_ref, blk_ref, oblk_ref, oc_ref, first_ref, n_ref,
               x_ref, wg_ref, wu_ref, bg_ref, bu_ref, a_ref, wg_s, wu_s):
    s = pl.program_id(0)

    @pl.when(first_ref[s] == 1)
    def _():
        wg_s[...] = wg_ref[0].astype(BF16)
        wu_s[...] = wu_ref[0].astype(BF16)

    @pl.when(s < n_ref[0])
    def _():
        x = x_ref[...].astype(BF16)
        g = jnp.minimum(jnp.dot(x, wg_s[...], preferred_element_type=F32) + bg_ref[0], SWIGLU_LIMIT)
        u = jnp.clip(jnp.dot(x, wu_s[...], preferred_element_type=F32) + bu_ref[0], -SWIGLU_LIMIT, SWIGLU_LIMIT)
        a_ref[...] = (g * _sigmoid(SWIGLU_ALPHA * g) * (u + 1.0)).astype(BF16)

    @pl.when(s >= n_ref[0])
    def _():
        a_ref[...] = jnp.zeros_like(a_ref)


def _ffn2_body(e_ref, c_ref, blk_ref, oblk_ref, oc_ref, first_ref, n_ref, a_ref, wd_ref, bd_ref, y_ref, wd_s):
    s = pl.program_id(0)

    @pl.when(first_ref[s] == 1)
    def _():
        wd_s[...] = wd_ref[0].astype(BF16)

    @pl.when(s < n_ref[0])
    def _():
        y_ref[...] = jnp.dot(a_ref[...], wd_s[...], preferred_element_type=F32) + bd_ref[0]

    @pl.when(s >= n_ref[0])
    def _():
        y_ref[...] = jnp.zeros_like(y_ref)


def _ffn1(plan, xb, w_gate, w_up, b_gate, b_up, n_steps):
    rows, dm = xb.shape
    de = w_gate.shape[2]
    wspec = pl.BlockSpec((1, dm, FF_CHUNK), lambda s, e, c, *_: (e[s], 0, c[s]))
    bspec = pl.BlockSpec((1, 1, FF_CHUNK), lambda s, e, c, *_: (e[s], 0, c[s]))
    grid_spec = pltpu.PrefetchScalarGridSpec(
        num_scalar_prefetch=7,
        grid=(n_steps,),
        in_specs=[pl.BlockSpec((MOE_BLOCK, dm), lambda s, e, c, blk, *_: (blk[s], 0)), wspec, wspec, bspec, bspec],
        out_specs=pl.BlockSpec((MOE_BLOCK, FF_CHUNK), lambda s, e, c, blk, oblk, oc, *_: (oblk[s], oc[s])),
        scratch_shapes=[pltpu.VMEM((dm, FF_CHUNK), BF16), pltpu.VMEM((dm, FF_CHUNK), BF16)],
    )
    return pl.pallas_call(
        _ffn1_body, grid_spec=grid_spec,
        out_shape=jax.ShapeDtypeStruct((rows, de), BF16),
        compiler_params=_cparams(("arbitrary",), 60),
        name="ffn1",
    )(*plan, xb, w_gate, w_up, b_gate, b_up)


def _ffn2(plan, act, w_down, b_down, n_steps, chunk):
    rows, de = act.shape
    dm = w_down.shape[2]
    grid_spec = pltpu.PrefetchScalarGridSpec(
        num_scalar_prefetch=7,
        grid=(n_steps,),
        in_specs=[pl.BlockSpec((MOE_BLOCK, de), lambda s, e, c, blk, *_: (blk[s], 0)),
                  pl.BlockSpec((1, de, chunk), lambda s, e, c, *_: (e[s], 0, c[s])),
                  pl.BlockSpec((1, 1, chunk), lambda s, e, c, *_: (e[s], 0, c[s]))],
        out_specs=pl.BlockSpec((MOE_BLOCK, chunk), lambda s, e, c, blk, oblk, oc, *_: (oblk[s], oc[s])),
        scratch_shapes=[pltpu.VMEM((de, chunk), BF16)],
    )
    return pl.pallas_call(
        _ffn2_body, grid_spec=grid_spec,
        out_shape=jax.ShapeDtypeStruct((rows, dm), F32),
        compiler_params=_cparams(("arbitrary",), 60),
        name="ffn2",
    )(*plan, act, w_down, b_down)


def _combine_body(dcur_ref, dnxt_ref, w_ref, x1_ref, g2_ref, lg_ref, lb_ref, yb_ref, o_ref, buf, sem, *, tc, n_tiles):
    i = pl.program_id(0)

    def issue(d_ref, slot):
        def row(r, carry):
            for k in range(TOP_K):
                pltpu.make_async_copy(yb_ref.at[pl.ds(d_ref[r * TOP_K + k], 1)],
                                      buf.at[slot, k, pl.ds(r, 1)], sem.at[slot]).start()
            return carry
        lax.fori_loop(0, tc, row, 0)

    @pl.when(i == 0)
    def _():
        issue(dcur_ref, 0)

    @pl.when(i + 1 < n_tiles)
    def _():
        issue(dnxt_ref, (i + 1) % 2)

    slot = i % 2
    for k in range(TOP_K):
        pltpu.make_async_copy(yb_ref.at[pl.ds(0, tc)], buf.at[slot, k], sem.at[slot]).wait()
    w = w_ref[...]
    y = w[:, 0:1] * buf[slot, 0]
    for k in range(1, TOP_K):
        y = y + w[:, k:k + 1] * buf[slot, k]
    o_ref[...] = _layer_norm(DEEPNORM_ALPHA * x1_ref[...] + g2_ref[0] * y, lg_ref[...], lb_ref[...])


def _combine(dest_flat, w4, x1, g2, ln_g, ln_b, yb, seq):
    t, dm = x1.shape
    tc = 256
    n_tiles = t // tc
    per_b = seq // tc
    row = lambda i: (i, 0)
    return pl.pallas_call(
        functools.partial(_combine_body, tc=tc, n_tiles=n_tiles),
        grid=(n_tiles,),
        in_specs=[pl.BlockSpec((tc * TOP_K,), lambda i: (i,), memory_space=pltpu.SMEM),
                  pl.BlockSpec((tc * TOP_K,), lambda i: (jnp.minimum(i + 1, n_tiles - 1),), memory_space=pltpu.SMEM),
                  pl.BlockSpec((tc, LANES), row),
                  pl.BlockSpec((tc, dm), row),
                  pl.BlockSpec((1, 1, dm), lambda i: (i // per_b, 0, 0)),
                  _const_spec((1, dm)), _const_spec((1, dm)),
                  pl.BlockSpec(memory_space=pl.ANY)],
        out_specs=pl.BlockSpec((tc, dm), row),
        out_shape=jax.ShapeDtypeStruct((t, dm), F32),
        scratch_shapes=[pltpu.VMEM((2, TOP_K, tc, dm), F32), pltpu.SemaphoreType.DMA((2,))],
        compiler_params=_cparams(("arbitrary",), 40),
        name="combine",
    )(dest_flat, dest_flat, w4, x1, g2, ln_g, ln_b, yb)


def _dft_tables(seq):
    n = 2 * seq
    j = np.arange(seq // 2, dtype=np.int64)
    tables = []
    for k in (2 * j, 2 * j + 1):
        ang = (2.0 * np.pi / n) * ((k[:, None] * j[None, :]) % n).astype(np.float64)
        tables += [np.cos(ang), np.sin(ang)]
    tables += [tables[2].T, tables[3].T]
    return tuple(jnp.asarray(m, F32).astype(BF16) for m in tables)


def _filter_features(seq):
    t = jnp.linspace(0.0, 1.0, seq, dtype=F32)[:, None]
    bands = (HYENA_EMB - 1) // 2
    f = jnp.linspace(1e-4, bands - 1, bands, dtype=F32)[None, :]
    ang = 2.0 * math.pi * jnp.arange(seq, dtype=F32)[:, None] * f / seq
    feats = jnp.concatenate([t, jnp.cos(ang), -jnp.sin(ang)], axis=-1)
    max_decay = math.log(HYENA_DECAY_TARGET) / HYENA_SHORT_DECAY_PCT
    min_decay = math.log(HYENA_DECAY_TARGET) / HYENA_LONG_DECAY_PCT
    deltas = jnp.abs(jnp.linspace(min_decay, max_decay, HYENA_WIDTH, dtype=F32))[None, :]
    return t, feats, deltas


def _moe_plan(counts, n_blocks, n_chunks):
    ids = jnp.arange(N_EXPERTS, dtype=I32)
    nblk = (counts + MOE_BLOCK - 1) // MOE_BLOCK
    blk_end = jnp.cumsum(nblk)
    blk_start = blk_end - nblk
    used = blk_end[-1]
    steps = n_chunks * nblk
    step_end = jnp.cumsum(steps)
    n_used = step_end[-1]
    s_all = jnp.arange(n_chunks * n_blocks, dtype=I32)
    s = jnp.minimum(s_all, n_used - 1)
    e_s = jnp.minimum(jnp.sum((s[:, None] >= step_end[None, :]).astype(I32), axis=1), N_EXPERTS - 1)
    onehot = e_s[:, None] == ids[None, :]
    pick = lambda table: jnp.sum(jnp.where(onehot, table[None, :], 0), axis=1)
    loc = s - pick(step_end - steps)
    nb = jnp.maximum(pick(nblk), 1)
    c_s = loc // nb
    r_s = loc % nb
    blk = pick(blk_start) + r_s
    tail = s_all >= n_used
    j = jnp.maximum(s_all - n_used, 0)
    n_tail = jnp.maximum(n_blocks - used, 1)
    oblk = jnp.where(tail, used + j % n_tail, blk)
    oc = jnp.where(tail, j // n_tail, c_s)
    first = jnp.logical_and(r_s == 0, jnp.logical_not(tail))
    as_i32 = lambda v: v.astype(I32)
    plan = tuple(map(as_i32, (e_s, c_s, blk, oblk, oc, first, n_used.reshape(1))))
    return as_i32(blk_start * MOE_BLOCK), as_i32(used.reshape(1)), plan


def kernel(x, c, ctx, c_ctx, ln_in_g, ln_in_b, w_mod, b_mod, w_in, b_in, mlstm_conv_w, mlstm_conv_b,
           w_qh, w_kh, hyena_conv_w, hyena_conv_b, filt_w1, filt_b1, filt_wh, filt_bh, filt_freq, filt_wout,
           hyena_skip, w_proj_a, w_proj_h, w_out, ln1_g, ln1_b, w_router, b_router, w_gate, b_gate,
           w_up, b_up, w_down, b_down, ln2_g, ln2_b):
    bsz, seq, dm = x.shape
    ctx_len = ctx.shape[1]
    t = bsz * seq
    assert w_mod.shape[0] == DEPTH and dm == D_MODEL and ctx_len == CHUNK and bsz + 1 <= 16
    row = lambda v: v.reshape(1, -1)

    cond = jnp.concatenate([c, c_ctx[None], jnp.zeros((16 - bsz - 1, dm), F32)], axis=0)
    mod = _mod(cond, w_mod[0], row(b_mod[0]))
    sh1, sc1, g1, sh2, sc2, g2 = [m[:, None, :] for m in jnp.split(mod, 6, axis=-1)]

    w_main = jnp.concatenate([w_in[0][:, :IN_GATES], w_in[0][:, IN_O:]], axis=1).astype(BF16)
    b_main = row(jnp.concatenate([b_in[0][:IN_GATES], b_in[0][IN_O:]]))
    w_g = jnp.pad(w_in[0][:, IN_GATES:IN_O], ((0, 0), (0, LANES - 4 * HEADS))).astype(BF16)
    b_g = row(jnp.pad(b_in[0][IN_GATES:IN_O], (0, LANES - 4 * HEADS)))
    lng, lnb = row(ln_in_g), row(ln_in_b)
    x2d = x.reshape(t, dm)
    z, gates, sg = _in_proj(x2d, lng, lnb, sc1[:bsz], sh1[:bsz], w_main, b_main, w_g, b_g, seq, Z_BG)
    zc, gates_c = _in_proj(ctx.reshape(bsz * ctx_len, dm), lng, lnb, sc1[bsz:bsz + 1], sh1[bsz:bsz + 1],
                           w_main[:, :IN_GATES], b_main[:, :IN_GATES], w_g, b_g, bsz * ctx_len, IN_GATES)

    g_all = jnp.concatenate([gates_c[:, :4 * HEADS].reshape(bsz, ctx_len, 4, HEADS),
                             gates[:, :4 * HEADS].reshape(bsz, seq, 4, HEADS)], axis=1)
    gates_t = g_all.transpose(0, 3, 2, 1)

    oh = _mlstm(z, zc, gates_t, mlstm_conv_w[0].reshape(9, MLSTM_WIDTH), row(mlstm_conv_b[0]),
                w_qh[0], w_kh[0].transpose(0, 2, 1), bsz, seq, ctx_len)

    tcol, feats, deltas = _filter_features(seq)
    feats = jnp.pad(feats, ((0, 0), (0, LANES - HYENA_EMB)))
    w1 = jnp.pad(filt_w1[0], ((0, LANES - HYENA_EMB), (0, 0)))
    a = _filt(feats, w1, row(filt_b1[0]), filt_wh[0], filt_bh[0], filt_freq[0])
    fh = a.shape[1]
    w_fout4 = filt_wout[0].reshape(fh, 4, HYENA_WIDTH).transpose(1, 0, 2)
    hh = _hyena(z, hyena_conv_w[0].reshape(9, 3 * HYENA_WIDTH), row(hyena_conv_b[0]), a, w_fout4, deltas, tcol,
                hyena_skip[0], _dft_tables(seq), bsz, seq)

    w_r = jnp.pad(w_router[0], ((0, 0), (0, LANES - N_EXPERTS)))
    b_r = row(jnp.pad(b_router[0], (0, LANES - N_EXPERTS), constant_values=-1e30))
    mix = _mix(oh, hh, sg, w_proj_a[0].astype(BF16), w_proj_h[0].astype(BF16))
    x1, logits = _merge(mix, x2d, lng, lnb, g1[:bsz], row(ln1_g[0]), row(ln1_b[0]), sc2[:bsz], sh2[:bsz],
                        w_out[0].astype(BF16), w_r, b_r, seq)

    w4, dest, counts = _route(logits)
    n_blocks = -(-(t * TOP_K + N_EXPERTS * (MOE_BLOCK - 1)) // MOE_BLOCK)
    n_chunks = dm // FF_CHUNK
    counts = counts[0, :N_EXPERTS]
    pstart, used, plan = _moe_plan(counts, n_blocks, n_chunks)
    dest_flat = dest.reshape(t * TOP_K)
    xb = _scatter(x1, sc2[:bsz], sh2[:bsz], dest_flat, counts, pstart, used, n_blocks, seq)
    act = _ffn1(plan, xb, w_gate[0], w_up[0], b_gate[0][:, None, :], b_up[0][:, None, :], n_chunks * n_blocks)
    _, _, plan_down = _moe_plan(counts, n_blocks, 1)
    yb = _ffn2(plan_down, act, w_down[0], b_down[0][:, None, :], n_blocks, dm)
    out = _combine(dest_flat, w4, x1, g2[:bsz], row(ln2_g[0]), row(ln2_b[0]), yb, seq)
    return out.reshape(bsz, seq, dm)
```

```python
import functools
import math

import numpy as np
import jax
import jax.numpy as jnp
from jax import lax
from jax.experimental import pallas as pl
from jax.experimental.pallas import tpu as pltpu

F32 = jnp.float32
BF16 = jnp.bfloat16
I32 = jnp.int32
HIGHEST = lax.Precision.HIGHEST

D_MODEL = 2048
GRID_W = 64
HEADS = 4
HEAD_DIM = 256
MLSTM_WIDTH = HEADS * HEAD_DIM
HYENA_WIDTH = D_MODEL // 2
HYENA_EMB = 33
HYENA_DECAY_TARGET = 1e-2
HYENA_SHORT_DECAY_PCT = 0.3
HYENA_LONG_DECAY_PCT = 1.5
N_EXPERTS = 32
TOP_K = 4
SWIGLU_LIMIT = 7.0
SWIGLU_ALPHA = 1.702
LN_EPS = 1e-5
DEPTH = 1
DEEPNORM_ALPHA = (2.0 * DEPTH) ** 0.25

IN_V = MLSTM_WIDTH
IN_GATES = 2 * MLSTM_WIDTH
IN_O = IN_GATES + 4 * HEADS
Z_QK, Z_V, Z_O, Z_HY, Z_BG = 0, 1024, 2048, 3072, 6144

LANES = 128
MXU = 256
CHUNK = 256
MOE_BLOCK = 512
ROW_STEP = 128
FF_CHUNK = 1024
MIB = 1024 * 1024


def _cparams(semantics, vmem_mib):
    return pltpu.CompilerParams(dimension_semantics=semantics, vmem_limit_bytes=vmem_mib * MIB)


def _const_spec(shape):
    nd = len(shape)
    return pl.BlockSpec(shape, lambda *_: (0,) * nd, pipeline_mode=pl.Buffered(1))


def _layer_norm(x, g, b):
    mu = jnp.mean(x, axis=-1, keepdims=True)
    xc = x - mu
    var = jnp.mean(xc * xc, axis=-1, keepdims=True)
    return xc * lax.rsqrt(var + LN_EPS) * g + b


def _sigmoid(x):
    return 1.0 / (1.0 + jnp.exp(-x))


def _silu(x):
    return x * _sigmoid(x)


def _log_sigmoid(x):
    return jnp.minimum(x, 0.0) - jnp.log(1.0 + jnp.exp(-jnp.abs(x)))


def _mod_body(c_ref, w_ref, b_ref, o_ref):
    s = _silu(c_ref[...])
    o_ref[...] = jnp.dot(s.astype(BF16), w_ref[...].astype(BF16), preferred_element_type=F32) + b_ref[...]


def _mod(cond, w, b):
    rows, dm = cond.shape
    n = w.shape[1]
    tn = 1024
    return pl.pallas_call(
        _mod_body,
        grid=(n // tn,),
        in_specs=[pl.BlockSpec((rows, dm), lambda j: (0, 0)),
                  pl.BlockSpec((dm, tn), lambda j: (0, j)),
                  pl.BlockSpec((1, tn), lambda j: (0, j))],
        out_specs=pl.BlockSpec((rows, tn), lambda j: (0, j)),
        out_shape=jax.ShapeDtypeStruct((rows, n), F32),
        compiler_params=_cparams(("arbitrary",), 40),
        name="mod",
    )(cond, w, b)


def _in_proj_body(x_ref, lng_ref, lnb_ref, sc_ref, sh_ref, w_ref, b_ref, wg_ref, bg_ref,
                  z_ref, g_ref, *rest, tm, n_plain):
    hx_s = rest[-1]

    @pl.when(pl.program_id(1) == 0)
    def _():
        def rows(r, carry):
            sl = pl.ds(pl.multiple_of(r * 128, 128), 128)
            xn = _layer_norm(x_ref[sl, :], lng_ref[...], lnb_ref[...])
            hx_s[sl, :] = (xn * (1.0 + sc_ref[0]) + sh_ref[0]).astype(BF16)
            return carry
        lax.fori_loop(0, tm // 128, rows, 0)
        g_ref[...] = jnp.dot(hx_s[...], wg_ref[...], preferred_element_type=F32) + bg_ref[...]

    acc = jnp.dot(hx_s[...], w_ref[...], preferred_element_type=F32) + b_ref[...]
    if len(rest) == 1:
        z_ref[...] = acc
    else:
        @pl.when(pl.program_id(1) < n_plain)
        def _():
            z_ref[...] = acc

        @pl.when(pl.program_id(1) >= n_plain)
        def _():
            rest[0][...] = _sigmoid(acc).astype(BF16)


def _in_proj(x2d, ln_g, ln_b, scale, shift, w, b, w_gates, b_gates, rows_per_mod, plain_cols):
    t, dm = x2d.shape
    n = w.shape[1]
    tm, tn = min(1024, t), 1024
    n_plain = plain_cols // tn
    out_specs = [pl.BlockSpec((tm, tn), lambda i, j: (i, jnp.minimum(j, n_plain - 1))),
                 pl.BlockSpec((tm, LANES), lambda i, j: (i, 0))]
    out_shape = [jax.ShapeDtypeStruct((t, plain_cols), F32), jax.ShapeDtypeStruct((t, LANES), F32)]
    if n > plain_cols:
        out_specs.append(pl.BlockSpec((tm, tn), lambda i, j: (i, jnp.maximum(j - n_plain, 0))))
        out_shape.append(jax.ShapeDtypeStruct((t, n - plain_cols), BF16))
    return pl.pallas_call(
        functools.partial(_in_proj_body, tm=tm, n_plain=n_plain),
        grid=(t // tm, n // tn),
        in_specs=[pl.BlockSpec((tm, dm), lambda i, j: (i, 0)),
                  pl.BlockSpec((1, dm), lambda i, j: (0, 0)),
                  pl.BlockSpec((1, dm), lambda i, j: (0, 0)),
                  pl.BlockSpec((1, 1, dm), lambda i, j: (i * tm // rows_per_mod, 0, 0)),
                  pl.BlockSpec((1, 1, dm), lambda i, j: (i * tm // rows_per_mod, 0, 0)),
                  pl.BlockSpec((dm, tn), lambda i, j: (0, j)),
                  pl.BlockSpec((1, tn), lambda i, j: (0, j)),
                  pl.BlockSpec((dm, LANES), lambda i, j: (0, 0)),
                  pl.BlockSpec((1, LANES), lambda i, j: (0, 0))],
        out_specs=out_specs,
        out_shape=out_shape,
        scratch_shapes=[pltpu.VMEM((tm, dm), BF16)],
        compiler_params=_cparams(("parallel", "arbitrary"), 48),
        name="in_proj",
    )(x2d, ln_g, ln_b, scale, shift, w, b, w_gates, b_gates)


def _dwconv(u, w9, bias, width, single_row):
    length, ch = u.shape
    col = lax.broadcasted_iota(I32, (length, ch), 0) % width
    if not single_row:
        zpad = jnp.zeros((width, ch), F32)
        up = jnp.concatenate([zpad, u[:length - width]], axis=0)
        dn = jnp.concatenate([u[width:], zpad], axis=0)
    out = None
    for dc in (-1, 0, 1):
        a = u * w9[4 + dc:5 + dc]
        if not single_row:
            a = a + up * w9[1 + dc:2 + dc] + dn * w9[7 + dc:8 + dc]
        if dc == -1:
            a = jnp.where(col == 0, 0.0, pltpu.roll(a, 1, 0))
        elif dc == 1:
            a = jnp.where(col == width - 1, 0.0, pltpu.roll(a, length - 1, 0))
        out = a if out is None else out + a
    return out + bias


def _mlstm_body(zqk_ref, zv_ref, zo_ref, cqk_ref, cv_ref, gt_ref, cw_ref, cb_ref, wq_ref, wkt_ref,
                o_ref, q_s, k_s, kt_s, v_s, hf_s, hb_s, r_s, c_s, ct_s, *, ctx_len, seq):
    n_chunks = (ctx_len + seq) // CHUNK
    total = ctx_len + seq
    cw = cw_ref[...]
    cb = cb_ref[...]
    wq = wq_ref[0].astype(BF16)
    wkt = wkt_ref[0].astype(BF16)
    nt = (((1,), (1,)), ((), ()))
    scale = HEAD_DIM ** -0.5

    def project(u, off, n):
        ub = u.astype(BF16)
        q_s[off:off + n, :] = jnp.dot(ub, wq, preferred_element_type=F32).astype(BF16)
        k_s[off:off + n, :] = (lax.dot_general(ub, wkt, nt, preferred_element_type=F32) * scale).astype(BF16)
        kt_s[:, off:off + n] = (lax.dot_general(wkt, ub, nt, preferred_element_type=F32) * scale).astype(BF16)

    project(_silu(_dwconv(cqk_ref[...], cw, cb, ctx_len, True)), 0, ctx_len)
    project(_silu(_dwconv(zqk_ref[...], cw, cb, GRID_W, False)), ctx_len, seq)
    v_s[0:ctx_len, :] = cv_ref[...]
    v_s[ctx_len:total, :] = zv_ref[...]

    gt = gt_ref[0, 0]
    lf = _log_sigmoid(gt)
    pos = lax.broadcasted_iota(I32, (4, total), 1) % CHUNK
    pre, suf = lf, lf
    s = 1
    while s < CHUNK:
        pre = pre + jnp.where(pos >= s, pltpu.roll(pre, s, 1), 0.0)
        suf = suf + jnp.where(pos < CHUNK - s, pltpu.roll(suf, total - s, 1), 0.0)
        s *= 2
    b_f, li_f, b_b, li_b = pre[1:2], gt[0:1], suf[3:4], gt[2:3]
    pm_f, pm_b = li_f - b_f, li_b - b_b
    pos1 = pos[0:1]
    s = 1
    while s < CHUNK:
        pm_f = jnp.maximum(pm_f, jnp.where(pos1 >= s, pltpu.roll(pm_f, s, 1), -jnp.inf))
        pm_b = jnp.maximum(pm_b, jnp.where(pos1 < CHUNK - s, pltpu.roll(pm_b, total - s, 1), -jnp.inf))
        s *= 2
    r_s[...] = jnp.concatenate([b_f, li_f, pm_f, b_b, li_b, pm_b, jnp.zeros((2, total), F32)], axis=0)
    zfill = jnp.zeros((CHUNK - 8, CHUNK), F32)
    for c in range(n_chunks):
        blk = jnp.concatenate([r_s[:, c * CHUNK:(c + 1) * CHUNK], zfill], axis=0).T
        c_s[c * CHUNK:(c + 1) * CHUNK, :] = blk[:, :LANES]

    ct_s[...] = jnp.zeros_like(ct_s)
    row_i = lax.broadcasted_iota(I32, (CHUNK, CHUNK), 0)
    col_i = lax.broadcasted_iota(I32, (CHUNK, CHUNK), 1)
    masks = (row_i >= col_i, row_i <= col_i)

    def chunk_step(c, d, n_vec, m):
        off = c * CHUNK if isinstance(c, int) else pl.multiple_of(c * CHUNK, CHUNK)
        rows = r_s[:, pl.ds(off, CHUNK)]
        cols = c_s[pl.ds(off, CHUNK), :]
        b_row, li_row, pm_row = rows[3 * d:3 * d + 1], rows[3 * d + 1:3 * d + 2], rows[3 * d + 2:3 * d + 3]
        b_col, li_col, pm_col = cols[:, 3 * d:3 * d + 1], cols[:, 3 * d + 1:3 * d + 2], cols[:, 3 * d + 2:3 * d + 3]
        last = slice(CHUNK - 1, CHUNK) if d == 0 else slice(0, 1)
        b_end, pm_end = b_row[:, last], pm_row[:, last]
        qc = q_s[pl.ds(off, CHUNK), :]
        kc = k_s[pl.ds(off, CHUNK), :]
        ktc = kt_s[:, pl.ds(off, CHUNK)]
        vc = v_s[pl.ds(off, CHUNK), :]
        ct = ct_s[d]

        inter = b_col + m
        m_t = jnp.maximum(inter, b_col + pm_col)
        wts = jnp.exp(jnp.where(masks[d], b_col - b_row + li_row, -jnp.inf) - m_t)
        s_inter = jnp.exp(inter - m_t)
        scores = jnp.dot(qc, ktc, preferred_element_type=F32) * wts
        num = (s_inter * jnp.dot(qc, ct.astype(BF16), preferred_element_type=F32)
               + jnp.dot(scores.astype(BF16), vc.astype(BF16), preferred_element_type=F32))
        den = (s_inter * jnp.sum(qc.astype(F32) * n_vec, axis=-1, keepdims=True)
               + jnp.sum(scores, axis=-1, keepdims=True))
        h = num / jnp.maximum(jnp.abs(den), jnp.exp(-m_t))

        m_new = jnp.maximum(b_end + m, b_end + pm_end)
        decay = jnp.exp(b_end + m - m_new)
        w = jnp.exp(b_end - b_col + li_col - m_new)
        ct_s[d] = decay * ct + jnp.dot(ktc, (vc * w).astype(BF16), preferred_element_type=F32)
        n_new = decay * n_vec + jnp.sum(kc.astype(F32) * w, axis=0, keepdims=True)
        return h, n_new, m_new

    n0 = jnp.zeros((1, HEAD_DIM), F32)
    m0 = jnp.zeros((1, 1), F32)
    _, nf, mf = chunk_step(0, 0, n0, m0)
    _, nb, mb = chunk_step(0, 1, n0, m0)

    def body(i, carry):
        nf, mf, nb, mb = carry
        hf, nf, mf = chunk_step(i, 0, nf, mf)
        hf_s[pl.ds(pl.multiple_of(i * CHUNK - ctx_len, CHUNK), CHUNK), :] = hf
        j = n_chunks - i
        hb, nb, mb = chunk_step(j, 1, nb, mb)
        hb_s[pl.ds(pl.multiple_of(j * CHUNK - ctx_len, CHUNK), CHUNK), :] = hb
        return nf, mf, nb, mb

    lax.fori_loop(1, n_chunks, body, (nf, mf, nb, mb))
    o_ref[...] = (_sigmoid(zo_ref[...]) * (hf_s[...] + hb_s[...])).astype(BF16)


def _mlstm(z, zc, gates_t, conv_w9, conv_b, w_qh, w_kh_t, bsz, seq, ctx_len):
    total = ctx_len + seq
    hd = HEAD_DIM
    qk_blk, v_blk, o_blk = Z_QK // hd, Z_V // hd, Z_O // hd
    return pl.pallas_call(
        functools.partial(_mlstm_body, ctx_len=ctx_len, seq=seq),
        grid=(bsz, HEADS),
        in_specs=[pl.BlockSpec((seq, hd), lambda b, h: (b, qk_blk + h)),
                  pl.BlockSpec((seq, hd), lambda b, h: (b, v_blk + h)),
                  pl.BlockSpec((seq, hd), lambda b, h: (b, o_blk + h)),
                  pl.BlockSpec((ctx_len, hd), lambda b, h: (b, qk_blk + h)),
                  pl.BlockSpec((ctx_len, hd), lambda b, h: (b, v_blk + h)),
                  pl.BlockSpec((1, 1, 4, total), lambda b, h: (b, h, 0, 0)),
                  pl.BlockSpec((9, hd), lambda b, h: (0, h)),
                  pl.BlockSpec((1, hd), lambda b, h: (0, h)),
                  pl.BlockSpec((1, hd, hd), lambda b, h: (h, 0, 0)),
                  pl.BlockSpec((1, hd, hd), lambda b, h: (h, 0, 0))],
        out_specs=pl.BlockSpec((seq, hd), lambda b, h: (b, h)),
        out_shape=jax.ShapeDtypeStruct((bsz * seq, MLSTM_WIDTH), BF16),
        scratch_shapes=[pltpu.VMEM((total, hd), BF16), pltpu.VMEM((total, hd), BF16), pltpu.VMEM((hd, total), BF16),
                        pltpu.VMEM((total, hd), F32), pltpu.VMEM((seq, hd), F32), pltpu.VMEM((seq, hd), F32),
                        pltpu.VMEM((8, total), F32), pltpu.VMEM((total, LANES), F32), pltpu.VMEM((2, hd, hd), F32)],
        compiler_params=_cparams(("parallel", "arbitrary"), 48),
        name="mlstm",
    )(z, z, z, zc, zc, gates_t, conv_w9, conv_b, w_qh, w_kh_t)


def _filt_body(z_ref, w1_ref, b1_ref, wh_ref, bh_ref, fr_ref, a_ref):
    fr = fr_ref[...]
    a = jnp.sin(fr[0:1] * (jnp.dot(z_ref[...], w1_ref[...], precision=HIGHEST, preferred_element_type=F32)
                           + b1_ref[...]))
    for i in range(2):
        a = jnp.sin(fr[i + 1:i + 2] * (jnp.dot(a, wh_ref[i], precision=HIGHEST, preferred_element_type=F32)
                                       + bh_ref[i:i + 1]))
    a_ref[...] = a


def _filt(feats, w1, b1, wh, bh, freq):
    length = feats.shape[0]
    fh = w1.shape[1]
    return pl.pallas_call(
        _filt_body,
        out_shape=jax.ShapeDtypeStruct((length, fh), F32),
        name="filt",
    )(feats, w1, b1, wh, bh, freq)


def _hyena_body(zx1_ref, zx2_ref, zv_ref, cw1_ref, cw2_ref, cwv_ref, cb1_ref, cb2_ref, cbv_ref,
                a_ref, wf_ref, dl_ref, t_ref, skip_ref, ce_ref, se_ref, co_ref, so_ref, cot_ref, sot_ref, o_ref,
                hce_s, hse_s, hco_s, hso_s, hn_s, v_s, g_s, *, seq):
    n_fft = 2 * seq
    half = seq // 2
    sign = jnp.where(lax.broadcasted_iota(I32, (seq, 1), 0) % 2 == 0, 1.0, -1.0)
    jrow = lax.broadcasted_iota(I32, (half, 1), 0)
    sgn_j = jnp.where(jrow % 2 == 0, 1.0, -1.0)
    first = jrow == 0
    flip = jnp.where(lax.broadcasted_iota(I32, (MXU, MXU), 0) + lax.broadcasted_iota(I32, (MXU, MXU), 1) == MXU - 1,
                     1.0, 0.0).astype(BF16)

    def mm(m_ref, x):
        return jnp.dot(m_ref[...], x, preferred_element_type=F32)

    def reverse_rows(x):
        hi = x.astype(BF16)
        lo = (x - hi.astype(F32)).astype(BF16)
        blocks = []
        for b in range(half // MXU):
            src = slice(half - MXU * (b + 1), half - MXU * b)
            blocks.append(jnp.dot(flip, hi[src], preferred_element_type=F32)
                          + jnp.dot(flip, lo[src], preferred_element_type=F32))
        return jnp.concatenate(blocks, axis=0)

    def fold(v):
        rolled = pltpu.roll(reverse_rows(v[half:]), 1, 0)
        mid = rolled[0:1]
        vr = jnp.where(first, 0.0, rolled)
        return v[:half] + vr, v[:half] - vr, mid

    def forward(v):
        s, d, mid = fold(v)
        sb, db = s.astype(BF16), d.astype(BF16)
        return mm(ce_ref, sb) + sgn_j * mid, mm(se_ref, db), mm(co_ref, db), mm(so_ref, sb) + sgn_j * mid

    @pl.when(pl.program_id(1) == 0)
    def _():
        window = jnp.exp(-t_ref[...] * dl_ref[...])
        row0 = lax.broadcasted_iota(I32, (seq, 1), 0) == 0
        for o in range(2):
            fwd = jnp.dot(a_ref[...], wf_ref[2 * o], precision=HIGHEST, preferred_element_type=F32) * window
            bwd = jnp.dot(a_ref[...], wf_ref[2 * o + 1], precision=HIGHEST, preferred_element_type=F32) * window
            bwd = jnp.where(row0, 0.0, bwd)
            even = fwd + bwd
            hce_s[o], _, hco_s[o], _ = forward(even)
            _, hse_s[o], _, hso_s[o] = forward(fwd - bwd)
            hn_s[o] = jnp.sum(even * sign, axis=0, keepdims=True)

    def conv_to(dst_ref, z_ref, cw_ref, cb_ref):
        for lo in range(0, z_ref.shape[1], LANES):
            ls = slice(lo, lo + LANES)
            dst_ref[:, ls] = _dwconv(z_ref[:, ls], cw_ref[:, ls], cb_ref[:, ls], GRID_W, False)

    conv_to(v_s, zv_ref, cwv_ref, cbv_ref)

    for o, (zg_ref, cwg_ref, cbg_ref) in enumerate(((zx1_ref, cw1_ref, cb1_ref), (zx2_ref, cw2_ref, cb2_ref))):
        v = v_s[...]
        nyq = jnp.sum(v * sign, axis=0, keepdims=True) * hn_s[o] * (1.0 / n_fft)
        xce, xse, xco, xso = forward(v)
        hce, hse, hco, hso = hce_s[o], hse_s[o], hco_s[o], hso_s[o]
        scale_e = jnp.where(first, 1.0 / n_fft, 2.0 / n_fft)
        zce = scale_e * (xce * hce - xse * hse)
        zse = scale_e * (xce * hse + xse * hce)
        zco = (2.0 / n_fft) * (xco * hco - xso * hso)
        zso = (2.0 / n_fft) * (xco * hso + xso * hco)
        y_mid = jnp.sum(sgn_j * (zce + zso), axis=0, keepdims=True)
        sym = mm(ce_ref, zce.astype(BF16)) + mm(sot_ref, zso.astype(BF16))
        asym = mm(cot_ref, zco.astype(BF16)) + mm(se_ref, zse.astype(BF16))
        rolled = pltpu.roll(reverse_rows(sym - asym), 1, 0)
        y = jnp.concatenate([sym + asym, jnp.where(first, y_mid, rolled)], axis=0)
        conv_to(g_s, zg_ref, cwg_ref, cbg_ref)
        v_s[...] = g_s[...] * (y + sign * nyq + v * skip_ref[o:o + 1, :])

    o_ref[...] = v_s[...].astype(BF16)


def _hyena(z, conv_w9, conv_b, a, w_fout4, deltas, tcol, skip, tables, bsz, seq):
    ct = MXU
    n_ct = HYENA_WIDTH // ct
    hy = Z_HY // ct
    half = seq // 2
    zspec = lambda off: pl.BlockSpec((seq, ct), lambda j, b: (b, hy + off * n_ct + j))
    wspec = lambda off: pl.BlockSpec((9, ct), lambda j, b: (0, off * n_ct + j))
    bspec = lambda off: pl.BlockSpec((1, ct), lambda j, b: (0, off * n_ct + j))
    fh = a.shape[1]
    return pl.pallas_call(
        functools.partial(_hyena_body, seq=seq),
        grid=(n_ct, bsz),
        in_specs=[zspec(0), zspec(1), zspec(2), wspec(0), wspec(1), wspec(2), bspec(0), bspec(1), bspec(2),
                  _const_spec((seq, fh)),
                  pl.BlockSpec((4, fh, ct), lambda j, b: (0, 0, j)),
                  pl.BlockSpec((1, ct), lambda j, b: (0, j)),
                  _const_spec((seq, 1)),
                  pl.BlockSpec((2, ct), lambda j, b: (0, j))] + [_const_spec((half, half))] * len(tables),
        out_specs=pl.BlockSpec((seq, ct), lambda j, b: (b, j)),
        out_shape=jax.ShapeDtypeStruct((bsz * seq, HYENA_WIDTH), BF16),
        scratch_shapes=[pltpu.VMEM((2, half, ct), F32)] * 4 + [pltpu.VMEM((2, 1, ct), F32),
                                                               pltpu.VMEM((seq, ct), F32), pltpu.VMEM((seq, ct), F32)],
        compiler_params=_cparams(("arbitrary", "arbitrary"), 60),
        name="hyena",
    )(z, z, z, conv_w9, conv_w9, conv_w9, conv_b, conv_b, conv_b, a, w_fout4, deltas, tcol, skip, *tables)


def _mix_body(oh_ref, hh_ref, ga_ref, gh_ref, wa_ref, wh_ref, m_ref):
    y_a = jnp.dot(oh_ref[...], wa_ref[...], preferred_element_type=F32)
    y_h = jnp.dot(hh_ref[...], wh_ref[...], preferred_element_type=F32)
    m_ref[...] = (ga_ref[...].astype(F32) * y_a + gh_ref[...].astype(F32) * y_h).astype(BF16)


def _mix(oh, hh, sg, w_a, w_h):
    t = oh.shape[0]
    dm = w_a.shape[1]
    tm = 512
    row = lambda i: (i, 0)
    return pl.pallas_call(
        _mix_body,
        grid=(t // tm,),
        in_specs=[pl.BlockSpec((tm, MLSTM_WIDTH), row), pl.BlockSpec((tm, HYENA_WIDTH), row),
                  pl.BlockSpec((tm, dm), row), pl.BlockSpec((tm, dm), lambda i: (i, 1)),
                  _const_spec((MLSTM_WIDTH, dm)), _const_spec((HYENA_WIDTH, dm))],
        out_specs=pl.BlockSpec((tm, dm), row),
        out_shape=jax.ShapeDtypeStruct((t, dm), BF16),
        compiler_params=_cparams(("parallel",), 48),
        name="mix",
    )(oh, hh, sg, sg, w_a, w_h)


def _merge_body(m_ref, x_ref, lng_ref, lnb_ref, g1_ref, l1g_ref, l1b_ref, sc2_ref, sh2_ref, wo_ref, wr_ref, br_ref,
                x1_ref, lg_ref, mo_s, *, tm):
    mo_s[...] = jnp.dot(m_ref[...], wo_ref[...], preferred_element_type=F32)
    wr = wr_ref[...]
    wr_hi = wr.astype(BF16)
    wr_lo = (wr - wr_hi.astype(F32)).astype(BF16)

    def rows(r, carry):
        sl = pl.ds(pl.multiple_of(r * 128, 128), 128)
        x0 = _layer_norm(x_ref[sl, :], lng_ref[...], lnb_ref[...])
        x1 = _layer_norm(DEEPNORM_ALPHA * x0 + g1_ref[0] * mo_s[sl, :], l1g_ref[...], l1b_ref[...])
        x1_ref[sl, :] = x1
        tok = x1 * (1.0 + sc2_ref[0]) + sh2_ref[0]
        t_hi = tok.astype(BF16)
        t_lo = (tok - t_hi.astype(F32)).astype(BF16)
        lg_ref[sl, :] = (jnp.dot(t_hi, wr_hi, preferred_element_type=F32)
                         + jnp.dot(t_lo, wr_hi, preferred_element_type=F32)
                         + jnp.dot(t_hi, wr_lo, preferred_element_type=F32) + br_ref[...])
        return carry
    lax.fori_loop(0, tm // 128, rows, 0)


def _merge(mix, x2d, ln_g, ln_b, g1, ln1_g, ln1_b, sc2, sh2, w_o, w_r, b_r, seq):
    t, dm = x2d.shape
    tm = 512
    per_b = seq // tm
    row = lambda i: (i, 0)
    mod = lambda i: (i // per_b, 0, 0)
    return pl.pallas_call(
        functools.partial(_merge_body, tm=tm),
        grid=(t // tm,),
        scratch_shapes=[pltpu.VMEM((tm, dm), F32)],
        in_specs=[pl.BlockSpec((tm, dm), row), pl.BlockSpec((tm, dm), row),
                  _const_spec((1, dm)), _const_spec((1, dm)),
                  pl.BlockSpec((1, 1, dm), mod),
                  _const_spec((1, dm)), _const_spec((1, dm)),
                  pl.BlockSpec((1, 1, dm), mod), pl.BlockSpec((1, 1, dm), mod),
                  _const_spec((dm, dm)), _const_spec((dm, LANES)), _const_spec((1, LANES))],
        out_specs=[pl.BlockSpec((tm, dm), row), pl.BlockSpec((tm, LANES), row)],
        out_shape=[jax.ShapeDtypeStruct((t, dm), F32), jax.ShapeDtypeStruct((t, LANES), F32)],
        compiler_params=_cparams(("parallel",), 56),
        name="merge",
    )(mix, x2d, ln_g, ln_b, g1, ln1_g, ln1_b, sc2, sh2, w_o, w_r, b_r)


def _route_body(lg_ref, w_ref, d_ref, cnt_ref, run_s, tot_s, *, tr):
    phase = pl.program_id(0)

    @pl.when(pl.program_id(1) == 0)
    def _():
        @pl.when(phase == 1)
        def _():
            tot_s[...] = run_s[...]
        run_s[...] = jnp.zeros_like(run_s)

    lane = lax.broadcasted_iota(I32, (tr, LANES), 1)
    lane_f = lane.astype(F32)
    logit = lg_ref[...]
    hot, val = [], []
    for _ in range(TOP_K):
        mk = jnp.max(logit, axis=-1, keepdims=True)
        ik = jnp.min(jnp.where(logit == mk, lane_f, float(LANES)), axis=-1, keepdims=True)
        hk = lane_f == ik
        logit = jnp.where(hk, -jnp.inf, logit)
        hot.append(hk)
        val.append(mk)
    cnt = jnp.zeros((tr, LANES), F32)
    for hk in hot:
        cnt = cnt + jnp.where(hk, 1.0, 0.0)
    run = run_s[...] + jnp.sum(cnt, axis=0, keepdims=True)

    @pl.when(phase == 0)
    def _():
        cnt_ref[...] = run.astype(I32)

    @pl.when(phase == 1)
    def _():
        total = tot_s[...]
        padded = jnp.floor((total + (MOE_BLOCK - 1.0)) * (1.0 / MOE_BLOCK)) * MOE_BLOCK
        lane8 = lax.broadcasted_iota(I32, (8, LANES), 1)
        incl = padded
        sft = 1
        while sft < LANES:
            incl = incl + jnp.where(lane8 >= sft, pltpu.roll(incl, sft, 1), 0.0)
            sft *= 2
        pstart = (incl - padded)[0:1, :]
        lower = (lax.broadcasted_iota(I32, (tr, tr), 0) > lax.broadcasted_iota(I32, (tr, tr), 1))
        before = jnp.dot(jnp.where(lower, 1.0, 0.0).astype(BF16), cnt.astype(BF16),
                         preferred_element_type=F32) + (run_s[0:1, :] + pstart)
        ex = [jnp.exp(v - val[0]) for v in val]
        denom = ex[0] + ex[1] + ex[2] + ex[3]
        w_out = jnp.zeros((tr, LANES), F32)
        d_out = jnp.zeros((tr, LANES), I32)
        for k in range(TOP_K):
            dest = jnp.sum(jnp.where(hot[k], before, 0.0), axis=-1, keepdims=True)
            w_out = jnp.where(lane == k, ex[k] / denom, w_out)
            d_out = jnp.where(lane == k, dest.astype(I32), d_out)
        w_ref[...] = w_out
        d_ref[...] = d_out[:, :TOP_K]
        cnt_ref[...] = total.astype(I32)

    run_s[...] = run


def _route(logits):
    t = logits.shape[0]
    tr = 512
    row = lambda p, i: (i, 0)
    out_row = lambda p, i: (i * p, 0)
    return pl.pallas_call(
        functools.partial(_route_body, tr=tr),
        grid=(2, t // tr),
        in_specs=[pl.BlockSpec((tr, LANES), row)],
        out_specs=[pl.BlockSpec((tr, LANES), out_row), pl.BlockSpec((tr, TOP_K), out_row),
                   pl.BlockSpec((8, LANES), lambda p, i: (0, 0))],
        out_shape=[jax.ShapeDtypeStruct((t, LANES), F32), jax.ShapeDtypeStruct((t, TOP_K), I32),
                   jax.ShapeDtypeStruct((8, LANES), I32)],
        scratch_shapes=[pltpu.VMEM((8, LANES), F32), pltpu.VMEM((8, LANES), F32)],
        compiler_params=_cparams(("arbitrary", "arbitrary"), 32),
        name="route",
    )(logits)


def _scatter_body(cnt_ref, pstart_ref, used_ref, dest_ref, x1_ref, sc_ref, sh_ref, xb_ref, tok_ref, zero_s, sem, pad_sem,
                  *, ts, n_blocks):
    def row_copy(src, r_src, r_dst, s):
        return pltpu.make_async_copy(src.at[pl.ds(r_src, 1)], xb_ref.at[pl.ds(r_dst, 1)], s)

    def block_copy(blk):
        return pltpu.make_async_copy(zero_s, xb_ref.at[pl.ds(pl.multiple_of(blk * MOE_BLOCK, MOE_BLOCK), MOE_BLOCK)],
                                     pad_sem)

    @pl.when(pl.program_id(0) == 0)
    def _():
        zero_s[...] = jnp.zeros_like(zero_s)

        def per_expert(e, carry):
            cnt = cnt_ref[e]
            lo = pstart_ref[e] + cnt
            hi = pstart_ref[e] + (cnt + MOE_BLOCK - 1) // MOE_BLOCK * MOE_BLOCK

            def start(r, c):
                row_copy(zero_s, 0, r, pad_sem).start()
                return c

            def wait(r, c):
                row_copy(zero_s, 0, r, pad_sem).wait()
                return c
            lax.fori_loop(lo, hi, start, 0)
            lax.fori_loop(lo, hi, wait, 0)
            return carry
        lax.fori_loop(0, N_EXPERTS, per_expert, 0)

        def tail_start(blk, c):
            block_copy(blk).start()
            return c

        def tail_wait(blk, c):
            block_copy(blk).wait()
            return c
        lax.fori_loop(used_ref[0], n_blocks, tail_start, 0)
        lax.fori_loop(used_ref[0], n_blocks, tail_wait, 0)

    tok_ref[...] = x1_ref[...] * (1.0 + sc_ref[0]) + sh_ref[0]

    def row(r, carry):
        for k in range(TOP_K):
            row_copy(tok_ref, r, dest_ref[r * TOP_K + k], sem).start()
        return carry
    lax.fori_loop(0, ts, row, 0, unroll=2)
    for _ in range(TOP_K):
        pltpu.make_async_copy(tok_ref, xb_ref.at[pl.ds(0, ts)], sem).wait()


def _scatter(x1, sc2, sh2, dest_flat, counts, pstart, used, n_blocks, seq):
    t, dm = x1.shape
    ts = 256
    per_b = seq // ts
    grid_spec = pltpu.PrefetchScalarGridSpec(
        num_scalar_prefetch=3,
        grid=(t // ts,),
        in_specs=[pl.BlockSpec((ts * TOP_K,), lambda i, *_: (i,), memory_space=pltpu.SMEM),
                  pl.BlockSpec((ts, dm), lambda i, *_: (i, 0)),
                  pl.BlockSpec((1, 1, dm), lambda i, *_: (i // per_b, 0, 0)),
                  pl.BlockSpec((1, 1, dm), lambda i, *_: (i // per_b, 0, 0))],
        out_specs=pl.BlockSpec(memory_space=pl.ANY),
        scratch_shapes=[pltpu.VMEM((ts, dm), F32), pltpu.VMEM((MOE_BLOCK, dm), F32),
                        pltpu.SemaphoreType.DMA(()), pltpu.SemaphoreType.DMA(())],
    )
    return pl.pallas_call(
        functools.partial(_scatter_body, ts=ts, n_blocks=n_blocks),
        grid_spec=grid_spec,
        out_shape=jax.ShapeDtypeStruct((n_blocks * MOE_BLOCK, dm), F32),
        compiler_params=_cparams(("arbitrary",), 32),
        name="scatter",
    )(counts, pstart, used, dest_flat, x1, sc2, sh2)


def _ffn1_body(e_ref, c_ref, blk_ref, oblk_ref, oc_ref, first_ref, rows_ref, n_ref,
               x_ref, wg_ref, wu_ref, bg_ref, bu_ref, a_ref, wg_s, wu_s):
    s = pl.program_id(0)

    @pl.when(first_ref[s] == 1)
    def _():
        wg_s[...] = wg_ref[0].astype(BF16)
        wu_s[...] = wu_ref[0].astype(BF16)

    for m in range(ROW_STEP, MOE_BLOCK + 1, ROW_STEP):
        @pl.when(jnp.logical_and(s < n_ref[0], rows_ref[s] == m))
        def _(m=m):
            x = x_ref[0:m, :].astype(BF16)
            g = jnp.minimum(jnp.dot(x, wg_s[...], preferred_element_type=F32) + bg_ref[0], SWIGLU_LIMIT)
            u = jnp.clip(jnp.dot(x, wu_s[...], preferred_element_type=F32) + bu_ref[0], -SWIGLU_LIMIT, SWIGLU_LIMIT)
            a_ref[0:m, :] = (g * _sigmoid(SWIGLU_ALPHA * g) * (u + 1.0)).astype(BF16)
            if m < MOE_BLOCK:
                a_ref[m:, :] = jnp.zeros((MOE_BLOCK - m, a_ref.shape[1]), BF16)

    @pl.when(s >= n_ref[0])
    def _():
        a_ref[...] = jnp.zeros_like(a_ref)


def _ffn2_body(e_ref, c_ref, blk_ref, oblk_ref, oc_ref, first_ref, rows_ref, n_ref, a_ref, wd_ref, bd_ref, y_ref, wd_s):
    s = pl.program_id(0)

    @pl.when(first_ref[s] == 1)
    def _():
        wd_s[...] = wd_ref[0].astype(BF16)

    for m in range(ROW_STEP, MOE_BLOCK + 1, ROW_STEP):
        @pl.when(jnp.logical_and(s < n_ref[0], rows_ref[s] == m))
        def _(m=m):
            y_ref[0:m, :] = jnp.dot(a_ref[0:m, :], wd_s[...], preferred_element_type=F32) + bd_ref[0]
            if m < MOE_BLOCK:
                y_ref[m:, :] = jnp.zeros((MOE_BLOCK - m, y_ref.shape[1]), F32)

    @pl.when(s >= n_ref[0])
    def _():
        y_ref[...] = jnp.zeros_like(y_ref)


def _ffn1(plan, xb, w_gate, w_up, b_gate, b_up, n_steps):
    rows, dm = xb.shape
    de = w_gate.shape[2]
    wspec = pl.BlockSpec((1, dm, FF_CHUNK), lambda s, e, c, *_: (e[s], 0, c[s]))
    bspec = pl.BlockSpec((1, 1, FF_CHUNK), lambda s, e, c, *_: (e[s], 0, c[s]))
    grid_spec = pltpu.PrefetchScalarGridSpec(
        num_scalar_prefetch=8,
        grid=(n_steps,),
        in_specs=[pl.BlockSpec((MOE_BLOCK, dm), lambda s, e, c, blk, *_: (blk[s], 0)), wspec, wspec, bspec, bspec],
        out_specs=pl.BlockSpec((MOE_BLOCK, FF_CHUNK), lambda s, e, c, blk, oblk, oc, *_: (oblk[s], oc[s])),
        scratch_shapes=[pltpu.VMEM((dm, FF_CHUNK), BF16), pltpu.VMEM((dm, FF_CHUNK), BF16)],
    )
    return pl.pallas_call(
        _ffn1_body, grid_spec=grid_spec,
        out_shape=jax.ShapeDtypeStruct((rows, de), BF16),
        compiler_params=_cparams(("arbitrary",), 60),
        name="ffn1",
    )(*plan, xb, w_gate, w_up, b_gate, b_up)


def _ffn2(plan, act, w_down, b_down, n_steps, chunk):
    rows, de = act.shape
    dm = w_down.shape[2]
    grid_spec = pltpu.PrefetchScalarGridSpec(
        num_scalar_prefetch=8,
        grid=(n_steps,),
        in_specs=[pl.BlockSpec((MOE_BLOCK, de), lambda s, e, c, blk, *_: (blk[s], 0)),
                  pl.BlockSpec((1, de, chunk), lambda s, e, c, *_: (e[s], 0, c[s])),
                  pl.BlockSpec((1, 1, chunk), lambda s, e, c, *_: (e[s], 0, c[s]))],
        out_specs=pl.BlockSpec((MOE_BLOCK, chunk), lambda s, e, c, blk, oblk, oc, *_: (oblk[s], oc[s])),
        scratch_shapes=[pltpu.VMEM((de, chunk), BF16)],
    )
    return pl.pallas_call(
        _ffn2_body, grid_spec=grid_spec,
        out_shape=jax.ShapeDtypeStruct((rows, dm), F32),
        compiler_params=_cparams(("arbitrary",), 60),
        name="ffn2",
    )(*plan, act, w_down, b_down)


def _combine_body(dcur_ref, dnxt_ref, w_ref, x1_ref, g2_ref, lg_ref, lb_ref, yb_ref, o_ref, buf, sem, *, tc, n_tiles):
    i = pl.program_id(0)

    def issue(d_ref, slot):
        def row(r, carry):
            for k in range(TOP_K):
                pltpu.make_async_copy(yb_ref.at[pl.ds(d_ref[r * TOP_K + k], 1)],
                                      buf.at[slot, k, pl.ds(r, 1)], sem.at[slot]).start()
            return carry
        lax.fori_loop(0, tc, row, 0, unroll=2)

    @pl.when(i == 0)
    def _():
        issue(dcur_ref, 0)

    @pl.when(i + 1 < n_tiles)
    def _():
        issue(dnxt_ref, (i + 1) % 2)

    slot = i % 2
    for k in range(TOP_K):
        pltpu.make_async_copy(yb_ref.at[pl.ds(0, tc)], buf.at[slot, k], sem.at[slot]).wait()
    w = w_ref[...]
    y = w[:, 0:1] * buf[slot, 0]
    for k in range(1, TOP_K):
        y = y + w[:, k:k + 1] * buf[slot, k]
    o_ref[...] = _layer_norm(DEEPNORM_ALPHA * x1_ref[...] + g2_ref[0] * y, lg_ref[...], lb_ref[...])


def _combine(dest_flat, w4, x1, g2, ln_g, ln_b, yb, seq):
    t, dm = x1.shape
    tc = 256
    n_tiles = t // tc
    per_b = seq // tc
    row = lambda i: (i, 0)
    return pl.pallas_call(
        functools.partial(_combine_body, tc=tc, n_tiles=n_tiles),
        grid=(n_tiles,),
        in_specs=[pl.BlockSpec((tc * TOP_K,), lambda i: (i,), memory_space=pltpu.SMEM),
                  pl.BlockSpec((tc * TOP_K,), lambda i: (jnp.minimum(i + 1, n_tiles - 1),), memory_space=pltpu.SMEM),
                  pl.BlockSpec((tc, LANES), row),
                  pl.BlockSpec((tc, dm), row),
                  pl.BlockSpec((1, 1, dm), lambda i: (i // per_b, 0, 0)),
                  _const_spec((1, dm)), _const_spec((1, dm)),
                  pl.BlockSpec(memory_space=pl.ANY)],
        out_specs=pl.BlockSpec((tc, dm), row),
        out_shape=jax.ShapeDtypeStruct((t, dm), F32),
        scratch_shapes=[pltpu.VMEM((2, TOP_K, tc, dm), F32), pltpu.SemaphoreType.DMA((2,))],
        compiler_params=_cparams(("arbitrary",), 40),
        name="combine",
    )(dest_flat, dest_flat, w4, x1, g2, ln_g, ln_b, yb)


def _dft_tables(seq):
    n = 2 * seq
    j = np.arange(seq // 2, dtype=np.int64)
    tables = []
    for k in (2 * j, 2 * j + 1):
        ang = (2.0 * np.pi / n) * ((k[:, None] * j[None, :]) % n).astype(np.float64)
        tables += [np.cos(ang), np.sin(ang)]
    tables += [tables[2].T, tables[3].T]
    return tuple(jnp.asarray(m, F32).astype(BF16) for m in tables)


def _filter_features(seq):
    t = jnp.linspace(0.0, 1.0, seq, dtype=F32)[:, None]
    bands = (HYENA_EMB - 1) // 2
    f = jnp.linspace(1e-4, bands - 1, bands, dtype=F32)[None, :]
    ang = 2.0 * math.pi * jnp.arange(seq, dtype=F32)[:, None] * f / seq
    feats = jnp.concatenate([t, jnp.cos(ang), -jnp.sin(ang)], axis=-1)
    max_decay = math.log(HYENA_DECAY_TARGET) / HYENA_SHORT_DECAY_PCT
    min_decay = math.log(HYENA_DECAY_TARGET) / HYENA_LONG_DECAY_PCT
    deltas = jnp.abs(jnp.linspace(min_decay, max_decay, HYENA_WIDTH, dtype=F32))[None, :]
    return t, feats, deltas


def _moe_plan(counts, n_blocks, n_chunks):
    ids = jnp.arange(N_EXPERTS, dtype=I32)
    nblk = (counts + MOE_BLOCK - 1) // MOE_BLOCK
    blk_end = jnp.cumsum(nblk)
    blk_start = blk_end - nblk
    used = blk_end[-1]
    steps = n_chunks * nblk
    step_end = jnp.cumsum(steps)
    n_used = step_end[-1]
    s_all = jnp.arange(n_chunks * n_blocks, dtype=I32)
    s = jnp.minimum(s_all, n_used - 1)
    e_s = jnp.minimum(jnp.sum((s[:, None] >= step_end[None, :]).astype(I32), axis=1), N_EXPERTS - 1)
    onehot = e_s[:, None] == ids[None, :]
    pick = lambda table: jnp.sum(jnp.where(onehot, table[None, :], 0), axis=1)
    loc = s - pick(step_end - steps)
    nb = jnp.maximum(pick(nblk), 1)
    c_s = loc // nb
    r_s = loc % nb
    blk = pick(blk_start) + r_s
    tail = s_all >= n_used
    j = jnp.maximum(s_all - n_used, 0)
    n_tail = jnp.maximum(n_blocks - used, 1)
    oblk = jnp.where(tail, used + j % n_tail, blk)
    oc = jnp.where(tail, j // n_tail, c_s)
    first = jnp.logical_and(r_s == 0, jnp.logical_not(tail))
    rows = jnp.clip(pick(counts) - r_s * MOE_BLOCK, 1, MOE_BLOCK)
    rows = (rows + ROW_STEP - 1) // ROW_STEP * ROW_STEP
    as_i32 = lambda v: v.astype(I32)
    plan = tuple(map(as_i32, (e_s, c_s, blk, oblk, oc, first, rows, n_used.reshape(1))))
    return as_i32(blk_start * MOE_BLOCK), as_i32(used.reshape(1)), plan


def kernel(x, c, ctx, c_ctx, ln_in_g, ln_in_b, w_mod, b_mod, w_in, b_in, mlstm_conv_w, mlstm_conv_b,
           w_qh, w_kh, hyena_conv_w, hyena_conv_b, filt_w1, filt_b1, filt_wh, filt_bh, filt_freq, filt_wout,
           hyena_skip, w_proj_a, w_proj_h, w_out, ln1_g, ln1_b, w_router, b_router, w_gate, b_gate,
           w_up, b_up, w_down, b_down, ln2_g, ln2_b):
    bsz, seq, dm = x.shape
    ctx_len = ctx.shape[1]
    t = bsz * seq
    assert w_mod.shape[0] == DEPTH and dm == D_MODEL and ctx_len == CHUNK and bsz + 1 <= 16
    row = lambda v: v.reshape(1, -1)

    cond = jnp.concatenate([c, c_ctx[None], jnp.zeros((16 - bsz - 1, dm), F32)], axis=0)
    mod = _mod(cond, w_mod[0], row(b_mod[0]))
    sh1, sc1, g1, sh2, sc2, g2 = [m[:, None, :] for m in jnp.split(mod, 6, axis=-1)]

    w_main = jnp.concatenate([w_in[0][:, :IN_GATES], w_in[0][:, IN_O:]], axis=1).astype(BF16)
    b_main = row(jnp.concatenate([b_in[0][:IN_GATES], b_in[0][IN_O:]]))
    w_g = jnp.pad(w_in[0][:, IN_GATES:IN_O], ((0, 0), (0, LANES - 4 * HEADS))).astype(BF16)
    b_g = row(jnp.pad(b_in[0][IN_GATES:IN_O], (0, LANES - 4 * HEADS)))
    lng, lnb = row(ln_in_g), row(ln_in_b)
    x2d = x.reshape(t, dm)
    z, gates, sg = _in_proj(x2d, lng, lnb, sc1[:bsz], sh1[:bsz], w_main, b_main, w_g, b_g, seq, Z_BG)
    zc, gates_c = _in_proj(ctx.reshape(bsz * ctx_len, dm), lng, lnb, sc1[bsz:bsz + 1], sh1[bsz:bsz + 1],
                           w_main[:, :IN_GATES], b_main[:, :IN_GATES], w_g, b_g, bsz * ctx_len, IN_GATES)

    g_all = jnp.concatenate([gates_c[:, :4 * HEADS].reshape(bsz, ctx_len, 4, HEADS),
                             gates[:, :4 * HEADS].reshape(bsz, seq, 4, HEADS)], axis=1)
    gates_t = g_all.transpose(0, 3, 2, 1)

    oh = _mlstm(z, zc, gates_t, mlstm_conv_w[0].reshape(9, MLSTM_WIDTH), row(mlstm_conv_b[0]),
                w_qh[0], w_kh[0].transpose(0, 2, 1), bsz, seq, ctx_len)

    tcol, feats, deltas = _filter_features(seq)
    feats = jnp.pad(feats, ((0, 0), (0, LANES - HYENA_EMB)))
    w1 = jnp.pad(filt_w1[0], ((0, LANES - HYENA_EMB), (0, 0)))
    a = _filt(feats, w1, row(filt_b1[0]), filt_wh[0], filt_bh[0], filt_freq[0])
    fh = a.shape[1]
    w_fout4 = filt_wout[0].reshape(fh, 4, HYENA_WIDTH).transpose(1, 0, 2)
    hh = _hyena(z, hyena_conv_w[0].reshape(9, 3 * HYENA_WIDTH), row(hyena_conv_b[0]), a, w_fout4, deltas, tcol,
                hyena_skip[0], _dft_tables(seq), bsz, seq)

    w_r = jnp.pad(w_router[0], ((0, 0), (0, LANES - N_EXPERTS)))
    b_r = row(jnp.pad(b_router[0], (0, LANES - N_EXPERTS), constant_values=-1e30))
    mix = _mix(oh, hh, sg, w_proj_a[0].astype(BF16), w_proj_h[0].astype(BF16))
    x1, logits = _merge(mix, x2d, lng, lnb, g1[:bsz], row(ln1_g[0]), row(ln1_b[0]), sc2[:bsz], sh2[:bsz],
                        w_out[0].astype(BF16), w_r, b_r, seq)

    w4, dest, counts = _route(logits)
    n_blocks = -(-(t * TOP_K + N_EXPERTS * (MOE_BLOCK - 1)) // MOE_BLOCK)
    n_chunks = dm // FF_CHUNK
    counts = counts[0, :N_EXPERTS]
    pstart, used, plan = _moe_plan(counts, n_blocks, n_chunks)
    dest_flat = dest.reshape(t * TOP_K)
    xb = _scatter(x1, sc2[:bsz], sh2[:bsz], dest_flat, counts, pstart, used, n_blocks, seq)
    act = _ffn1(plan, xb, w_gate[0], w_up[0], b_gate[0][:, None, :], b_up[0][:, None, :], n_chunks * n_blocks)
    _, _, plan_down = _moe_plan(counts, n_blocks, 1)
    yb = _ffn2(plan_down, act, w_down[0], b_down[0][:, None, :], n_blocks, dm)
    out = _combine(dest_flat, w4, x1, g2[:bsz], row(ln2_g[0]), row(ln2_b[0]), yb, seq)
    return out.reshape(bsz, seq, dm)
```

```python
import functools
import math

import numpy as np
import jax
import jax.numpy as jnp
from jax import lax
from jax.experimental import pallas as pl
from jax.experimental.pallas import tpu as pltpu

F32 = jnp.float32
BF16 = jnp.bfloat16
I32 = jnp.int32
HIGHEST = lax.Precision.HIGHEST

D_MODEL = 2048
GRID_W = 64
HEADS = 4
HEAD_DIM = 256
MLSTM_WIDTH = HEADS * HEAD_DIM
HYENA_WIDTH = D_MODEL // 2
HYENA_EMB = 33
HYENA_DECAY_TARGET = 1e-2
HYENA_SHORT_DECAY_PCT = 0.3
HYENA_LONG_DECAY_PCT = 1.5
N_EXPERTS = 32
TOP_K = 4
SWIGLU_LIMIT = 7.0
SWIGLU_ALPHA = 1.702
LN_EPS = 1e-5
DEPTH = 1
DEEPNORM_ALPHA = (2.0 * DEPTH) ** 0.25

IN_V = MLSTM_WIDTH
IN_GATES = 2 * MLSTM_WIDTH
IN_O = IN_GATES + 4 * HEADS
Z_QK, Z_V, Z_O, Z_HY, Z_BG = 0, 1024, 2048, 3072, 6144

LANES = 128
MXU = 256
CHUNK = 256
MOE_BLOCK = 512
FF_CHUNK = 1024
MIB = 1024 * 1024


def _cparams(semantics, vmem_mib):
    return pltpu.CompilerParams(dimension_semantics=semantics, vmem_limit_bytes=vmem_mib * MIB)


def _const_spec(shape):
    nd = len(shape)
    return pl.BlockSpec(shape, lambda *_: (0,) * nd, pipeline_mode=pl.Buffered(1))


def _layer_norm(x, g, b):
    mu = jnp.mean(x, axis=-1, keepdims=True)
    xc = x - mu
    var = jnp.mean(xc * xc, axis=-1, keepdims=True)
    return xc * lax.rsqrt(var + LN_EPS) * g + b


def _sigmoid(x):
    return 1.0 / (1.0 + jnp.exp(-x))


def _silu(x):
    return x * _sigmoid(x)


def _log_sigmoid(x):
    return jnp.minimum(x, 0.0) - jnp.log(1.0 + jnp.exp(-jnp.abs(x)))


def _mod_body(c_ref, w_ref, b_ref, o_ref):
    s = _silu(c_ref[...])
    o_ref[...] = jnp.dot(s.astype(BF16), w_ref[...].astype(BF16), preferred_element_type=F32) + b_ref[...]


def _mod(cond, w, b):
    rows, dm = cond.shape
    n = w.shape[1]
    tn = 1024
    return pl.pallas_call(
        _mod_body,
        grid=(n // tn,),
        in_specs=[pl.BlockSpec((rows, dm), lambda j: (0, 0)),
                  pl.BlockSpec((dm, tn), lambda j: (0, j)),
                  pl.BlockSpec((1, tn), lambda j: (0, j))],
        out_specs=pl.BlockSpec((rows, tn), lambda j: (0, j)),
        out_shape=jax.ShapeDtypeStruct((rows, n), F32),
        compiler_params=_cparams(("arbitrary",), 40),
        name="mod",
    )(cond, w, b)


def _in_proj_body(x_ref, lng_ref, lnb_ref, sc_ref, sh_ref, w_ref, b_ref, wg_ref, bg_ref,
                  z_ref, g_ref, *rest, tm, n_plain):
    hx_s = rest[-1]

    @pl.when(pl.program_id(1) == 0)
    def _():
        def rows(r, carry):
            sl = pl.ds(pl.multiple_of(r * 128, 128), 128)
            xn = _layer_norm(x_ref[sl, :], lng_ref[...], lnb_ref[...])
            hx_s[sl, :] = (xn * (1.0 + sc_ref[0]) + sh_ref[0]).astype(BF16)
            return carry
        lax.fori_loop(0, tm // 128, rows, 0)
        g_ref[...] = jnp.dot(hx_s[...], wg_ref[...], preferred_element_type=F32) + bg_ref[...]

    acc = jnp.dot(hx_s[...], w_ref[...], preferred_element_type=F32) + b_ref[...]
    if len(rest) == 1:
        z_ref[...] = acc
    else:
        @pl.when(pl.program_id(1) < n_plain)
        def _():
            z_ref[...] = acc

        @pl.when(pl.program_id(1) >= n_plain)
        def _():
            rest[0][...] = _sigmoid(acc).astype(BF16)


def _in_proj(x2d, ln_g, ln_b, scale, shift, w, b, w_gates, b_gates, rows_per_mod, plain_cols):
    t, dm = x2d.shape
    n = w.shape[1]
    tm, tn = min(1024, t), 1024
    n_plain = plain_cols // tn
    out_specs = [pl.BlockSpec((tm, tn), lambda i, j: (i, jnp.minimum(j, n_plain - 1))),
                 pl.BlockSpec((tm, LANES), lambda i, j: (i, 0))]
    out_shape = [jax.ShapeDtypeStruct((t, plain_cols), F32), jax.ShapeDtypeStruct((t, LANES), F32)]
    if n > plain_cols:
        out_specs.append(pl.BlockSpec((tm, tn), lambda i, j: (i, jnp.maximum(j - n_plain, 0))))
        out_shape.append(jax.ShapeDtypeStruct((t, n - plain_cols), BF16))
    return pl.pallas_call(
        functools.partial(_in_proj_body, tm=tm, n_plain=n_plain),
        grid=(t // tm, n // tn),
        in_specs=[pl.BlockSpec((tm, dm), lambda i, j: (i, 0)),
                  pl.BlockSpec((1, dm), lambda i, j: (0, 0)),
                  pl.BlockSpec((1, dm), lambda i, j: (0, 0)),
                  pl.BlockSpec((1, 1, dm), lambda i, j: (i * tm // rows_per_mod, 0, 0)),
                  pl.BlockSpec((1, 1, dm), lambda i, j: (i * tm // rows_per_mod, 0, 0)),
                  pl.BlockSpec((dm, tn), lambda i, j: (0, j)),
                  pl.BlockSpec((1, tn), lambda i, j: (0, j)),
                  pl.BlockSpec((dm, LANES), lambda i, j: (0, 0)),
                  pl.BlockSpec((1, LANES), lambda i, j: (0, 0))],
        out_specs=out_specs,
        out_shape=out_shape,
        scratch_shapes=[pltpu.VMEM((tm, dm), BF16)],
        compiler_params=_cparams(("parallel", "arbitrary"), 48),
        name="in_proj",
    )(x2d, ln_g, ln_b, scale, shift, w, b, w_gates, b_gates)


def _dwconv(u, w9, bias, width, single_row):
    length, ch = u.shape
    col = lax.broadcasted_iota(I32, (length, ch), 0) % width
    if not single_row:
        zpad = jnp.zeros((width, ch), F32)
        up = jnp.concatenate([zpad, u[:length - width]], axis=0)
        dn = jnp.concatenate([u[width:], zpad], axis=0)
    out = None
    for dc in (-1, 0, 1):
        a = u * w9[4 + dc:5 + dc]
        if not single_row:
            a = a + up * w9[1 + dc:2 + dc] + dn * w9[7 + dc:8 + dc]
        if dc == -1:
            a = jnp.where(col == 0, 0.0, pltpu.roll(a, 1, 0))
        elif dc == 1:
            a = jnp.where(col == width - 1, 0.0, pltpu.roll(a, length - 1, 0))
        out = a if out is None else out + a
    return out + bias


def _mlstm_body(zqk_ref, zv_ref, zo_ref, cqk_ref, cv_ref, gt_ref, cw_ref, cb_ref, wq_ref, wkt_ref,
                o_ref, q_s, k_s, kt_s, v_s, hf_s, hb_s, r_s, c_s, ct_s, *, ctx_len, seq):
    n_chunks = (ctx_len + seq) // CHUNK
    total = ctx_len + seq
    cw = cw_ref[...]
    cb = cb_ref[...]
    wq = wq_ref[0].astype(BF16)
    wkt = wkt_ref[0].astype(BF16)
    nt = (((1,), (1,)), ((), ()))
    scale = HEAD_DIM ** -0.5

    def project(u, off, n):
        ub = u.astype(BF16)
        q_s[off:off + n, :] = jnp.dot(ub, wq, preferred_element_type=F32).astype(BF16)
        k_s[off:off + n, :] = (lax.dot_general(ub, wkt, nt, preferred_element_type=F32) * scale).astype(BF16)
        kt_s[:, off:off + n] = (lax.dot_general(wkt, ub, nt, preferred_element_type=F32) * scale).astype(BF16)

    project(_silu(_dwconv(cqk_ref[...], cw, cb, ctx_len, True)), 0, ctx_len)
    project(_silu(_dwconv(zqk_ref[...], cw, cb, GRID_W, False)), ctx_len, seq)
    v_s[0:ctx_len, :] = cv_ref[...]
    v_s[ctx_len:total, :] = zv_ref[...]

    gt = gt_ref[0, 0]
    lf = _log_sigmoid(gt)
    pos = lax.broadcasted_iota(I32, (4, total), 1) % CHUNK
    pre, suf = lf, lf
    s = 1
    while s < CHUNK:
        pre = pre + jnp.where(pos >= s, pltpu.roll(pre, s, 1), 0.0)
        suf = suf + jnp.where(pos < CHUNK - s, pltpu.roll(suf, total - s, 1), 0.0)
        s *= 2
    b_f, li_f, b_b, li_b = pre[1:2], gt[0:1], suf[3:4], gt[2:3]
    pm_f, pm_b = li_f - b_f, li_b - b_b
    pos1 = pos[0:1]
    s = 1
    while s < CHUNK:
        pm_f = jnp.maximum(pm_f, jnp.where(pos1 >= s, pltpu.roll(pm_f, s, 1), -jnp.inf))
        pm_b = jnp.maximum(pm_b, jnp.where(pos1 < CHUNK - s, pltpu.roll(pm_b, total - s, 1), -jnp.inf))
        s *= 2
    r_s[...] = jnp.concatenate([b_f, li_f, pm_f, b_b, li_b, pm_b, jnp.zeros((2, total), F32)], axis=0)
    zfill = jnp.zeros((CHUNK - 8, CHUNK), F32)
    for c in range(n_chunks):
        blk = jnp.concatenate([r_s[:, c * CHUNK:(c + 1) * CHUNK], zfill], axis=0).T
        c_s[c * CHUNK:(c + 1) * CHUNK, :] = blk[:, :LANES]

    ct_s[...] = jnp.zeros_like(ct_s)
    row_i = lax.broadcasted_iota(I32, (CHUNK, CHUNK), 0)
    col_i = lax.broadcasted_iota(I32, (CHUNK, CHUNK), 1)
    masks = (row_i >= col_i, row_i <= col_i)

    def chunk_step(c, d, n_vec, m):
        off = c * CHUNK if isinstance(c, int) else pl.multiple_of(c * CHUNK, CHUNK)
        rows = r_s[:, pl.ds(off, CHUNK)]
        cols = c_s[pl.ds(off, CHUNK), :]
        b_row, li_row, pm_row = rows[3 * d:3 * d + 1], rows[3 * d + 1:3 * d + 2], rows[3 * d + 2:3 * d + 3]
        b_col, li_col, pm_col = cols[:, 3 * d:3 * d + 1], cols[:, 3 * d + 1:3 * d + 2], cols[:, 3 * d + 2:3 * d + 3]
        last = slice(CHUNK - 1, CHUNK) if d == 0 else slice(0, 1)
        b_end, pm_end = b_row[:, last], pm_row[:, last]
        qc = q_s[pl.ds(off, CHUNK), :]
        kc = k_s[pl.ds(off, CHUNK), :]
        ktc = kt_s[:, pl.ds(off, CHUNK)]
        vc = v_s[pl.ds(off, CHUNK), :]
        ct = ct_s[d]

        inter = b_col + m
        m_t = jnp.maximum(inter, b_col + pm_col)
        wts = jnp.exp(jnp.where(masks[d], b_col - b_row + li_row, -jnp.inf) - m_t)
        s_inter = jnp.exp(inter - m_t)
        scores = jnp.dot(qc, ktc, preferred_element_type=F32) * wts
        num = (s_inter * jnp.dot(qc, ct.astype(BF16), preferred_element_type=F32)
               + jnp.dot(scores.astype(BF16), vc.astype(BF16), preferred_element_type=F32))
        den = (s_inter * jnp.sum(qc.astype(F32) * n_vec, axis=-1, keepdims=True)
               + jnp.sum(scores, axis=-1, keepdims=True))
        h = num / jnp.maximum(jnp.abs(den), jnp.exp(-m_t))

        m_new = jnp.maximum(b_end + m, b_end + pm_end)
        decay = jnp.exp(b_end + m - m_new)
        w = jnp.exp(b_end - b_col + li_col - m_new)
        ct_s[d] = decay * ct + jnp.dot(ktc, (vc * w).astype(BF16), preferred_element_type=F32)
        n_new = decay * n_vec + jnp.sum(kc.astype(F32) * w, axis=0, keepdims=True)
        return h, n_new, m_new

    n0 = jnp.zeros((1, HEAD_DIM), F32)
    m0 = jnp.zeros((1, 1), F32)
    _, nf, mf = chunk_step(0, 0, n0, m0)
    _, nb, mb = chunk_step(0, 1, n0, m0)

    def body(i, carry):
        nf, mf, nb, mb = carry
        hf, nf, mf = chunk_step(i, 0, nf, mf)
        hf_s[pl.ds(pl.multiple_of(i * CHUNK - ctx_len, CHUNK), CHUNK), :] = hf
        j = n_chunks - i
        hb, nb, mb = chunk_step(j, 1, nb, mb)
        hb_s[pl.ds(pl.multiple_of(j * CHUNK - ctx_len, CHUNK), CHUNK), :] = hb
        return nf, mf, nb, mb

    lax.fori_loop(1, n_chunks, body, (nf, mf, nb, mb))
    o_ref[...] = (_sigmoid(zo_ref[...]) * (hf_s[...] + hb_s[...])).astype(BF16)


def _mlstm(z, zc, gates_t, conv_w9, conv_b, w_qh, w_kh_t, bsz, seq, ctx_len):
    total = ctx_len + seq
    hd = HEAD_DIM
    qk_blk, v_blk, o_blk = Z_QK // hd, Z_V // hd, Z_O // hd
    return pl.pallas_call(
        functools.partial(_mlstm_body, ctx_len=ctx_len, seq=seq),
        grid=(bsz, HEADS),
        in_specs=[pl.BlockSpec((seq, hd), lambda b, h: (b, qk_blk + h)),
                  pl.BlockSpec((seq, hd), lambda b, h: (b, v_blk + h)),
                  pl.BlockSpec((seq, hd), lambda b, h: (b, o_blk + h)),
                  pl.BlockSpec((ctx_len, hd), lambda b, h: (b, qk_blk + h)),
                  pl.BlockSpec((ctx_len, hd), lambda b, h: (b, v_blk + h)),
                  pl.BlockSpec((1, 1, 4, total), lambda b, h: (b, h, 0, 0)),
                  pl.BlockSpec((9, hd), lambda b, h: (0, h)),
                  pl.BlockSpec((1, hd), lambda b, h: (0, h)),
                  pl.BlockSpec((1, hd, hd), lambda b, h: (h, 0, 0)),
                  pl.BlockSpec((1, hd, hd), lambda b, h: (h, 0, 0))],
        out_specs=pl.BlockSpec((seq, hd), lambda b, h: (b, h)),
        out_shape=jax.ShapeDtypeStruct((bsz * seq, MLSTM_WIDTH), BF16),
        scratch_shapes=[pltpu.VMEM((total, hd), BF16), pltpu.VMEM((total, hd), BF16), pltpu.VMEM((hd, total), BF16),
                        pltpu.VMEM((total, hd), F32), pltpu.VMEM((seq, hd), F32), pltpu.VMEM((seq, hd), F32),
                        pltpu.VMEM((8, total), F32), pltpu.VMEM((total, LANES), F32), pltpu.VMEM((2, hd, hd), F32)],
        compiler_params=_cparams(("parallel", "arbitrary"), 48),
        name="mlstm",
    )(z, z, z, zc, zc, gates_t, conv_w9, conv_b, w_qh, w_kh_t)


def _filt_body(z_ref, w1_ref, b1_ref, wh_ref, bh_ref, fr_ref, a_ref):
    fr = fr_ref[...]
    a = jnp.sin(fr[0:1] * (jnp.dot(z_ref[...], w1_ref[...], precision=HIGHEST, preferred_element_type=F32)
                           + b1_ref[...]))
    for i in range(2):
        a = jnp.sin(fr[i + 1:i + 2] * (jnp.dot(a, wh_ref[i], precision=HIGHEST, preferred_element_type=F32)
                                       + bh_ref[i:i + 1]))
    a_ref[...] = a


def _filt(feats, w1, b1, wh, bh, freq):
    length = feats.shape[0]
    fh = w1.shape[1]
    return pl.pallas_call(
        _filt_body,
        out_shape=jax.ShapeDtypeStruct((length, fh), F32),
        name="filt",
    )(feats, w1, b1, wh, bh, freq)


def _hyena_body(zx1_ref, zx2_ref, zv_ref, cw1_ref, cw2_ref, cwv_ref, cb1_ref, cb2_ref, cbv_ref,
                a_ref, wf_ref, dl_ref, t_ref, skip_ref, ce_ref, se_ref, co_ref, so_ref, cot_ref, sot_ref, o_ref,
                hce_s, hse_s, hco_s, hso_s, hn_s, v_s, g_s, *, seq):
    n_fft = 2 * seq
    half = seq // 2
    sign = jnp.where(lax.broadcasted_iota(I32, (seq, 1), 0) % 2 == 0, 1.0, -1.0)
    jrow = lax.broadcasted_iota(I32, (half, 1), 0)
    sgn_j = jnp.where(jrow % 2 == 0, 1.0, -1.0)
    first = jrow == 0
    flip = jnp.where(lax.broadcasted_iota(I32, (MXU, MXU), 0) + lax.broadcasted_iota(I32, (MXU, MXU), 1) == MXU - 1,
                     1.0, 0.0).astype(BF16)

    def mm(m_ref, x):
        return jnp.dot(m_ref[...], x, preferred_element_type=F32)

    def reverse_rows(x):
        hi = x.astype(BF16)
        lo = (x - hi.astype(F32)).astype(BF16)
        blocks = []
        for b in range(half // MXU):
            src = slice(half - MXU * (b + 1), half - MXU * b)
            blocks.append(jnp.dot(flip, hi[src], preferred_element_type=F32)
                          + jnp.dot(flip, lo[src], preferred_element_type=F32))
        return jnp.concatenate(blocks, axis=0)

    def fold(v):
        rolled = pltpu.roll(reverse_rows(v[half:]), 1, 0)
        mid = rolled[0:1]
        vr = jnp.where(first, 0.0, rolled)
        return v[:half] + vr, v[:half] - vr, mid

    def forward(v):
        s, d, mid = fold(v)
        sb, db = s.astype(BF16), d.astype(BF16)
        return mm(ce_ref, sb) + sgn_j * mid, mm(se_ref, db), mm(co_ref, db), mm(so_ref, sb) + sgn_j * mid

    @pl.when(pl.program_id(1) == 0)
    def _():
        window = jnp.exp(-t_ref[...] * dl_ref[...])
        row0 = lax.broadcasted_iota(I32, (seq, 1), 0) == 0
        for o in range(2):
            fwd = jnp.dot(a_ref[...], wf_ref[2 * o], precision=HIGHEST, preferred_element_type=F32) * window
            bwd = jnp.dot(a_ref[...], wf_ref[2 * o + 1], precision=HIGHEST, preferred_element_type=F32) * window
            bwd = jnp.where(row0, 0.0, bwd)
            even = fwd + bwd
            hce_s[o], _, hco_s[o], _ = forward(even)
            _, hse_s[o], _, hso_s[o] = forward(fwd - bwd)
            hn_s[o] = jnp.sum(even * sign, axis=0, keepdims=True)

    def conv_to(dst_ref, z_ref, cw_ref, cb_ref):
        for lo in range(0, z_ref.shape[1], LANES):
            ls = slice(lo, lo + LANES)
            dst_ref[:, ls] = _dwconv(z_ref[:, ls], cw_ref[:, ls], cb_ref[:, ls], GRID_W, False)

    conv_to(v_s, zv_ref, cwv_ref, cbv_ref)

    for o, (zg_ref, cwg_ref, cbg_ref) in enumerate(((zx1_ref, cw1_ref, cb1_ref), (zx2_ref, cw2_ref, cb2_ref))):
        v = v_s[...]
        nyq = jnp.sum(v * sign, axis=0, keepdims=True) * hn_s[o] * (1.0 / n_fft)
        xce, xse, xco, xso = forward(v)
        hce, hse, hco, hso = hce_s[o], hse_s[o], hco_s[o], hso_s[o]
        scale_e = jnp.where(first, 1.0 / n_fft, 2.0 / n_fft)
        zce = scale_e * (xce * hce - xse * hse)
        zse = scale_e * (xce * hse + xse * hce)
        zco = (2.0 / n_fft) * (xco * hco - xso * hso)
        zso = (2.0 / n_fft) * (xco * hso + xso * hco)
        y_mid = jnp.sum(sgn_j * (zce + zso), axis=0, keepdims=True)
        sym = mm(ce_ref, zce.astype(BF16)) + mm(sot_ref, zso.astype(BF16))
        asym = mm(cot_ref, zco.astype(BF16)) + mm(se_ref, zse.astype(BF16))
        rolled = pltpu.roll(reverse_rows(sym - asym), 1, 0)
        y = jnp.concatenate([sym + asym, jnp.where(first, y_mid, rolled)], axis=0)
        conv_to(g_s, zg_ref, cwg_ref, cbg_ref)
        v_s[...] = g_s[...] * (y + sign * nyq + v * skip_ref[o:o + 1, :])

    o_ref[...] = v_s[...].astype(BF16)


def _hyena(z, conv_w9, conv_b, a, w_fout4, deltas, tcol, skip, tables, bsz, seq):
    ct = MXU
    n_ct = HYENA_WIDTH // ct
    hy = Z_HY // ct
    half = seq // 2
    zspec = lambda off: pl.BlockSpec((seq, ct), lambda j, b: (b, hy + off * n_ct + j))
    wspec = lambda off: pl.BlockSpec((9, ct), lambda j, b: (0, off * n_ct + j))
    bspec = lambda off: pl.BlockSpec((1, ct), lambda j, b: (0, off * n_ct + j))
    fh = a.shape[1]
    return pl.pallas_call(
        functools.partial(_hyena_body, seq=seq),
        grid=(n_ct, bsz),
        in_specs=[zspec(0), zspec(1), zspec(2), wspec(0), wspec(1), wspec(2), bspec(0), bspec(1), bspec(2),
                  _const_spec((seq, fh)),
                  pl.BlockSpec((4, fh, ct), lambda j, b: (0, 0, j)),
                  pl.BlockSpec((1, ct), lambda j, b: (0, j)),
                  _const_spec((seq, 1)),
                  pl.BlockSpec((2, ct), lambda j, b: (0, j))] + [_const_spec((half, half))] * len(tables),
        out_specs=pl.BlockSpec((seq, ct), lambda j, b: (b, j)),
        out_shape=jax.ShapeDtypeStruct((bsz * seq, HYENA_WIDTH), BF16),
        scratch_shapes=[pltpu.VMEM((2, half, ct), F32)] * 4 + [pltpu.VMEM((2, 1, ct), F32),
                                                               pltpu.VMEM((seq, ct), F32), pltpu.VMEM((seq, ct), F32)],
        compiler_params=_cparams(("arbitrary", "arbitrary"), 60),
        name="hyena",
    )(z, z, z, conv_w9, conv_w9, conv_w9, conv_b, conv_b, conv_b, a, w_fout4, deltas, tcol, skip, *tables)


def _mix_body(oh_ref, hh_ref, ga_ref, gh_ref, wa_ref, wh_ref, m_ref):
    y_a = jnp.dot(oh_ref[...], wa_ref[...], preferred_element_type=F32)
    y_h = jnp.dot(hh_ref[...], wh_ref[...], preferred_element_type=F32)
    m_ref[...] = (ga_ref[...].astype(F32) * y_a + gh_ref[...].astype(F32) * y_h).astype(BF16)


def _mix(oh, hh, sg, w_a, w_h):
    t = oh.shape[0]
    dm = w_a.shape[1]
    tm = 512
    row = lambda i: (i, 0)
    return pl.pallas_call(
        _mix_body,
        grid=(t // tm,),
        in_specs=[pl.BlockSpec((tm, MLSTM_WIDTH), row), pl.BlockSpec((tm, HYENA_WIDTH), row),
                  pl.BlockSpec((tm, dm), row), pl.BlockSpec((tm, dm), lambda i: (i, 1)),
                  _const_spec((MLSTM_WIDTH, dm)), _const_spec((HYENA_WIDTH, dm))],
        out_specs=pl.BlockSpec((tm, dm), row),
        out_shape=jax.ShapeDtypeStruct((t, dm), BF16),
        compiler_params=_cparams(("parallel",), 48),
        name="mix",
    )(oh, hh, sg, sg, w_a, w_h)


def _merge_body(m_ref, x_ref, lng_ref, lnb_ref, g1_ref, l1g_ref, l1b_ref, sc2_ref, sh2_ref, wo_ref, wr_ref, br_ref,
                x1_ref, lg_ref, mo_s, *, tm):
    mo_s[...] = jnp.dot(m_ref[...], wo_ref[...], preferred_element_type=F32)
    wr = wr_ref[...]
    wr_hi = wr.astype(BF16)
    wr_lo = (wr - wr_hi.astype(F32)).astype(BF16)

    def rows(r, carry):
        sl = pl.ds(pl.multiple_of(r * 128, 128), 128)
        x0 = _layer_norm(x_ref[sl, :], lng_ref[...], lnb_ref[...])
        x1 = _layer_norm(DEEPNORM_ALPHA * x0 + g1_ref[0] * mo_s[sl, :], l1g_ref[...], l1b_ref[...])
        x1_ref[sl, :] = x1
        tok = x1 * (1.0 + sc2_ref[0]) + sh2_ref[0]
        t_hi = tok.astype(BF16)
        t_lo = (tok - t_hi.astype(F32)).astype(BF16)
        lg_ref[sl, :] = (jnp.dot(t_hi, wr_hi, preferred_element_type=F32)
                         + jnp.dot(t_lo, wr_hi, preferred_element_type=F32)
                         + jnp.dot(t_hi, wr_lo, preferred_element_type=F32) + br_ref[...])
        return carry
    lax.fori_loop(0, tm // 128, rows, 0)


def _merge(mix, x2d, ln_g, ln_b, g1, ln1_g, ln1_b, sc2, sh2, w_o, w_r, b_r, seq):
    t, dm = x2d.shape
    tm = 512
    per_b = seq // tm
    row = lambda i: (i, 0)
    mod = lambda i: (i // per_b, 0, 0)
    return pl.pallas_call(
        functools.partial(_merge_body, tm=tm),
        grid=(t // tm,),
        scratch_shapes=[pltpu.VMEM((tm, dm), F32)],
        in_specs=[pl.BlockSpec((tm, dm), row), pl.BlockSpec((tm, dm), row),
                  _const_spec((1, dm)), _const_spec((1, dm)),
                  pl.BlockSpec((1, 1, dm), mod),
                  _const_spec((1, dm)), _const_spec((1, dm)),
                  pl.BlockSpec((1, 1, dm), mod), pl.BlockSpec((1, 1, dm), mod),
                  _const_spec((dm, dm)), _const_spec((dm, LANES)), _const_spec((1, LANES))],
        out_specs=[pl.BlockSpec((tm, dm), row), pl.BlockSpec((tm, LANES), row)],
        out_shape=[jax.ShapeDtypeStruct((t, dm), F32), jax.ShapeDtypeStruct((t, LANES), F32)],
        compiler_params=_cparams(("parallel",), 56),
        name="merge",
    )(mix, x2d, ln_g, ln_b, g1, ln1_g, ln1_b, sc2, sh2, w_o, w_r, b_r)


def _route_body(lg_ref, w_ref, d_ref, cnt_ref, run_s, tot_s, *, tr):
    phase = pl.program_id(0)

    @pl.when(pl.program_id(1) == 0)
    def _():
        @pl.when(phase == 1)
        def _():
            tot_s[...] = run_s[...]
        run_s[...] = jnp.zeros_like(run_s)

    lane = lax.broadcasted_iota(I32, (tr, LANES), 1)
    lane_f = lane.astype(F32)
    logit = lg_ref[...]
    hot, val = [], []
    for _ in range(TOP_K):
        mk = jnp.max(logit, axis=-1, keepdims=True)
        ik = jnp.min(jnp.where(logit == mk, lane_f, float(LANES)), axis=-1, keepdims=True)
        hk = lane_f == ik
        logit = jnp.where(hk, -jnp.inf, logit)
        hot.append(hk)
        val.append(mk)
    cnt = jnp.zeros((tr, LANES), F32)
    for hk in hot:
        cnt = cnt + jnp.where(hk, 1.0, 0.0)
    run = run_s[...] + jnp.sum(cnt, axis=0, keepdims=True)

    @pl.when(phase == 0)
    def _():
        cnt_ref[...] = run.astype(I32)

    @pl.when(phase == 1)
    def _():
        total = tot_s[...]
        padded = jnp.floor((total + (MOE_BLOCK - 1.0)) * (1.0 / MOE_BLOCK)) * MOE_BLOCK
        lane8 = lax.broadcasted_iota(I32, (8, LANES), 1)
        incl = padded
        sft = 1
        while sft < LANES:
            incl = incl + jnp.where(lane8 >= sft, pltpu.roll(incl, sft, 1), 0.0)
            sft *= 2
        pstart = (incl - padded)[0:1, :]
        lower = (lax.broadcasted_iota(I32, (tr, tr), 0) > lax.broadcasted_iota(I32, (tr, tr), 1))
        before = jnp.dot(jnp.where(lower, 1.0, 0.0).astype(BF16), cnt.astype(BF16),
                         preferred_element_type=F32) + (run_s[0:1, :] + pstart)
        ex = [jnp.exp(v - val[0]) for v in val]
        denom = ex[0] + ex[1] + ex[2] + ex[3]
        w_out = jnp.zeros((tr, LANES), F32)
        d_out = jnp.zeros((tr, LANES), I32)
        for k in range(TOP_K):
            dest = jnp.sum(jnp.where(hot[k], before, 0.0), axis=-1, keepdims=True)
            w_out = jnp.where(lane == k, ex[k] / denom, w_out)
            d_out = jnp.where(lane == k, dest.astype(I32), d_out)
        w_ref[...] = w_out
        d_ref[...] = d_out[:, :TOP_K]
        cnt_ref[...] = total.astype(I32)

    run_s[...] = run


def _route(logits):
    t = logits.shape[0]
    tr = 512
    row = lambda p, i: (i, 0)
    out_row = lambda p, i: (i * p, 0)
    return pl.pallas_call(
        functools.partial(_route_body, tr=tr),
        grid=(2, t // tr),
        in_specs=[pl.BlockSpec((tr, LANES), row)],
        out_specs=[pl.BlockSpec((tr, LANES), out_row), pl.BlockSpec((tr, TOP_K), out_row),
                   pl.BlockSpec((8, LANES), lambda p, i: (0, 0))],
        out_shape=[jax.ShapeDtypeStruct((t, LANES), F32), jax.ShapeDtypeStruct((t, TOP_K), I32),
                   jax.ShapeDtypeStruct((8, LANES), I32)],
        scratch_shapes=[pltpu.VMEM((8, LANES), F32), pltpu.VMEM((8, LANES), F32)],
        compiler_params=_cparams(("arbitrary", "arbitrary"), 32),
        name="route",
    )(logits)


def _scatter_body(cnt_ref, pstart_ref, used_ref, dest_ref, x1_ref, sc_ref, sh_ref, xb_ref, tok_ref, zero_s, sem, pad_sem,
                  *, ts, n_blocks):
    def row_copy(src, r_src, r_dst, s):
        return pltpu.make_async_copy(src.at[pl.ds(r_src, 1)], xb_ref.at[pl.ds(r_dst, 1)], s)

    def block_copy(blk):
        return pltpu.make_async_copy(zero_s, xb_ref.at[pl.ds(pl.multiple_of(blk * MOE_BLOCK, MOE_BLOCK), MOE_BLOCK)],
                                     pad_sem)

    @pl.when(pl.program_id(0) == 0)
    def _():
        zero_s[...] = jnp.zeros_like(zero_s)

        def last_block(e):
            return pstart_ref[e] // MOE_BLOCK + cnt_ref[e] // MOE_BLOCK

        def pad_start(e, c):
            @pl.when(cnt_ref[e] % MOE_BLOCK != 0)
            def _():
                block_copy(last_block(e)).start()
            return c

        def pad_wait(e, c):
            @pl.when(cnt_ref[e] % MOE_BLOCK != 0)
            def _():
                block_copy(last_block(e)).wait()
            return c
        lax.fori_loop(0, N_EXPERTS, pad_start, 0)
        lax.fori_loop(0, N_EXPERTS, pad_wait, 0)

        def tail_start(blk, c):
            block_copy(blk).start()
            return c

        def tail_wait(blk, c):
            block_copy(blk).wait()
            return c
        lax.fori_loop(used_ref[0], n_blocks, tail_start, 0)
        lax.fori_loop(used_ref[0], n_blocks, tail_wait, 0)

    tok_ref[...] = x1_ref[...] * (1.0 + sc_ref[0]) + sh_ref[0]

    def row(r, carry):
        for k in range(TOP_K):
            row_copy(tok_ref, r, dest_ref[r * TOP_K + k], sem).start()
        return carry
    lax.fori_loop(0, ts, row, 0, unroll=4)
    for _ in range(TOP_K):
        pltpu.make_async_copy(tok_ref, xb_ref.at[pl.ds(0, ts)], sem).wait()


def _scatter(x1, sc2, sh2, dest_flat, counts, pstart, used, n_blocks, seq):
    t, dm = x1.shape
    ts = 256
    per_b = seq // ts
    grid_spec = pltpu.PrefetchScalarGridSpec(
        num_scalar_prefetch=3,
        grid=(t // ts,),
        in_specs=[pl.BlockSpec((ts * TOP_K,), lambda i, *_: (i,), memory_space=pltpu.SMEM),
                  pl.BlockSpec((ts, dm), lambda i, *_: (i, 0)),
                  pl.BlockSpec((1, 1, dm), lambda i, *_: (i // per_b, 0, 0)),
                  pl.BlockSpec((1, 1, dm), lambda i, *_: (i // per_b, 0, 0))],
        out_specs=pl.BlockSpec(memory_space=pl.ANY),
        scratch_shapes=[pltpu.VMEM((ts, dm), F32), pltpu.VMEM((MOE_BLOCK, dm), F32),
                        pltpu.SemaphoreType.DMA(()), pltpu.SemaphoreType.DMA(())],
    )
    return pl.pallas_call(
        functools.partial(_scatter_body, ts=ts, n_blocks=n_blocks),
        grid_spec=grid_spec,
        out_shape=jax.ShapeDtypeStruct((n_blocks * MOE_BLOCK, dm), F32),
        compiler_params=_cparams(("arbitrary",), 32),
        name="scatter",
    )(counts, pstart, used, dest_flat, x1, sc2, sh2)


def _ffn1_body(e_ref, c_ref, blk_ref, oblk_ref, oc_ref, first_ref, n_ref,
               x_ref, wg_ref, wu_ref, bg_ref, bu_ref, a_ref, wg_s, wu_s):
    s = pl.program_id(0)

    @pl.when(first_ref[s] == 1)
    def _():
        wg_s[...] = wg_ref[0].astype(BF16)
        wu_s[...] = wu_ref[0].astype(BF16)

    @pl.when(s < n_ref[0])
    def _():
        x = x_ref[...].astype(BF16)
        g = jnp.minimum(jnp.dot(x, wg_s[...], preferred_element_type=F32) + bg_ref[0], SWIGLU_LIMIT)
        u = jnp.clip(jnp.dot(x, wu_s[...], preferred_element_type=F32) + bu_ref[0], -SWIGLU_LIMIT, SWIGLU_LIMIT)
        a_ref[...] = (g * _sigmoid(SWIGLU_ALPHA * g) * (u + 1.0)).astype(BF16)

    @pl.when(s >= n_ref[0])
    def _():
        a_ref[...] = jnp.zeros_like(a_ref)


def _ffn2_body(e_ref, c_ref, blk_ref, oblk_ref, oc_ref, first_ref, n_ref, a_ref, wd_ref, bd_ref, y_ref, wd_s):
    s = pl.program_id(0)

    @pl.when(first_ref[s] == 1)
    def _():
        wd_s[...] = wd_ref[0].astype(BF16)

    @pl.when(s < n_ref[0])
    def _():
        y_ref[...] = jnp.dot(a_ref[...], wd_s[...], preferred_element_type=F32) + bd_ref[0]

    @pl.when(s >= n_ref[0])
    def _():
        y_ref[...] = jnp.zeros_like(y_ref)


def _ffn1(plan, xb, w_gate, w_up, b_gate, b_up, n_steps):
    rows, dm = xb.shape
    de = w_gate.shape[2]
    wspec = pl.BlockSpec((1, dm, FF_CHUNK), lambda s, e, c, *_: (e[s], 0, c[s]))
    bspec = pl.BlockSpec((1, 1, FF_CHUNK), lambda s, e, c, *_: (e[s], 0, c[s]))
    grid_spec = pltpu.PrefetchScalarGridSpec(
        num_scalar_prefetch=7,
        grid=(n_steps,),
        in_specs=[pl.BlockSpec((MOE_BLOCK, dm), lambda s, e, c, blk, *_: (blk[s], 0)), wspec, wspec, bspec, bspec],
        out_specs=pl.BlockSpec((MOE_BLOCK, FF_CHUNK), lambda s, e, c, blk, oblk, oc, *_: (oblk[s], oc[s])),
        scratch_shapes=[pltpu.VMEM((dm, FF_CHUNK), BF16), pltpu.VMEM((dm, FF_CHUNK), BF16)],
    )
    return pl.pallas_call(
        _ffn1_body, grid_spec=grid_spec,
        out_shape=jax.ShapeDtypeStruct((rows, de), BF16),
        compiler_params=_cparams(("arbitrary",), 60),
        name="ffn1",
    )(*plan, xb, w_gate, w_up, b_gate, b_up)


def _ffn2(plan, act, w_down, b_down, n_steps, chunk):
    rows, de = act.shape
    dm = w_down.shape[2]
    grid_spec = pltpu.PrefetchScalarGridSpec(
        num_scalar_prefetch=7,
        grid=(n_steps,),
        in_specs=[pl.BlockSpec((MOE_BLOCK, de), lambda s, e, c, blk, *_: (blk[s], 0)),
                  pl.BlockSpec((1, de, chunk), lambda s, e, c, *_: (e[s], 0, c[s])),
                  pl.BlockSpec((1, 1, chunk), lambda s, e, c, *_: (e[s], 0, c[s]))],
        out_specs=pl.BlockSpec((MOE_BLOCK, chunk), lambda s, e, c, blk, oblk, oc, *_: (oblk[s], oc[s])),
        scratch_shapes=[pltpu.VMEM((de, chunk), BF16)],
    )
    return pl.pallas_call(
        _ffn2_body, grid_spec=grid_spec,
        out_shape=jax.ShapeDtypeStruct((rows, dm), F32),
        compiler_params=_cparams(("arbitrary",), 60),
        name="ffn2",
    )(*plan, act, w_down, b_down)


def _combine_body(dcur_ref, dnxt_ref, w_ref, x1_ref, g2_ref, lg_ref, lb_ref, yb_ref, o_ref, buf, sem, *, tc, n_tiles):
    i = pl.program_id(0)

    def issue(d_ref, slot):
        def row(r, carry):
            for k in range(TOP_K):
                pltpu.make_async_copy(yb_ref.at[pl.ds(d_ref[r * TOP_K + k], 1)],
                                      buf.at[slot, k, pl.ds(r, 1)], sem.at[slot]).start()
            return carry
        lax.fori_loop(0, tc, row, 0, unroll=4)

    @pl.when(i == 0)
    def _():
        issue(dcur_ref, 0)

    @pl.when(i + 1 < n_tiles)
    def _():
        issue(dnxt_ref, (i + 1) % 2)

    slot = i % 2
    for k in range(TOP_K):
        pltpu.make_async_copy(yb_ref.at[pl.ds(0, tc)], buf.at[slot, k], sem.at[slot]).wait()
    w = w_ref[...]
    y = w[:, 0:1] * buf[slot, 0]
    for k in range(1, TOP_K):
        y = y + w[:, k:k + 1] * buf[slot, k]
    o_ref[...] = _layer_norm(DEEPNORM_ALPHA * x1_ref[...] + g2_ref[0] * y, lg_ref[...], lb_ref[...])


def _combine(dest_flat, w4, x1, g2, ln_g, ln_b, yb, seq):
    t, dm = x1.shape
    tc = 256
    n_tiles = t // tc
    per_b = seq // tc
    row = lambda i: (i, 0)
    return pl.pallas_call(
        functools.partial(_combine_body, tc=tc, n_tiles=n_tiles),
        grid=(n_tiles,),
        in_specs=[pl.BlockSpec((tc * TOP_K,), lambda i: (i,), memory_space=pltpu.SMEM),
                  pl.BlockSpec((tc * TOP_K,), lambda i: (jnp.minimum(i + 1, n_tiles - 1),), memory_space=pltpu.SMEM),
                  pl.BlockSpec((tc, LANES), row),
                  pl.BlockSpec((tc, dm), row),
                  pl.BlockSpec((1, 1, dm), lambda i: (i // per_b, 0, 0)),
                  _const_spec((1, dm)), _const_spec((1, dm)),
                  pl.BlockSpec(memory_space=pl.ANY)],
        out_specs=pl.BlockSpec((tc, dm), row),
        out_shape=jax.ShapeDtypeStruct((t, dm), F32),
        scratch_shapes=[pltpu.VMEM((2, TOP_K, tc, dm), F32), pltpu.SemaphoreType.DMA((2,))],
        compiler_params=_cparams(("arbitrary",), 40),
        name="combine",
    )(dest_flat, dest_flat, w4, x1, g2, ln_g, ln_b, yb)


def _dft_tables(seq):
    n = 2 * seq
    j = np.arange(seq // 2, dtype=np.int64)
    tables = []
    for k in (2 * j, 2 * j + 1):
        ang = (2.0 * np.pi / n) * ((k[:, None] * j[None, :]) % n).astype(np.float64)
        tables += [np.cos(ang), np.sin(ang)]
    tables += [tables[2].T, tables[3].T]
    return tuple(jnp.asarray(m, F32).astype(BF16) for m in tables)


def _filter_features(seq):
    t = jnp.linspace(0.0, 1.0, seq, dtype=F32)[:, None]
    bands = (HYENA_EMB - 1) // 2
    f = jnp.linspace(1e-4, bands - 1, bands, dtype=F32)[None, :]
    ang = 2.0 * math.pi * jnp.arange(seq, dtype=F32)[:, None] * f / seq
    feats = jnp.concatenate([t, jnp.cos(ang), -jnp.sin(ang)], axis=-1)
    max_decay = math.log(HYENA_DECAY_TARGET) / HYENA_SHORT_DECAY_PCT
    min_decay = math.log(HYENA_DECAY_TARGET) / HYENA_LONG_DECAY_PCT
    deltas = jnp.abs(jnp.linspace(min_decay, max_decay, HYENA_WIDTH, dtype=F32))[None, :]
    return t, feats, deltas


def _moe_plan(counts, n_blocks, n_chunks):
    ids = jnp.arange(N_EXPERTS, dtype=I32)
    nblk = (counts + MOE_BLOCK - 1) // MOE_BLOCK
    blk_end = jnp.cumsum(nblk)
    blk_start = blk_end - nblk
    used = blk_end[-1]
    steps = n_chunks * nblk
    step_end = jnp.cumsum(steps)
    n_used = step_end[-1]
    s_all = jnp.arange(n_chunks * n_blocks, dtype=I32)
    s = jnp.minimum(s_all, n_used - 1)
    e_s = jnp.minimum(jnp.sum((s[:, None] >= step_end[None, :]).astype(I32), axis=1), N_EXPERTS - 1)
    onehot = e_s[:, None] == ids[None, :]
    pick = lambda table: jnp.sum(jnp.where(onehot, table[None, :], 0), axis=1)
    loc = s - pick(step_end - steps)
    nb = jnp.maximum(pick(nblk), 1)
    c_s = loc // nb
    r_s = loc % nb
    blk = pick(blk_start) + r_s
    tail = s_all >= n_used
    j = jnp.maximum(s_all - n_used, 0)
    n_tail = jnp.maximum(n_blocks - used, 1)
    oblk = jnp.where(tail, used + j % n_tail, blk)
    oc = jnp.where(tail, j // n_tail, c_s)
    first = jnp.logical_and(r_s == 0, jnp.logical_not(tail))
    as_i32 = lambda v: v.astype(I32)
    plan = tuple(map(as_i32, (e_s, c_s, blk, oblk, oc, first, n_used.reshape(1))))
    return as_i32(blk_start * MOE_BLOCK), as_i32(used.reshape(1)), plan


def kernel(x, c, ctx, c_ctx, ln_in_g, ln_in_b, w_mod, b_mod, w_in, b_in, mlstm_conv_w, mlstm_conv_b,
           w_qh, w_kh, hyena_conv_w, hyena_conv_b, filt_w1, filt_b1, filt_wh, filt_bh, filt_freq, filt_wout,
           hyena_skip, w_proj_a, w_proj_h, w_out, ln1_g, ln1_b, w_router, b_router, w_gate, b_gate,
           w_up, b_up, w_down, b_down, ln2_g, ln2_b):
    bsz, seq, dm = x.shape
    ctx_len = ctx.shape[1]
    t = bsz * seq
    assert w_mod.shape[0] == DEPTH and dm == D_MODEL and ctx_len == CHUNK and bsz + 1 <= 16
    row = lambda v: v.reshape(1, -1)

    cond = jnp.concatenate([c, c_ctx[None], jnp.zeros((16 - bsz - 1, dm), F32)], axis=0)
    mod = _mod(cond, w_mod[0], row(b_mod[0]))
    sh1, sc1, g1, sh2, sc2, g2 = [m[:, None, :] for m in jnp.split(mod, 6, axis=-1)]

    w_main = jnp.concatenate([w_in[0][:, :IN_GATES].astype(BF16), w_in[0][:, IN_O:].astype(BF16)], axis=1)
    b_main = row(jnp.concatenate([b_in[0][:IN_GATES], b_in[0][IN_O:]]))
    w_g = jnp.pad(w_in[0][:, IN_GATES:IN_O], ((0, 0), (0, LANES - 4 * HEADS))).astype(BF16)
    b_g = row(jnp.pad(b_in[0][IN_GATES:IN_O], (0, LANES - 4 * HEADS)))
    lng, lnb = row(ln_in_g), row(ln_in_b)
    x2d = x.reshape(t, dm)
    z, gates, sg = _in_proj(x2d, lng, lnb, sc1[:bsz], sh1[:bsz], w_main, b_main, w_g, b_g, seq, Z_BG)
    zc, gates_c = _in_proj(ctx.reshape(bsz * ctx_len, dm), lng, lnb, sc1[bsz:bsz + 1], sh1[bsz:bsz + 1],
                           w_main[:, :IN_GATES], b_main[:, :IN_GATES], w_g, b_g, bsz * ctx_len, IN_GATES)

    g_all = jnp.concatenate([gates_c[:, :4 * HEADS].reshape(bsz, ctx_len, 4, HEADS),
                             gates[:, :4 * HEADS].reshape(bsz, seq, 4, HEADS)], axis=1)
    gates_t = g_all.transpose(0, 3, 2, 1)

    oh = _mlstm(z, zc, gates_t, mlstm_conv_w[0].reshape(9, MLSTM_WIDTH), row(mlstm_conv_b[0]),
                w_qh[0], w_kh[0].transpose(0, 2, 1), bsz, seq, ctx_len)

    tcol, feats, deltas = _filter_features(seq)
    feats = jnp.pad(feats, ((0, 0), (0, LANES - HYENA_EMB)))
    w1 = jnp.pad(filt_w1[0], ((0, LANES - HYENA_EMB), (0, 0)))
    a = _filt(feats, w1, row(filt_b1[0]), filt_wh[0], filt_bh[0], filt_freq[0])
    fh = a.shape[1]
    w_fout4 = filt_wout[0].reshape(fh, 4, HYENA_WIDTH).transpose(1, 0, 2)
    hh = _hyena(z, hyena_conv_w[0].reshape(9, 3 * HYENA_WIDTH), row(hyena_conv_b[0]), a, w_fout4, deltas, tcol,
                hyena_skip[0], _dft_tables(seq), bsz, seq)

    w_r = jnp.pad(w_router[0], ((0, 0), (0, LANES - N_EXPERTS)))
    b_r = row(jnp.pad(b_router[0], (0, LANES - N_EXPERTS), constant_values=-1e30))
    mix = _mix(oh, hh, sg, w_proj_a[0].astype(BF16), w_proj_h[0].astype(BF16))
    x1, logits = _merge(mix, x2d, lng, lnb, g1[:bsz], row(ln1_g[0]), row(ln1_b[0]), sc2[:bsz], sh2[:bsz],
                        w_out[0].astype(BF16), w_r, b_r, seq)

    w4, dest, counts = _route(logits)
    n_blocks = -(-(t * TOP_K + N_EXPERTS * (MOE_BLOCK - 1)) // MOE_BLOCK)
    n_chunks = dm // FF_CHUNK
    counts = counts[0, :N_EXPERTS]
    pstart, used, plan = _moe_plan(counts, n_blocks, n_chunks)
    dest_flat = dest.reshape(t * TOP_K)
    xb = _scatter(x1, sc2[:bsz], sh2[:bsz], dest_flat, counts, pstart, used, n_blocks, seq)
    act = _ffn1(plan, xb, w_gate[0], w_up[0], b_gate[0][:, None, :], b_up[0][:, None, :], n_chunks * n_blocks)
    _, _, plan_down = _moe_plan(counts, n_blocks, 1)
    yb = _ffn2(plan_down, act, w_down[0], b_down[0][:, None, :], n_blocks, dm)
    out = _combine(dest_flat, w4, x1, g2[:bsz], row(ln2_g[0]), row(ln2_b[0]), yb, seq)
    return out.reshape(bsz, seq, dm)
```

```python
import functools
import math

import numpy as np
import jax
import jax.numpy as jnp
from jax import lax
from jax.experimental import pallas as pl
from jax.experimental.pallas import tpu as pltpu

F32 = jnp.float32
BF16 = jnp.bfloat16
I32 = jnp.int32
HIGHEST = lax.Precision.HIGHEST

D_MODEL = 2048
GRID_W = 64
HEADS = 4
HEAD_DIM = 256
MLSTM_WIDTH = HEADS * HEAD_DIM
HYENA_WIDTH = D_MODEL // 2
HYENA_EMB = 33
HYENA_DECAY_TARGET = 1e-2
HYENA_SHORT_DECAY_PCT = 0.3
HYENA_LONG_DECAY_PCT = 1.5
N_EXPERTS = 32
TOP_K = 4
SWIGLU_LIMIT = 7.0
SWIGLU_ALPHA = 1.702
LN_EPS = 1e-5
DEPTH = 1
DEEPNORM_ALPHA = (2.0 * DEPTH) ** 0.25

IN_V = MLSTM_WIDTH
IN_GATES = 2 * MLSTM_WIDTH
IN_O = IN_GATES + 4 * HEADS
Z_QK, Z_V, Z_O, Z_HY, Z_BG = 0, 1024, 2048, 3072, 6144

LANES = 128
MXU = 256
CHUNK = 256
MOE_BLOCK = 512
FF_CHUNK = 1024
MIB = 1024 * 1024


def _cparams(semantics, vmem_mib):
    return pltpu.CompilerParams(dimension_semantics=semantics, vmem_limit_bytes=vmem_mib * MIB)


def _const_spec(shape):
    nd = len(shape)
    return pl.BlockSpec(shape, lambda *_: (0,) * nd, pipeline_mode=pl.Buffered(1))


def _layer_norm(x, g, b):
    mu = jnp.mean(x, axis=-1, keepdims=True)
    xc = x - mu
    var = jnp.mean(xc * xc, axis=-1, keepdims=True)
    return xc * lax.rsqrt(var + LN_EPS) * g + b


def _sigmoid(x):
    return 1.0 / (1.0 + jnp.exp(-x))


def _silu(x):
    return x * _sigmoid(x)


def _log_sigmoid(x):
    return jnp.minimum(x, 0.0) - jnp.log(1.0 + jnp.exp(-jnp.abs(x)))


def _mod_body(c_ref, w_ref, b_ref, o_ref):
    s = _silu(c_ref[...])
    o_ref[...] = jnp.dot(s.astype(BF16), w_ref[...].astype(BF16), preferred_element_type=F32) + b_ref[...]


def _mod(cond, w, b):
    rows, dm = cond.shape
    n = w.shape[1]
    tn = 1024
    return pl.pallas_call(
        _mod_body,
        grid=(n // tn,),
        in_specs=[pl.BlockSpec((rows, dm), lambda j: (0, 0)),
                  pl.BlockSpec((dm, tn), lambda j: (0, j)),
                  pl.BlockSpec((1, tn), lambda j: (0, j))],
        out_specs=pl.BlockSpec((rows, tn), lambda j: (0, j)),
        out_shape=jax.ShapeDtypeStruct((rows, n), F32),
        compiler_params=_cparams(("arbitrary",), 40),
        name="mod",
    )(cond, w, b)


def _in_proj_body(x_ref, lng_ref, lnb_ref, sc_ref, sh_ref, wa_ref, ba_ref, wb_ref, bb_ref, wg_ref, bg_ref,
                  z_ref, g_ref, *rest, tm, n_a, n_plain):
    hx_s = rest[-1]
    j = pl.program_id(1)

    @pl.when(j == 0)
    def _():
        def rows(r, carry):
            sl = pl.ds(pl.multiple_of(r * 128, 128), 128)
            xn = _layer_norm(x_ref[sl, :], lng_ref[...], lnb_ref[...])
            hx_s[sl, :] = (xn * (1.0 + sc_ref[0]) + sh_ref[0]).astype(BF16)
            return carry
        lax.fori_loop(0, tm // 128, rows, 0)
        g_ref[...] = jnp.dot(hx_s[...], wg_ref[...], preferred_element_type=F32) + bg_ref[...]

    def emit(w_ref, b_ref):
        acc = jnp.dot(hx_s[...], w_ref[...], preferred_element_type=F32) + b_ref[...]
        if len(rest) == 1:
            z_ref[...] = acc
        else:
            @pl.when(j < n_plain)
            def _():
                z_ref[...] = acc

            @pl.when(j >= n_plain)
            def _():
                rest[0][...] = _sigmoid(acc).astype(BF16)

    @pl.when(j < n_a)
    def _():
        emit(wa_ref, ba_ref)

    @pl.when(j >= n_a)
    def _():
        emit(wb_ref, bb_ref)


def _in_proj(x2d, ln_g, ln_b, scale, shift, w_a, b_a, w_b, b_b, w_gates, b_gates, rows_per_mod, n_cols, plain_cols):
    t, dm = x2d.shape
    tm, tn = min(1024, t), 1024
    n_a = w_a.shape[1] // tn
    n_plain = plain_cols // tn
    out_specs = [pl.BlockSpec((tm, tn), lambda i, j: (i, jnp.minimum(j, n_plain - 1))),
                 pl.BlockSpec((tm, LANES), lambda i, j: (i, 0))]
    out_shape = [jax.ShapeDtypeStruct((t, plain_cols), F32), jax.ShapeDtypeStruct((t, LANES), F32)]
    if n_cols > plain_cols:
        out_specs.append(pl.BlockSpec((tm, tn), lambda i, j: (i, jnp.maximum(j - n_plain, 0))))
        out_shape.append(jax.ShapeDtypeStruct((t, n_cols - plain_cols), BF16))
    a_tile = lambda i, j: (0, jnp.minimum(j, n_a - 1))
    b_tile = lambda i, j: (0, jnp.maximum(j - n_a, 0))
    return pl.pallas_call(
        functools.partial(_in_proj_body, tm=tm, n_a=n_a, n_plain=n_plain),
        grid=(t // tm, n_cols // tn),
        in_specs=[pl.BlockSpec((tm, dm), lambda i, j: (i, 0)),
                  pl.BlockSpec((1, dm), lambda i, j: (0, 0)),
                  pl.BlockSpec((1, dm), lambda i, j: (0, 0)),
                  pl.BlockSpec((1, 1, dm), lambda i, j: (i * tm // rows_per_mod, 0, 0)),
                  pl.BlockSpec((1, 1, dm), lambda i, j: (i * tm // rows_per_mod, 0, 0)),
                  pl.BlockSpec((dm, tn), a_tile), pl.BlockSpec((1, tn), a_tile),
                  pl.BlockSpec((dm, tn), b_tile), pl.BlockSpec((1, tn), b_tile),
                  pl.BlockSpec((dm, LANES), lambda i, j: (0, 0)),
                  pl.BlockSpec((1, LANES), lambda i, j: (0, 0))],
        out_specs=out_specs,
        out_shape=out_shape,
        scratch_shapes=[pltpu.VMEM((tm, dm), BF16)],
        compiler_params=_cparams(("parallel", "arbitrary"), 56),
        name="in_proj",
    )(x2d, ln_g, ln_b, scale, shift, w_a, b_a, w_b, b_b, w_gates, b_gates)


def _dwconv(u, w9, bias, width, single_row):
    length, ch = u.shape
    col = lax.broadcasted_iota(I32, (length, ch), 0) % width
    if not single_row:
        zpad = jnp.zeros((width, ch), F32)
        up = jnp.concatenate([zpad, u[:length - width]], axis=0)
        dn = jnp.concatenate([u[width:], zpad], axis=0)
    out = None
    for dc in (-1, 0, 1):
        a = u * w9[4 + dc:5 + dc]
        if not single_row:
            a = a + up * w9[1 + dc:2 + dc] + dn * w9[7 + dc:8 + dc]
        if dc == -1:
            a = jnp.where(col == 0, 0.0, pltpu.roll(a, 1, 0))
        elif dc == 1:
            a = jnp.where(col == width - 1, 0.0, pltpu.roll(a, length - 1, 0))
        out = a if out is None else out + a
    return out + bias


def _mlstm_body(zqk_ref, zv_ref, zo_ref, cqk_ref, cv_ref, gt_ref, cw_ref, cb_ref, wq_ref, wkt_ref,
                o_ref, q_s, k_s, kt_s, v_s, hf_s, hb_s, r_s, c_s, ct_s, *, ctx_len, seq):
    n_chunks = (ctx_len + seq) // CHUNK
    total = ctx_len + seq
    cw = cw_ref[...]
    cb = cb_ref[...]
    wq = wq_ref[0].astype(BF16)
    wkt = wkt_ref[0].astype(BF16)
    nt = (((1,), (1,)), ((), ()))
    scale = HEAD_DIM ** -0.5

    def project(u, off, n):
        ub = u.astype(BF16)
        q_s[off:off + n, :] = jnp.dot(ub, wq, preferred_element_type=F32).astype(BF16)
        k_s[off:off + n, :] = (lax.dot_general(ub, wkt, nt, preferred_element_type=F32) * scale).astype(BF16)
        kt_s[:, off:off + n] = (lax.dot_general(wkt, ub, nt, preferred_element_type=F32) * scale).astype(BF16)

    project(_silu(_dwconv(cqk_ref[...], cw, cb, ctx_len, True)), 0, ctx_len)
    project(_silu(_dwconv(zqk_ref[...], cw, cb, GRID_W, False)), ctx_len, seq)
    v_s[0:ctx_len, :] = cv_ref[...]
    v_s[ctx_len:total, :] = zv_ref[...]

    gt = gt_ref[0, 0]
    lf = _log_sigmoid(gt)
    pos = lax.broadcasted_iota(I32, (4, total), 1) % CHUNK
    pre, suf = lf, lf
    s = 1
    while s < CHUNK:
        pre = pre + jnp.where(pos >= s, pltpu.roll(pre, s, 1), 0.0)
        suf = suf + jnp.where(pos < CHUNK - s, pltpu.roll(suf, total - s, 1), 0.0)
        s *= 2
    b_f, li_f, b_b, li_b = pre[1:2], gt[0:1], suf[3:4], gt[2:3]
    pm_f, pm_b = li_f - b_f, li_b - b_b
    pos1 = pos[0:1]
    s = 1
    while s < CHUNK:
        pm_f = jnp.maximum(pm_f, jnp.where(pos1 >= s, pltpu.roll(pm_f, s, 1), -jnp.inf))
        pm_b = jnp.maximum(pm_b, jnp.where(pos1 < CHUNK - s, pltpu.roll(pm_b, total - s, 1), -jnp.inf))
        s *= 2
    r_s[...] = jnp.concatenate([b_f, li_f, pm_f, b_b, li_b, pm_b, jnp.zeros((2, total), F32)], axis=0)
    zfill = jnp.zeros((CHUNK - 8, CHUNK), F32)
    for c in range(n_chunks):
        blk = jnp.concatenate([r_s[:, c * CHUNK:(c + 1) * CHUNK], zfill], axis=0).T
        c_s[c * CHUNK:(c + 1) * CHUNK, :] = blk[:, :LANES]

    ct_s[...] = jnp.zeros_like(ct_s)
    row_i = lax.broadcasted_iota(I32, (CHUNK, CHUNK), 0)
    col_i = lax.broadcasted_iota(I32, (CHUNK, CHUNK), 1)
    masks = (row_i >= col_i, row_i <= col_i)

    def chunk_step(c, d, n_vec, m):
        off = c * CHUNK if isinstance(c, int) else pl.multiple_of(c * CHUNK, CHUNK)
        rows = r_s[:, pl.ds(off, CHUNK)]
        cols = c_s[pl.ds(off, CHUNK), :]
        b_row, li_row, pm_row = rows[3 * d:3 * d + 1], rows[3 * d + 1:3 * d + 2], rows[3 * d + 2:3 * d + 3]
        b_col, li_col, pm_col = cols[:, 3 * d:3 * d + 1], cols[:, 3 * d + 1:3 * d + 2], cols[:, 3 * d + 2:3 * d + 3]
        last = slice(CHUNK - 1, CHUNK) if d == 0 else slice(0, 1)
        b_end, pm_end = b_row[:, last], pm_row[:, last]
        qc = q_s[pl.ds(off, CHUNK), :]
        kc = k_s[pl.ds(off, CHUNK), :]
        ktc = kt_s[:, pl.ds(off, CHUNK)]
        vc = v_s[pl.ds(off, CHUNK), :]
        ct = ct_s[d]

        inter = b_col + m
        m_t = jnp.maximum(inter, b_col + pm_col)
        wts = jnp.exp(jnp.where(masks[d], b_col - b_row + li_row, -jnp.inf) - m_t)
        s_inter = jnp.exp(inter - m_t)
        scores = jnp.dot(qc, ktc, preferred_element_type=F32) * wts
        num = (s_inter * jnp.dot(qc, ct.astype(BF16), preferred_element_type=F32)
               + jnp.dot(scores.astype(BF16), vc.astype(BF16), preferred_element_type=F32))
        den = (s_inter * jnp.sum(qc.astype(F32) * n_vec, axis=-1, keepdims=True)
               + jnp.sum(scores, axis=-1, keepdims=True))
        h = num / jnp.maximum(jnp.abs(den), jnp.exp(-m_t))

        m_new = jnp.maximum(b_end + m, b_end + pm_end)
        decay = jnp.exp(b_end + m - m_new)
        w = jnp.exp(b_end - b_col + li_col - m_new)
        ct_s[d] = decay * ct + jnp.dot(ktc, (vc * w).astype(BF16), preferred_element_type=F32)
        n_new = decay * n_vec + jnp.sum(kc.astype(F32) * w, axis=0, keepdims=True)
        return h, n_new, m_new

    n0 = jnp.zeros((1, HEAD_DIM), F32)
    m0 = jnp.zeros((1, 1), F32)
    _, nf, mf = chunk_step(0, 0, n0, m0)
    _, nb, mb = chunk_step(0, 1, n0, m0)

    def body(i, carry):
        nf, mf, nb, mb = carry
        hf, nf, mf = chunk_step(i, 0, nf, mf)
        hf_s[pl.ds(pl.multiple_of(i * CHUNK - ctx_len, CHUNK), CHUNK), :] = hf
        j = n_chunks - i
        hb, nb, mb = chunk_step(j, 1, nb, mb)
        hb_s[pl.ds(pl.multiple_of(j * CHUNK - ctx_len, CHUNK), CHUNK), :] = hb
        return nf, mf, nb, mb

    lax.fori_loop(1, n_chunks, body, (nf, mf, nb, mb))
    o_ref[...] = (_sigmoid(zo_ref[...]) * (hf_s[...] + hb_s[...])).astype(BF16)


def _mlstm(z, zc, gates_t, conv_w9, conv_b, w_qh, w_kh_t, bsz, seq, ctx_len):
    total = ctx_len + seq
    hd = HEAD_DIM
    qk_blk, v_blk, o_blk = Z_QK // hd, Z_V // hd, Z_O // hd
    return pl.pallas_call(
        functools.partial(_mlstm_body, ctx_len=ctx_len, seq=seq),
        grid=(bsz, HEADS),
        in_specs=[pl.BlockSpec((seq, hd), lambda b, h: (b, qk_blk + h)),
                  pl.BlockSpec((seq, hd), lambda b, h: (b, v_blk + h)),
                  pl.BlockSpec((seq, hd), lambda b, h: (b, o_blk + h)),
                  pl.BlockSpec((ctx_len, hd), lambda b, h: (b, qk_blk + h)),
                  pl.BlockSpec((ctx_len, hd), lambda b, h: (b, v_blk + h)),
                  pl.BlockSpec((1, 1, 4, total), lambda b, h: (b, h, 0, 0)),
                  pl.BlockSpec((9, hd), lambda b, h: (0, h)),
                  pl.BlockSpec((1, hd), lambda b, h: (0, h)),
                  pl.BlockSpec((1, hd, hd), lambda b, h: (h, 0, 0)),
                  pl.BlockSpec((1, hd, hd), lambda b, h: (h, 0, 0))],
        out_specs=pl.BlockSpec((seq, hd), lambda b, h: (b, h)),
        out_shape=jax.ShapeDtypeStruct((bsz * seq, MLSTM_WIDTH), BF16),
        scratch_shapes=[pltpu.VMEM((total, hd), BF16), pltpu.VMEM((total, hd), BF16), pltpu.VMEM((hd, total), BF16),
                        pltpu.VMEM((total, hd), F32), pltpu.VMEM((seq, hd), F32), pltpu.VMEM((seq, hd), F32),
                        pltpu.VMEM((8, total), F32), pltpu.VMEM((total, LANES), F32), pltpu.VMEM((2, hd, hd), F32)],
        compiler_params=_cparams(("parallel", "arbitrary"), 48),
        name="mlstm",
    )(z, z, z, zc, zc, gates_t, conv_w9, conv_b, w_qh, w_kh_t)


def _filt_body(z_ref, w1_ref, b1_ref, wh_ref, bh_ref, fr_ref, a_ref):
    fr = fr_ref[...]
    a = jnp.sin(fr[0:1] * (jnp.dot(z_ref[...], w1_ref[...], precision=HIGHEST, preferred_element_type=F32)
                           + b1_ref[...]))
    for i in range(2):
        a = jnp.sin(fr[i + 1:i + 2] * (jnp.dot(a, wh_ref[i], precision=HIGHEST, preferred_element_type=F32)
                                       + bh_ref[i:i + 1]))
    a_ref[...] = a


def _filt(feats, w1, b1, wh, bh, freq):
    length = feats.shape[0]
    fh = w1.shape[1]
    return pl.pallas_call(
        _filt_body,
        out_shape=jax.ShapeDtypeStruct((length, fh), F32),
        name="filt",
    )(feats, w1, b1, wh, bh, freq)


def _hyena_body(zx1_ref, zx2_ref, zv_ref, cw1_ref, cw2_ref, cwv_ref, cb1_ref, cb2_ref, cbv_ref,
                a_ref, wf_ref, dl_ref, t_ref, skip_ref, ce_ref, se_ref, co_ref, so_ref, cot_ref, sot_ref, o_ref,
                hce_s, hse_s, hco_s, hso_s, hn_s, v_s, g_s, *, seq):
    n_fft = 2 * seq
    half = seq // 2
    sign = jnp.where(lax.broadcasted_iota(I32, (seq, 1), 0) % 2 == 0, 1.0, -1.0)
    jrow = lax.broadcasted_iota(I32, (half, 1), 0)
    sgn_j = jnp.where(jrow % 2 == 0, 1.0, -1.0)
    first = jrow == 0
    flip = jnp.where(lax.broadcasted_iota(I32, (MXU, MXU), 0) + lax.broadcasted_iota(I32, (MXU, MXU), 1) == MXU - 1,
                     1.0, 0.0).astype(BF16)

    def mm(m_ref, x):
        return jnp.dot(m_ref[...], x, preferred_element_type=F32)

    def reverse_rows(x):
        hi = x.astype(BF16)
        lo = (x - hi.astype(F32)).astype(BF16)
        blocks = []
        for b in range(half // MXU):
            src = slice(half - MXU * (b + 1), half - MXU * b)
            blocks.append(jnp.dot(flip, hi[src], preferred_element_type=F32)
                          + jnp.dot(flip, lo[src], preferred_element_type=F32))
        return jnp.concatenate(blocks, axis=0)

    def fold(v):
        rolled = pltpu.roll(reverse_rows(v[half:]), 1, 0)
        mid = rolled[0:1]
        vr = jnp.where(first, 0.0, rolled)
        return v[:half] + vr, v[:half] - vr, mid

    def forward(v):
        s, d, mid = fold(v)
        sb, db = s.astype(BF16), d.astype(BF16)
        return mm(ce_ref, sb) + sgn_j * mid, mm(se_ref, db), mm(co_ref, db), mm(so_ref, sb) + sgn_j * mid

    @pl.when(pl.program_id(1) == 0)
    def _():
        window = jnp.exp(-t_ref[...] * dl_ref[...])
        row0 = lax.broadcasted_iota(I32, (seq, 1), 0) == 0
        for o in range(2):
            fwd = jnp.dot(a_ref[...], wf_ref[2 * o], precision=HIGHEST, preferred_element_type=F32) * window
            bwd = jnp.dot(a_ref[...], wf_ref[2 * o + 1], precision=HIGHEST, preferred_element_type=F32) * window
            bwd = jnp.where(row0, 0.0, bwd)
            even = fwd + bwd
            hce_s[o], _, hco_s[o], _ = forward(even)
            _, hse_s[o], _, hso_s[o] = forward(fwd - bwd)
            hn_s[o] = jnp.sum(even * sign, axis=0, keepdims=True)

    def conv_to(dst_ref, z_ref, cw_ref, cb_ref):
        for lo in range(0, z_ref.shape[1], LANES):
            ls = slice(lo, lo + LANES)
            dst_ref[:, ls] = _dwconv(z_ref[:, ls], cw_ref[:, ls], cb_ref[:, ls], GRID_W, False)

    conv_to(v_s, zv_ref, cwv_ref, cbv_ref)

    for o, (zg_ref, cwg_ref, cbg_ref) in enumerate(((zx1_ref, cw1_ref, cb1_ref), (zx2_ref, cw2_ref, cb2_ref))):
        v = v_s[...]
        nyq = jnp.sum(v * sign, axis=0, keepdims=True) * hn_s[o] * (1.0 / n_fft)
        xce, xse, xco, xso = forward(v)
        hce, hse, hco, hso = hce_s[o], hse_s[o], hco_s[o], hso_s[o]
        scale_e = jnp.where(first, 1.0 / n_fft, 2.0 / n_fft)
        zce = scale_e * (xce * hce - xse * hse)
        zse = scale_e * (xce * hse + xse * hce)
        zco = (2.0 / n_fft) * (xco * hco - xso * hso)
        zso = (2.0 / n_fft) * (xco * hso + xso * hco)
        y_mid = jnp.sum(sgn_j * (zce + zso), axis=0, keepdims=True)
        sym = mm(ce_ref, zce.astype(BF16)) + mm(sot_ref, zso.astype(BF16))
        asym = mm(cot_ref, zco.astype(BF16)) + mm(se_ref, zse.astype(BF16))
        rolled = pltpu.roll(reverse_rows(sym - asym), 1, 0)
        y = jnp.concatenate([sym + asym, jnp.where(first, y_mid, rolled)], axis=0)
        conv_to(g_s, zg_ref, cwg_ref, cbg_ref)
        v_s[...] = g_s[...] * (y + sign * nyq + v * skip_ref[o:o + 1, :])

    o_ref[...] = v_s[...].astype(BF16)


def _hyena(z, conv_w9, conv_b, a, w_fout4, deltas, tcol, skip, tables, bsz, seq):
    ct = MXU
    n_ct = HYENA_WIDTH // ct
    hy = Z_HY // ct
    half = seq // 2
    zspec = lambda off: pl.BlockSpec((seq, ct), lambda j, b: (b, hy + off * n_ct + j))
    wspec = lambda off: pl.BlockSpec((9, ct), lambda j, b: (0, off * n_ct + j))
    bspec = lambda off: pl.BlockSpec((1, ct), lambda j, b: (0, off * n_ct + j))
    fh = a.shape[1]
    return pl.pallas_call(
        functools.partial(_hyena_body, seq=seq),
        grid=(n_ct, bsz),
        in_specs=[zspec(0), zspec(1), zspec(2), wspec(0), wspec(1), wspec(2), bspec(0), bspec(1), bspec(2),
                  _const_spec((seq, fh)),
                  pl.BlockSpec((4, fh, ct), lambda j, b: (0, 0, j)),
                  pl.BlockSpec((1, ct), lambda j, b: (0, j)),
                  _const_spec((seq, 1)),
                  pl.BlockSpec((2, ct), lambda j, b: (0, j))] + [_const_spec((half, half))] * len(tables),
        out_specs=pl.BlockSpec((seq, ct), lambda j, b: (b, j)),
        out_shape=jax.ShapeDtypeStruct((bsz * seq, HYENA_WIDTH), BF16),
        scratch_shapes=[pltpu.VMEM((2, half, ct), F32)] * 4 + [pltpu.VMEM((2, 1, ct), F32),
                                                               pltpu.VMEM((seq, ct), F32), pltpu.VMEM((seq, ct), F32)],
        compiler_params=_cparams(("arbitrary", "arbitrary"), 60),
        name="hyena",
    )(z, z, z, conv_w9, conv_w9, conv_w9, conv_b, conv_b, conv_b, a, w_fout4, deltas, tcol, skip, *tables)


def _mix_body(oh_ref, hh_ref, ga_ref, gh_ref, wa_ref, wh_ref, m_ref):
    y_a = jnp.dot(oh_ref[...], wa_ref[...], preferred_element_type=F32)
    y_h = jnp.dot(hh_ref[...], wh_ref[...], preferred_element_type=F32)
    m_ref[...] = (ga_ref[...].astype(F32) * y_a + gh_ref[...].astype(F32) * y_h).astype(BF16)


def _mix(oh, hh, sg, w_a, w_h):
    t = oh.shape[0]
    dm = w_a.shape[1]
    tm = 512
    row = lambda i: (i, 0)
    return pl.pallas_call(
        _mix_body,
        grid=(t // tm,),
        in_specs=[pl.BlockSpec((tm, MLSTM_WIDTH), row), pl.BlockSpec((tm, HYENA_WIDTH), row),
                  pl.BlockSpec((tm, dm), row), pl.BlockSpec((tm, dm), lambda i: (i, 1)),
                  _const_spec((MLSTM_WIDTH, dm)), _const_spec((HYENA_WIDTH, dm))],
        out_specs=pl.BlockSpec((tm, dm), row),
        out_shape=jax.ShapeDtypeStruct((t, dm), BF16),
        compiler_params=_cparams(("parallel",), 48),
        name="mix",
    )(oh, hh, sg, sg, w_a, w_h)


def _merge_body(m_ref, x_ref, lng_ref, lnb_ref, g1_ref, l1g_ref, l1b_ref, sc2_ref, sh2_ref, wo_ref, wr_ref, br_ref,
                x1_ref, lg_ref, mo_s, *, tm):
    mo_s[...] = jnp.dot(m_ref[...], wo_ref[...], preferred_element_type=F32)
    wr = wr_ref[...]
    wr_hi = wr.astype(BF16)
    wr_lo = (wr - wr_hi.astype(F32)).astype(BF16)

    def rows(r, carry):
        sl = pl.ds(pl.multiple_of(r * 128, 128), 128)
        x0 = _layer_norm(x_ref[sl, :], lng_ref[...], lnb_ref[...])
        x1 = _layer_norm(DEEPNORM_ALPHA * x0 + g1_ref[0] * mo_s[sl, :], l1g_ref[...], l1b_ref[...])
        x1_ref[sl, :] = x1
        tok = x1 * (1.0 + sc2_ref[0]) + sh2_ref[0]
        t_hi = tok.astype(BF16)
        t_lo = (tok - t_hi.astype(F32)).astype(BF16)
        lg_ref[sl, :] = (jnp.dot(t_hi, wr_hi, preferred_element_type=F32)
                         + jnp.dot(t_lo, wr_hi, preferred_element_type=F32)
                         + jnp.dot(t_hi, wr_lo, preferred_element_type=F32) + br_ref[...])
        return carry
    lax.fori_loop(0, tm // 128, rows, 0)


def _merge(mix, x2d, ln_g, ln_b, g1, ln1_g, ln1_b, sc2, sh2, w_o, w_r, b_r, seq):
    t, dm = x2d.shape
    tm = 512
    per_b = seq // tm
    row = lambda i: (i, 0)
    mod = lambda i: (i // per_b, 0, 0)
    return pl.pallas_call(
        functools.partial(_merge_body, tm=tm),
        grid=(t // tm,),
        scratch_shapes=[pltpu.VMEM((tm, dm), F32)],
        in_specs=[pl.BlockSpec((tm, dm), row), pl.BlockSpec((tm, dm), row),
                  _const_spec((1, dm)), _const_spec((1, dm)),
                  pl.BlockSpec((1, 1, dm), mod),
                  _const_spec((1, dm)), _const_spec((1, dm)),
                  pl.BlockSpec((1, 1, dm), mod), pl.BlockSpec((1, 1, dm), mod),
                  _const_spec((dm, dm)), _const_spec((dm, LANES)), _const_spec((1, LANES))],
        out_specs=[pl.BlockSpec((tm, dm), row), pl.BlockSpec((tm, LANES), row)],
        out_shape=[jax.ShapeDtypeStruct((t, dm), F32), jax.ShapeDtypeStruct((t, LANES), F32)],
        compiler_params=_cparams(("parallel",), 56),
        name="merge",
    )(mix, x2d, ln_g, ln_b, g1, ln1_g, ln1_b, sc2, sh2, w_o, w_r, b_r)


def _route_body(lg_ref, w_ref, d_ref, cnt_ref, run_s, tot_s, *, tr):
    phase = pl.program_id(0)

    @pl.when(pl.program_id(1) == 0)
    def _():
        @pl.when(phase == 1)
        def _():
            tot_s[...] = run_s[...]
        run_s[...] = jnp.zeros_like(run_s)

    lane = lax.broadcasted_iota(I32, (tr, LANES), 1)
    lane_f = lane.astype(F32)
    logit = lg_ref[...]
    hot, val = [], []
    for _ in range(TOP_K):
        mk = jnp.max(logit, axis=-1, keepdims=True)
        ik = jnp.min(jnp.where(logit == mk, lane_f, float(LANES)), axis=-1, keepdims=True)
        hk = lane_f == ik
        logit = jnp.where(hk, -jnp.inf, logit)
        hot.append(hk)
        val.append(mk)
    cnt = jnp.zeros((tr, LANES), F32)
    for hk in hot:
        cnt = cnt + jnp.where(hk, 1.0, 0.0)
    run = run_s[...] + jnp.sum(cnt, axis=0, keepdims=True)

    @pl.when(phase == 0)
    def _():
        cnt_ref[...] = run.astype(I32)

    @pl.when(phase == 1)
    def _():
        total = tot_s[...]
        padded = jnp.floor((total + (MOE_BLOCK - 1.0)) * (1.0 / MOE_BLOCK)) * MOE_BLOCK
        lane8 = lax.broadcasted_iota(I32, (8, LANES), 1)
        incl = padded
        sft = 1
        while sft < LANES:
            incl = incl + jnp.where(lane8 >= sft, pltpu.roll(incl, sft, 1), 0.0)
            sft *= 2
        pstart = (incl - padded)[0:1, :]
        lower = (lax.broadcasted_iota(I32, (tr, tr), 0) > lax.broadcasted_iota(I32, (tr, tr), 1))
        before = jnp.dot(jnp.where(lower, 1.0, 0.0).astype(BF16), cnt.astype(BF16),
                         preferred_element_type=F32) + (run_s[0:1, :] + pstart)
        ex = [jnp.exp(v - val[0]) for v in val]
        denom = ex[0] + ex[1] + ex[2] + ex[3]
        w_out = jnp.zeros((tr, LANES), F32)
        d_out = jnp.zeros((tr, LANES), I32)
        for k in range(TOP_K):
            dest = jnp.sum(jnp.where(hot[k], before, 0.0), axis=-1, keepdims=True)
            w_out = jnp.where(lane == k, ex[k] / denom, w_out)
            d_out = jnp.where(lane == k, dest.astype(I32), d_out)
        w_ref[...] = w_out
        d_ref[...] = d_out[:, :TOP_K]
        cnt_ref[...] = total.astype(I32)

    run_s[...] = run


def _route(logits):
    t = logits.shape[0]
    tr = 512
    row = lambda p, i: (i, 0)
    out_row = lambda p, i: (i * p, 0)
    return pl.pallas_call(
        functools.partial(_route_body, tr=tr),
        grid=(2, t // tr),
        in_specs=[pl.BlockSpec((tr, LANES), row)],
        out_specs=[pl.BlockSpec((tr, LANES), out_row), pl.BlockSpec((tr, TOP_K), out_row),
                   pl.BlockSpec((8, LANES), lambda p, i: (0, 0))],
        out_shape=[jax.ShapeDtypeStruct((t, LANES), F32), jax.ShapeDtypeStruct((t, TOP_K), I32),
                   jax.ShapeDtypeStruct((8, LANES), I32)],
        scratch_shapes=[pltpu.VMEM((8, LANES), F32), pltpu.VMEM((8, LANES), F32)],
        compiler_params=_cparams(("arbitrary", "arbitrary"), 32),
        name="route",
    )(logits)


def _scatter_body(cnt_ref, pstart_ref, used_ref, dest_ref, x1_ref, sc_ref, sh_ref, xb_ref, tok_ref, zero_s, sem, pad_sem,
                  *, ts, n_blocks):
    def row_copy(src, r_src, r_dst, s):
        return pltpu.make_async_copy(src.at[pl.ds(r_src, 1)], xb_ref.at[pl.ds(r_dst, 1)], s)

    def block_copy(blk):
        return pltpu.make_async_copy(zero_s, xb_ref.at[pl.ds(pl.multiple_of(blk * MOE_BLOCK, MOE_BLOCK), MOE_BLOCK)],
                                     pad_sem)

    @pl.when(pl.program_id(0) == 0)
    def _():
        zero_s[...] = jnp.zeros_like(zero_s)

        def last_block(e):
            return pstart_ref[e] // MOE_BLOCK + cnt_ref[e] // MOE_BLOCK

        def pad_start(e, c):
            @pl.when(cnt_ref[e] % MOE_BLOCK != 0)
            def _():
                block_copy(last_block(e)).start()
            return c

        def pad_wait(e, c):
            @pl.when(cnt_ref[e] % MOE_BLOCK != 0)
            def _():
                block_copy(last_block(e)).wait()
            return c
        lax.fori_loop(0, N_EXPERTS, pad_start, 0)
        lax.fori_loop(0, N_EXPERTS, pad_wait, 0)

        def tail_start(blk, c):
            block_copy(blk).start()
            return c

        def tail_wait(blk, c):
            block_copy(blk).wait()
            return c
        lax.fori_loop(used_ref[0], n_blocks, tail_start, 0)
        lax.fori_loop(used_ref[0], n_blocks, tail_wait, 0)

    tok_ref[...] = x1_ref[...] * (1.0 + sc_ref[0]) + sh_ref[0]

    def row(r, carry):
        for k in range(TOP_K):
            row_copy(tok_ref, r, dest_ref[r * TOP_K + k], sem).start()
        return carry
    lax.fori_loop(0, ts, row, 0, unroll=8)
    for _ in range(TOP_K):
        pltpu.make_async_copy(tok_ref, xb_ref.at[pl.ds(0, ts)], sem).wait()


def _scatter(x1, sc2, sh2, dest_flat, counts, pstart, used, n_blocks, seq):
    t, dm = x1.shape
    ts = 256
    per_b = seq // ts
    grid_spec = pltpu.PrefetchScalarGridSpec(
        num_scalar_prefetch=3,
        grid=(t // ts,),
        in_specs=[pl.BlockSpec((ts * TOP_K,), lambda i, *_: (i,), memory_space=pltpu.SMEM),
                  pl.BlockSpec((ts, dm), lambda i, *_: (i, 0)),
                  pl.BlockSpec((1, 1, dm), lambda i, *_: (i // per_b, 0, 0)),
                  pl.BlockSpec((1, 1, dm), lambda i, *_: (i // per_b, 0, 0))],
        out_specs=pl.BlockSpec(memory_space=pl.ANY),
        scratch_shapes=[pltpu.VMEM((ts, dm), F32), pltpu.VMEM((MOE_BLOCK, dm), F32),
                        pltpu.SemaphoreType.DMA(()), pltpu.SemaphoreType.DMA(())],
    )
    return pl.pallas_call(
        functools.partial(_scatter_body, ts=ts, n_blocks=n_blocks),
        grid_spec=grid_spec,
        out_shape=jax.ShapeDtypeStruct((n_blocks * MOE_BLOCK, dm), F32),
        compiler_params=_cparams(("arbitrary",), 32),
        name="scatter",
    )(counts, pstart, used, dest_flat, x1, sc2, sh2)


def _ffn1_body(e_ref, c_ref, blk_ref, oblk_ref, oc_ref, first_ref, n_ref,
               x_ref, wg_ref, wu_ref, bg_ref, bu_ref, a_ref, wg_s, wu_s):
    s = pl.program_id(0)

    @pl.when(first_ref[s] == 1)
    def _():
        wg_s[...] = wg_ref[0].astype(BF16)
        wu_s[...] = wu_ref[0].astype(BF16)

    @pl.when(s < n_ref[0])
    def _():
        x = x_ref[...].astype(BF16)
        g = jnp.minimum(jnp.dot(x, wg_s[...], preferred_element_type=F32) + bg_ref[0], SWIGLU_LIMIT)
        u = jnp.clip(jnp.dot(x, wu_s[...], preferred_element_type=F32) + bu_ref[0], -SWIGLU_LIMIT, SWIGLU_LIMIT)
        a_ref[...] = (g * _sigmoid(SWIGLU_ALPHA * g) * (u + 1.0)).astype(BF16)

    @pl.when(s >= n_ref[0])
    def _():
        a_ref[...] = jnp.zeros_like(a_ref)


def _ffn2_body(e_ref, c_ref, blk_ref, oblk_ref, oc_ref, first_ref, n_ref, a_ref, wd_ref, bd_ref, y_ref, wd_s):
    s = pl.program_id(0)

    @pl.when(first_ref[s] == 1)
    def _():
        wd_s[...] = wd_ref[0].astype(BF16)

    @pl.when(s < n_ref[0])
    def _():
        y_ref[...] = jnp.dot(a_ref[...], wd_s[...], preferred_element_type=F32) + bd_ref[0]

    @pl.when(s >= n_ref[0])
    def _():
        y_ref[...] = jnp.zeros_like(y_ref)


def _ffn1(plan, xb, w_gate, w_up, b_gate, b_up, n_steps):
    rows, dm = xb.shape
    de = w_gate.shape[2]
    wspec = pl.BlockSpec((1, dm, FF_CHUNK), lambda s, e, c, *_: (e[s], 0, c[s]))
    bspec = pl.BlockSpec((1, 1, FF_CHUNK), lambda s, e, c, *_: (e[s], 0, c[s]))
    grid_spec = pltpu.PrefetchScalarGridSpec(
        num_scalar_prefetch=7,
        grid=(n_steps,),
        in_specs=[pl.BlockSpec((MOE_BLOCK, dm), lambda s, e, c, blk, *_: (blk[s], 0)), wspec, wspec, bspec, bspec],
        out_specs=pl.BlockSpec((MOE_BLOCK, FF_CHUNK), lambda s, e, c, blk, oblk, oc, *_: (oblk[s], oc[s])),
        scratch_shapes=[pltpu.VMEM((dm, FF_CHUNK), BF16), pltpu.VMEM((dm, FF_CHUNK), BF16)],
    )
    return pl.pallas_call(
        _ffn1_body, grid_spec=grid_spec,
        out_shape=jax.ShapeDtypeStruct((rows, de), BF16),
        compiler_params=_cparams(("arbitrary",), 60),
        name="ffn1",
    )(*plan, xb, w_gate, w_up, b_gate, b_up)


def _ffn2(plan, act, w_down, b_down, n_steps, chunk):
    rows, de = act.shape
    dm = w_down.shape[2]
    grid_spec = pltpu.PrefetchScalarGridSpec(
        num_scalar_prefetch=7,
        grid=(n_steps,),
        in_specs=[pl.BlockSpec((MOE_BLOCK, de), lambda s, e, c, blk, *_: (blk[s], 0)),
                  pl.BlockSpec((1, de, chunk), lambda s, e, c, *_: (e[s], 0, c[s])),
                  pl.BlockSpec((1, 1, chunk), lambda s, e, c, *_: (e[s], 0, c[s]))],
        out_specs=pl.BlockSpec((MOE_BLOCK, chunk), lambda s, e, c, blk, oblk, oc, *_: (oblk[s], oc[s])),
        scratch_shapes=[pltpu.VMEM((de, chunk), BF16)],
    )
    return pl.pallas_call(
        _ffn2_body, grid_spec=grid_spec,
        out_shape=jax.ShapeDtypeStruct((rows, dm), F32),
        compiler_params=_cparams(("arbitrary",), 60),
        name="ffn2",
    )(*plan, act, w_down, b_down)


def _combine_body(dcur_ref, dnxt_ref, w_ref, x1_ref, g2_ref, lg_ref, lb_ref, yb_ref, o_ref, buf, sem, *, tc, n_tiles):
    i = pl.program_id(0)

    def issue(d_ref, slot):
        def row(r, carry):
            for k in range(TOP_K):
                pltpu.make_async_copy(yb_ref.at[pl.ds(d_ref[r * TOP_K + k], 1)],
                                      buf.at[slot, k, pl.ds(r, 1)], sem.at[slot]).start()
            return carry
        lax.fori_loop(0, tc, row, 0, unroll=8)

    @pl.when(i == 0)
    def _():
        issue(dcur_ref, 0)

    @pl.when(i + 1 < n_tiles)
    def _():
        issue(dnxt_ref, (i + 1) % 2)

    slot = i % 2
    for k in range(TOP_K):
        pltpu.make_async_copy(yb_ref.at[pl.ds(0, tc)], buf.at[slot, k], sem.at[slot]).wait()
    w = w_ref[...]
    y = w[:, 0:1] * buf[slot, 0]
    for k in range(1, TOP_K):
        y = y + w[:, k:k + 1] * buf[slot, k]
    o_ref[...] = _layer_norm(DEEPNORM_ALPHA * x1_ref[...] + g2_ref[0] * y, lg_ref[...], lb_ref[...])


def _combine(dest_flat, w4, x1, g2, ln_g, ln_b, yb, seq):
    t, dm = x1.shape
    tc = 256
    n_tiles = t // tc
    per_b = seq // tc
    row = lambda i: (i, 0)
    return pl.pallas_call(
        functools.partial(_combine_body, tc=tc, n_tiles=n_tiles),
        grid=(n_tiles,),
        in_specs=[pl.BlockSpec((tc * TOP_K,), lambda i: (i,), memory_space=pltpu.SMEM),
                  pl.BlockSpec((tc * TOP_K,), lambda i: (jnp.minimum(i + 1, n_tiles - 1),), memory_space=pltpu.SMEM),
                  pl.BlockSpec((tc, LANES), row),
                  pl.BlockSpec((tc, dm), row),
                  pl.BlockSpec((1, 1, dm), lambda i: (i // per_b, 0, 0)),
                  _const_spec((1, dm)), _const_spec((1, dm)),
                  pl.BlockSpec(memory_space=pl.ANY)],
        out_specs=pl.BlockSpec((tc, dm), row),
        out_shape=jax.ShapeDtypeStruct((t, dm), F32),
        scratch_shapes=[pltpu.VMEM((2, TOP_K, tc, dm), F32), pltpu.SemaphoreType.DMA((2,))],
        compiler_params=_cparams(("arbitrary",), 40),
        name="combine",
    )(dest_flat, dest_flat, w4, x1, g2, ln_g, ln_b, yb)


def _dft_tables(seq):
    n = 2 * seq
    j = np.arange(seq // 2, dtype=np.int64)
    tables = []
    for k in (2 * j, 2 * j + 1):
        ang = (2.0 * np.pi / n) * ((k[:, None] * j[None, :]) % n).astype(np.float64)
        tables += [np.cos(ang), np.sin(ang)]
    tables += [tables[2].T, tables[3].T]
    return tuple(jnp.asarray(m, F32).astype(BF16) for m in tables)


def _filter_features(seq):
    t = jnp.linspace(0.0, 1.0, seq, dtype=F32)[:, None]
    bands = (HYENA_EMB - 1) // 2
    f = jnp.linspace(1e-4, bands - 1, bands, dtype=F32)[None, :]
    ang = 2.0 * math.pi * jnp.arange(seq, dtype=F32)[:, None] * f / seq
    feats = jnp.concatenate([t, jnp.cos(ang), -jnp.sin(ang)], axis=-1)
    max_decay = math.log(HYENA_DECAY_TARGET) / HYENA_SHORT_DECAY_PCT
    min_decay = math.log(HYENA_DECAY_TARGET) / HYENA_LONG_DECAY_PCT
    deltas = jnp.abs(jnp.linspace(min_decay, max_decay, HYENA_WIDTH, dtype=F32))[None, :]
    return t, feats, deltas


def _moe_plan(counts, n_blocks, n_chunks):
    ids = jnp.arange(N_EXPERTS, dtype=I32)
    nblk = (counts + MOE_BLOCK - 1) // MOE_BLOCK
    blk_end = jnp.cumsum(nblk)
    blk_start = blk_end - nblk
    used = blk_end[-1]
    steps = n_chunks * nblk
    step_end = jnp.cumsum(steps)
    n_used = step_end[-1]
    s_all = jnp.arange(n_chunks * n_blocks, dtype=I32)
    s = jnp.minimum(s_all, n_used - 1)
    e_s = jnp.minimum(jnp.sum((s[:, None] >= step_end[None, :]).astype(I32), axis=1), N_EXPERTS - 1)
    onehot = e_s[:, None] == ids[None, :]
    pick = lambda table: jnp.sum(jnp.where(onehot, table[None, :], 0), axis=1)
    loc = s - pick(step_end - steps)
    nb = jnp.maximum(pick(nblk), 1)
    c_s = loc // nb
    r_s = loc % nb
    blk = pick(blk_start) + r_s
    tail = s_all >= n_used
    j = jnp.maximum(s_all - n_used, 0)
    n_tail = jnp.maximum(n_blocks - used, 1)
    oblk = jnp.where(tail, used + j % n_tail, blk)
    oc = jnp.where(tail, j // n_tail, c_s)
    first = jnp.logical_and(r_s == 0, jnp.logical_not(tail))
    as_i32 = lambda v: v.astype(I32)
    plan = tuple(map(as_i32, (e_s, c_s, blk, oblk, oc, first, n_used.reshape(1))))
    return as_i32(blk_start * MOE_BLOCK), as_i32(used.reshape(1)), plan


def kernel(x, c, ctx, c_ctx, ln_in_g, ln_in_b, w_mod, b_mod, w_in, b_in, mlstm_conv_w, mlstm_conv_b,
           w_qh, w_kh, hyena_conv_w, hyena_conv_b, filt_w1, filt_b1, filt_wh, filt_bh, filt_freq, filt_wout,
           hyena_skip, w_proj_a, w_proj_h, w_out, ln1_g, ln1_b, w_router, b_router, w_gate, b_gate,
           w_up, b_up, w_down, b_down, ln2_g, ln2_b):
    bsz, seq, dm = x.shape
    ctx_len = ctx.shape[1]
    t = bsz * seq
    assert w_mod.shape[0] == DEPTH and dm == D_MODEL and ctx_len == CHUNK and bsz + 1 <= 16
    row = lambda v: v.reshape(1, -1)

    cond = jnp.concatenate([c, c_ctx[None], jnp.zeros((16 - bsz - 1, dm), F32)], axis=0)
    mod = _mod(cond, w_mod[0], row(b_mod[0]))
    sh1, sc1, g1, sh2, sc2, g2 = [m[:, None, :] for m in jnp.split(mod, 6, axis=-1)]

    w_a, w_b = w_in[0][:, :IN_GATES].astype(BF16), w_in[0][:, IN_O:].astype(BF16)
    b_a, b_b = row(b_in[0][:IN_GATES]), row(b_in[0][IN_O:])
    w_g = jnp.pad(w_in[0][:, IN_GATES:IN_O], ((0, 0), (0, LANES - 4 * HEADS))).astype(BF16)
    b_g = row(jnp.pad(b_in[0][IN_GATES:IN_O], (0, LANES - 4 * HEADS)))
    lng, lnb = row(ln_in_g), row(ln_in_b)
    x2d = x.reshape(t, dm)
    z, gates, sg = _in_proj(x2d, lng, lnb, sc1[:bsz], sh1[:bsz], w_a, b_a, w_b, b_b, w_g, b_g, seq,
                            IN_GATES + w_b.shape[1], Z_BG)
    zc, gates_c = _in_proj(ctx.reshape(bsz * ctx_len, dm), lng, lnb, sc1[bsz:bsz + 1], sh1[bsz:bsz + 1],
                           w_a, b_a, w_b, b_b, w_g, b_g, bsz * ctx_len, IN_GATES, IN_GATES)

    g_all = jnp.concatenate([gates_c[:, :4 * HEADS].reshape(bsz, ctx_len, 4, HEADS),
                             gates[:, :4 * HEADS].reshape(bsz, seq, 4, HEADS)], axis=1)
    gates_t = g_all.transpose(0, 3, 2, 1)

    oh = _mlstm(z, zc, gates_t, mlstm_conv_w[0].reshape(9, MLSTM_WIDTH), row(mlstm_conv_b[0]),
                w_qh[0], w_kh[0].transpose(0, 2, 1), bsz, seq, ctx_len)

    tcol, feats, deltas = _filter_features(seq)
    feats = jnp.pad(feats, ((0, 0), (0, LANES - HYENA_EMB)))
    w1 = jnp.pad(filt_w1[0], ((0, LANES - HYENA_EMB), (0, 0)))
    a = _filt(feats, w1, row(filt_b1[0]), filt_wh[0], filt_bh[0], filt_freq[0])
    fh = a.shape[1]
    w_fout4 = filt_wout[0].reshape(fh, 4, HYENA_WIDTH).transpose(1, 0, 2)
    hh = _hyena(z, hyena_conv_w[0].reshape(9, 3 * HYENA_WIDTH), row(hyena_conv_b[0]), a, w_fout4, deltas, tcol,
                hyena_skip[0], _dft_tables(seq), bsz, seq)

    w_r = jnp.pad(w_router[0], ((0, 0), (0, LANES - N_EXPERTS)))
    b_r = row(jnp.pad(b_router[0], (0, LANES - N_EXPERTS), constant_values=-1e30))
    mix = _mix(oh, hh, sg, w_proj_a[0].astype(BF16), w_proj_h[0].astype(BF16))
    x1, logits = _merge(mix, x2d, lng, lnb, g1[:bsz], row(ln1_g[0]), row(ln1_b[0]), sc2[:bsz], sh2[:bsz],
                        w_out[0].astype(BF16), w_r, b_r, seq)

    w4, dest, counts = _route(logits)
    n_blocks = -(-(t * TOP_K + N_EXPERTS * (MOE_BLOCK - 1)) // MOE_BLOCK)
    n_chunks = dm // FF_CHUNK
    counts = counts[0, :N_EXPERTS]
    pstart, used, plan = _moe_plan(counts, n_blocks, n_chunks)
    dest_flat = dest.reshape(t * TOP_K)
    xb = _scatter(x1, sc2[:bsz], sh2[:bsz], dest_flat, counts, pstart, used, n_blocks, seq)
    act = _ffn1(plan, xb, w_gate[0], w_up[0], b_gate[0][:, None, :], b_up[0][:, None, :], n_chunks * n_blocks)
    _, _, plan_down = _moe_plan(counts, n_blocks, 1)
    yb = _ffn2(plan_down, act, w_down[0], b_down[0][:, None, :], n_blocks, dm)
    out = _combine(dest_flat, w4, x1, g2[:bsz], row(ln2_g[0]), row(ln2_b[0]), yb, seq)
    return out.reshape(bsz, seq, dm)
```

```python
import functools
import math

import numpy as np
import jax
import jax.numpy as jnp
from jax import lax
from jax.experimental import pallas as pl
from jax.experimental.pallas import tpu as pltpu

F32 = jnp.float32
BF16 = jnp.bfloat16
I32 = jnp.int32
HIGHEST = lax.Precision.HIGHEST

D_MODEL = 2048
GRID_W = 64
HEADS = 4
HEAD_DIM = 256
MLSTM_WIDTH = HEADS * HEAD_DIM
HYENA_WIDTH = D_MODEL // 2
HYENA_EMB = 33
HYENA_DECAY_TARGET = 1e-2
HYENA_SHORT_DECAY_PCT = 0.3
HYENA_LONG_DECAY_PCT = 1.5
N_EXPERTS = 32
TOP_K = 4
SWIGLU_LIMIT = 7.0
SWIGLU_ALPHA = 1.702
LN_EPS = 1e-5
DEPTH = 1
DEEPNORM_ALPHA = (2.0 * DEPTH) ** 0.25

IN_V = MLSTM_WIDTH
IN_GATES = 2 * MLSTM_WIDTH
IN_O = IN_GATES + 4 * HEADS
Z_QK, Z_V, Z_O, Z_HY, Z_BG = 0, 1024, 2048, 3072, 6144

LANES = 128
MXU = 256
CHUNK = 256
MOE_BLOCK = 512
FF_CHUNK = 1024
MIB = 1024 * 1024


def _cparams(semantics, vmem_mib):
    return pltpu.CompilerParams(dimension_semantics=semantics, vmem_limit_bytes=vmem_mib * MIB)


def _const_spec(shape):
    nd = len(shape)
    return pl.BlockSpec(shape, lambda *_: (0,) * nd, pipeline_mode=pl.Buffered(1))


def _layer_norm(x, g, b):
    mu = jnp.mean(x, axis=-1, keepdims=True)
    xc = x - mu
    var = jnp.mean(xc * xc, axis=-1, keepdims=True)
    return xc * lax.rsqrt(var + LN_EPS) * g + b


def _sigmoid(x):
    return 1.0 / (1.0 + jnp.exp(-x))


def _silu(x):
    return x * _sigmoid(x)


def _log_sigmoid(x):
    return jnp.minimum(x, 0.0) - jnp.log(1.0 + jnp.exp(-jnp.abs(x)))


def _mod_body(c_ref, w_ref, b_ref, o_ref):
    s = _silu(c_ref[...])
    o_ref[...] = jnp.dot(s.astype(BF16), w_ref[...].astype(BF16), preferred_element_type=F32) + b_ref[...]


def _mod(cond, w, b):
    rows, dm = cond.shape
    n = w.shape[1]
    tn = 1024
    return pl.pallas_call(
        _mod_body,
        grid=(n // tn,),
        in_specs=[pl.BlockSpec((rows, dm), lambda j: (0, 0)),
                  pl.BlockSpec((dm, tn), lambda j: (0, j)),
                  pl.BlockSpec((1, tn), lambda j: (0, j))],
        out_specs=pl.BlockSpec((rows, tn), lambda j: (0, j)),
        out_shape=jax.ShapeDtypeStruct((rows, n), F32),
        compiler_params=_cparams(("arbitrary",), 40),
        name="mod",
    )(cond, w, b)


def _in_proj_body(x_ref, lng_ref, lnb_ref, sc_ref, sh_ref, w_ref, b_ref, wg_ref, bg_ref,
                  z_ref, g_ref, *rest, tm, n_plain):
    hx_s = rest[-1]

    @pl.when(pl.program_id(1) == 0)
    def _():
        def rows(r, carry):
            sl = pl.ds(pl.multiple_of(r * 128, 128), 128)
            xn = _layer_norm(x_ref[sl, :], lng_ref[...], lnb_ref[...])
            hx_s[sl, :] = (xn * (1.0 + sc_ref[0]) + sh_ref[0]).astype(BF16)
            return carry
        lax.fori_loop(0, tm // 128, rows, 0)
        g_ref[...] = jnp.dot(hx_s[...], wg_ref[...], preferred_element_type=F32) + bg_ref[...]

    acc = jnp.dot(hx_s[...], w_ref[...], preferred_element_type=F32) + b_ref[...]
    if len(rest) == 1:
        z_ref[...] = acc
    else:
        @pl.when(pl.program_id(1) < n_plain)
        def _():
            z_ref[...] = acc

        @pl.when(pl.program_id(1) >= n_plain)
        def _():
            rest[0][...] = _sigmoid(acc).astype(BF16)


def _in_proj(x2d, ln_g, ln_b, scale, shift, w, b, w_gates, b_gates, rows_per_mod, plain_cols):
    t, dm = x2d.shape
    n = w.shape[1]
    tm, tn = min(1024, t), 1024
    n_plain = plain_cols // tn
    out_specs = [pl.BlockSpec((tm, tn), lambda i, j: (i, jnp.minimum(j, n_plain - 1))),
                 pl.BlockSpec((tm, LANES), lambda i, j: (i, 0))]
    out_shape = [jax.ShapeDtypeStruct((t, plain_cols), F32), jax.ShapeDtypeStruct((t, LANES), F32)]
    if n > plain_cols:
        out_specs.append(pl.BlockSpec((tm, tn), lambda i, j: (i, jnp.maximum(j - n_plain, 0))))
        out_shape.append(jax.ShapeDtypeStruct((t, n - plain_cols), BF16))
    return pl.pallas_call(
        functools.partial(_in_proj_body, tm=tm, n_plain=n_plain),
        grid=(t // tm, n // tn),
        in_specs=[pl.BlockSpec((tm, dm), lambda i, j: (i, 0)),
                  pl.BlockSpec((1, dm), lambda i, j: (0, 0)),
                  pl.BlockSpec((1, dm), lambda i, j: (0, 0)),
                  pl.BlockSpec((1, 1, dm), lambda i, j: (i * tm // rows_per_mod, 0, 0)),
                  pl.BlockSpec((1, 1, dm), lambda i, j: (i * tm // rows_per_mod, 0, 0)),
                  pl.BlockSpec((dm, tn), lambda i, j: (0, j)),
                  pl.BlockSpec((1, tn), lambda i, j: (0, j)),
                  pl.BlockSpec((dm, LANES), lambda i, j: (0, 0)),
                  pl.BlockSpec((1, LANES), lambda i, j: (0, 0))],
        out_specs=out_specs,
        out_shape=out_shape,
        scratch_shapes=[pltpu.VMEM((tm, dm), BF16)],
        compiler_params=_cparams(("parallel", "arbitrary"), 48),
        name="in_proj",
    )(x2d, ln_g, ln_b, scale, shift, w, b, w_gates, b_gates)


def _dwconv(u, w9, bias, width, single_row):
    length, ch = u.shape
    col = lax.broadcasted_iota(I32, (length, ch), 0) % width
    if not single_row:
        zpad = jnp.zeros((width, ch), F32)
        up = jnp.concatenate([zpad, u[:length - width]], axis=0)
        dn = jnp.concatenate([u[width:], zpad], axis=0)
    out = None
    for dc in (-1, 0, 1):
        a = u * w9[4 + dc:5 + dc]
        if not single_row:
            a = a + up * w9[1 + dc:2 + dc] + dn * w9[7 + dc:8 + dc]
        if dc == -1:
            a = jnp.where(col == 0, 0.0, pltpu.roll(a, 1, 0))
        elif dc == 1:
            a = jnp.where(col == width - 1, 0.0, pltpu.roll(a, length - 1, 0))
        out = a if out is None else out + a
    return out + bias


def _mlstm_body(zqk_ref, zv_ref, zo_ref, cqk_ref, cv_ref, gt_ref, cw_ref, cb_ref, wq_ref, wkt_ref,
                o_ref, q_s, k_s, kt_s, v_s, hf_s, hb_s, r_s, c_s, ct_s, *, ctx_len, seq):
    n_chunks = (ctx_len + seq) // CHUNK
    total = ctx_len + seq
    cw = cw_ref[...]
    cb = cb_ref[...]
    wq = wq_ref[0].astype(BF16)
    wkt = wkt_ref[0].astype(BF16)
    nt = (((1,), (1,)), ((), ()))
    scale = HEAD_DIM ** -0.5

    def project(u, off, n):
        ub = u.astype(BF16)
        q_s[off:off + n, :] = jnp.dot(ub, wq, preferred_element_type=F32).astype(BF16)
        k_s[off:off + n, :] = (lax.dot_general(ub, wkt, nt, preferred_element_type=F32) * scale).astype(BF16)
        kt_s[:, off:off + n] = (lax.dot_general(wkt, ub, nt, preferred_element_type=F32) * scale).astype(BF16)

    project(_silu(_dwconv(cqk_ref[...], cw, cb, ctx_len, True)), 0, ctx_len)
    project(_silu(_dwconv(zqk_ref[...], cw, cb, GRID_W, False)), ctx_len, seq)
    v_s[0:ctx_len, :] = cv_ref[...]
    v_s[ctx_len:total, :] = zv_ref[...]

    gt = gt_ref[0, 0]
    lf = _log_sigmoid(gt)
    pos = lax.broadcasted_iota(I32, (4, total), 1) % CHUNK
    pre, suf = lf, lf
    s = 1
    while s < CHUNK:
        pre = pre + jnp.where(pos >= s, pltpu.roll(pre, s, 1), 0.0)
        suf = suf + jnp.where(pos < CHUNK - s, pltpu.roll(suf, total - s, 1), 0.0)
        s *= 2
    b_f, li_f, b_b, li_b = pre[1:2], gt[0:1], suf[3:4], gt[2:3]
    pm_f, pm_b = li_f - b_f, li_b - b_b
    pos1 = pos[0:1]
    s = 1
    while s < CHUNK:
        pm_f = jnp.maximum(pm_f, jnp.where(pos1 >= s, pltpu.roll(pm_f, s, 1), -jnp.inf))
        pm_b = jnp.maximum(pm_b, jnp.where(pos1 < CHUNK - s, pltpu.roll(pm_b, total - s, 1), -jnp.inf))
        s *= 2
    r_s[...] = jnp.concatenate([b_f, li_f, pm_f, b_b, li_b, pm_b, jnp.zeros((2, total), F32)], axis=0)
    zfill = jnp.zeros((CHUNK - 8, CHUNK), F32)
    for c in range(n_chunks):
        blk = jnp.concatenate([r_s[:, c * CHUNK:(c + 1) * CHUNK], zfill], axis=0).T
        c_s[c * CHUNK:(c + 1) * CHUNK, :] = blk[:, :LANES]

    ct_s[...] = jnp.zeros_like(ct_s)
    row_i = lax.broadcasted_iota(I32, (CHUNK, CHUNK), 0)
    col_i = lax.broadcasted_iota(I32, (CHUNK, CHUNK), 1)
    masks = (row_i >= col_i, row_i <= col_i)

    def chunk_step(c, d, n_vec, m, need_h=True):
        off = c * CHUNK if isinstance(c, int) else pl.multiple_of(c * CHUNK, CHUNK)
        rows = r_s[:, pl.ds(off, CHUNK)]
        cols = c_s[pl.ds(off, CHUNK), :]
        b_row, li_row, pm_row = rows[3 * d:3 * d + 1], rows[3 * d + 1:3 * d + 2], rows[3 * d + 2:3 * d + 3]
        b_col, li_col, pm_col = cols[:, 3 * d:3 * d + 1], cols[:, 3 * d + 1:3 * d + 2], cols[:, 3 * d + 2:3 * d + 3]
        last = slice(CHUNK - 1, CHUNK) if d == 0 else slice(0, 1)
        b_end, pm_end = b_row[:, last], pm_row[:, last]
        qc = q_s[pl.ds(off, CHUNK), :]
        kc = k_s[pl.ds(off, CHUNK), :]
        ktc = kt_s[:, pl.ds(off, CHUNK)]
        vc = v_s[pl.ds(off, CHUNK), :]
        ct = ct_s[d]

        h = None
        if need_h:
            inter = b_col + m
            m_t = jnp.maximum(inter, b_col + pm_col)
            wts = jnp.exp(jnp.where(masks[d], b_col - b_row + li_row, -jnp.inf) - m_t)
            s_inter = jnp.exp(inter - m_t)
            scores = jnp.dot(qc, ktc, preferred_element_type=F32) * wts
            num = (s_inter * jnp.dot(qc, ct.astype(BF16), preferred_element_type=F32)
                   + jnp.dot(scores.astype(BF16), vc.astype(BF16), preferred_element_type=F32))
            den = (s_inter * jnp.sum(qc.astype(F32) * n_vec, axis=-1, keepdims=True)
                   + jnp.sum(scores, axis=-1, keepdims=True))
            h = num / jnp.maximum(jnp.abs(den), jnp.exp(-m_t))

        m_new = jnp.maximum(b_end + m, b_end + pm_end)
        decay = jnp.exp(b_end + m - m_new)
        w = jnp.exp(b_end - b_col + li_col - m_new)
        ct_s[d] = decay * ct + jnp.dot(ktc, (vc * w).astype(BF16), preferred_element_type=F32)
        n_new = decay * n_vec + jnp.sum(kc.astype(F32) * w, axis=0, keepdims=True)
        return h, n_new, m_new

    n0 = jnp.zeros((1, HEAD_DIM), F32)
    m0 = jnp.zeros((1, 1), F32)
    _, nf, mf = chunk_step(0, 0, n0, m0, need_h=False)
    _, nb, mb = chunk_step(0, 1, n0, m0, need_h=False)

    def body(i, carry):
        nf, mf, nb, mb = carry
        hf, nf, mf = chunk_step(i, 0, nf, mf)
        hf_s[pl.ds(pl.multiple_of(i * CHUNK - ctx_len, CHUNK), CHUNK), :] = hf
        j = n_chunks - i
        hb, nb, mb = chunk_step(j, 1, nb, mb)
        hb_s[pl.ds(pl.multiple_of(j * CHUNK - ctx_len, CHUNK), CHUNK), :] = hb
        return nf, mf, nb, mb

    lax.fori_loop(1, n_chunks, body, (nf, mf, nb, mb))
    o_ref[...] = (_sigmoid(zo_ref[...]) * (hf_s[...] + hb_s[...])).astype(BF16)


def _mlstm(z, zc, gates_t, conv_w9, conv_b, w_qh, w_kh_t, bsz, seq, ctx_len):
    total = ctx_len + seq
    hd = HEAD_DIM
    qk_blk, v_blk, o_blk = Z_QK // hd, Z_V // hd, Z_O // hd
    return pl.pallas_call(
        functools.partial(_mlstm_body, ctx_len=ctx_len, seq=seq),
        grid=(bsz, HEADS),
        in_specs=[pl.BlockSpec((seq, hd), lambda b, h: (b, qk_blk + h)),
                  pl.BlockSpec((seq, hd), lambda b, h: (b, v_blk + h)),
                  pl.BlockSpec((seq, hd), lambda b, h: (b, o_blk + h)),
                  pl.BlockSpec((ctx_len, hd), lambda b, h: (b, qk_blk + h)),
                  pl.BlockSpec((ctx_len, hd), lambda b, h: (b, v_blk + h)),
                  pl.BlockSpec((1, 1, 4, total), lambda b, h: (b, h, 0, 0)),
                  pl.BlockSpec((9, hd), lambda b, h: (0, h)),
                  pl.BlockSpec((1, hd), lambda b, h: (0, h)),
                  pl.BlockSpec((1, hd, hd), lambda b, h: (h, 0, 0)),
                  pl.BlockSpec((1, hd, hd), lambda b, h: (h, 0, 0))],
        out_specs=pl.BlockSpec((seq, hd), lambda b, h: (b, h)),
        out_shape=jax.ShapeDtypeStruct((bsz * seq, MLSTM_WIDTH), BF16),
        scratch_shapes=[pltpu.VMEM((total, hd), BF16), pltpu.VMEM((total, hd), BF16), pltpu.VMEM((hd, total), BF16),
                        pltpu.VMEM((total, hd), F32), pltpu.VMEM((seq, hd), F32), pltpu.VMEM((seq, hd), F32),
                        pltpu.VMEM((8, total), F32), pltpu.VMEM((total, LANES), F32), pltpu.VMEM((2, hd, hd), F32)],
        compiler_params=_cparams(("parallel", "arbitrary"), 48),
        name="mlstm",
    )(z, z, z, zc, zc, gates_t, conv_w9, conv_b, w_qh, w_kh_t)


def _filt_body(z_ref, w1_ref, b1_ref, wh_ref, bh_ref, fr_ref, a_ref):
    fr = fr_ref[...]
    a = jnp.sin(fr[0:1] * (jnp.dot(z_ref[...], w1_ref[...], precision=HIGHEST, preferred_element_type=F32)
                           + b1_ref[...]))
    for i in range(2):
        a = jnp.sin(fr[i + 1:i + 2] * (jnp.dot(a, wh_ref[i], precision=HIGHEST, preferred_element_type=F32)
                                       + bh_ref[i:i + 1]))
    a_ref[...] = a


def _filt(feats, w1, b1, wh, bh, freq):
    length = feats.shape[0]
    fh = w1.shape[1]
    return pl.pallas_call(
        _filt_body,
        out_shape=jax.ShapeDtypeStruct((length, fh), F32),
        name="filt",
    )(feats, w1, b1, wh, bh, freq)


def _hyena_body(zx1_ref, zx2_ref, zv_ref, cw1_ref, cw2_ref, cwv_ref, cb1_ref, cb2_ref, cbv_ref,
                a_ref, wf_ref, dl_ref, t_ref, skip_ref, ce_ref, se_ref, co_ref, so_ref, cot_ref, sot_ref, o_ref,
                hce_s, hse_s, hco_s, hso_s, hn_s, v_s, g_s, *, seq):
    n_fft = 2 * seq
    half = seq // 2
    sign = jnp.where(lax.broadcasted_iota(I32, (seq, 1), 0) % 2 == 0, 1.0, -1.0)
    jrow = lax.broadcasted_iota(I32, (half, 1), 0)
    sgn_j = jnp.where(jrow % 2 == 0, 1.0, -1.0)
    first = jrow == 0
    flip = jnp.where(lax.broadcasted_iota(I32, (MXU, MXU), 0) + lax.broadcasted_iota(I32, (MXU, MXU), 1) == MXU - 1,
                     1.0, 0.0).astype(BF16)

    def mm(m_ref, x):
        return jnp.dot(m_ref[...], x, preferred_element_type=F32)

    def reverse_rows(x):
        hi = x.astype(BF16)
        lo = (x - hi.astype(F32)).astype(BF16)
        blocks = []
        for b in range(half // MXU):
            src = slice(half - MXU * (b + 1), half - MXU * b)
            blocks.append(jnp.dot(flip, hi[src], preferred_element_type=F32)
                          + jnp.dot(flip, lo[src], preferred_element_type=F32))
        return jnp.concatenate(blocks, axis=0)

    def fold(v):
        rolled = pltpu.roll(reverse_rows(v[half:]), 1, 0)
        mid = rolled[0:1]
        vr = jnp.where(first, 0.0, rolled)
        return v[:half] + vr, v[:half] - vr, mid

    def forward(v):
        s, d, mid = fold(v)
        sb, db = s.astype(BF16), d.astype(BF16)
        return mm(ce_ref, sb) + sgn_j * mid, mm(se_ref, db), mm(co_ref, db), mm(so_ref, sb) + sgn_j * mid

    @pl.when(pl.program_id(1) == 0)
    def _():
        window = jnp.exp(-t_ref[...] * dl_ref[...])
        row0 = lax.broadcasted_iota(I32, (seq, 1), 0) == 0
        for o in range(2):
            fwd = jnp.dot(a_ref[...], wf_ref[2 * o], precision=HIGHEST, preferred_element_type=F32) * window
            bwd = jnp.dot(a_ref[...], wf_ref[2 * o + 1], precision=HIGHEST, preferred_element_type=F32) * window
            bwd = jnp.where(row0, 0.0, bwd)
            even = fwd + bwd
            hce_s[o], _, hco_s[o], _ = forward(even)
            _, hse_s[o], _, hso_s[o] = forward(fwd - bwd)
            hn_s[o] = jnp.sum(even * sign, axis=0, keepdims=True)

    def conv_to(dst_ref, z_ref, cw_ref, cb_ref):
        for lo in range(0, z_ref.shape[1], LANES):
            ls = slice(lo, lo + LANES)
            dst_ref[:, ls] = _dwconv(z_ref[:, ls], cw_ref[:, ls], cb_ref[:, ls], GRID_W, False)

    conv_to(v_s, zv_ref, cwv_ref, cbv_ref)

    for o, (zg_ref, cwg_ref, cbg_ref) in enumerate(((zx1_ref, cw1_ref, cb1_ref), (zx2_ref, cw2_ref, cb2_ref))):
        v = v_s[...]
        nyq = jnp.sum(v * sign, axis=0, keepdims=True) * hn_s[o] * (1.0 / n_fft)
        xce, xse, xco, xso = forward(v)
        hce, hse, hco, hso = hce_s[o], hse_s[o], hco_s[o], hso_s[o]
        scale_e = jnp.where(first, 1.0 / n_fft, 2.0 / n_fft)
        zce = scale_e * (xce * hce - xse * hse)
        zse = scale_e * (xce * hse + xse * hce)
        zco = (2.0 / n_fft) * (xco * hco - xso * hso)
        zso = (2.0 / n_fft) * (xco * hso + xso * hco)
        y_mid = jnp.sum(sgn_j * (zce + zso), axis=0, keepdims=True)
        sym = mm(ce_ref, zce.astype(BF16)) + mm(sot_ref, zso.astype(BF16))
        asym = mm(cot_ref, zco.astype(BF16)) + mm(se_ref, zse.astype(BF16))
        rolled = pltpu.roll(reverse_rows(sym - asym), 1, 0)
        y = jnp.concatenate([sym + asym, jnp.where(first, y_mid, rolled)], axis=0)
        conv_to(g_s, zg_ref, cwg_ref, cbg_ref)
        v_s[...] = g_s[...] * (y + sign * nyq + v * skip_ref[o:o + 1, :])

    o_ref[...] = v_s[...].astype(BF16)


def _hyena(z, conv_w9, conv_b, a, w_fout4, deltas, tcol, skip, tables, bsz, seq):
    ct = MXU
    n_ct = HYENA_WIDTH // ct
    hy = Z_HY // ct
    half = seq // 2
    zspec = lambda off: pl.BlockSpec((seq, ct), lambda j, b: (b, hy + off * n_ct + j))
    wspec = lambda off: pl.BlockSpec((9, ct), lambda j, b: (0, off * n_ct + j))
    bspec = lambda off: pl.BlockSpec((1, ct), lambda j, b: (0, off * n_ct + j))
    fh = a.shape[1]
    return pl.pallas_call(
        functools.partial(_hyena_body, seq=seq),
        grid=(n_ct, bsz),
        in_specs=[zspec(0), zspec(1), zspec(2), wspec(0), wspec(1), wspec(2), bspec(0), bspec(1), bspec(2),
                  _const_spec((seq, fh)),
                  pl.BlockSpec((4, fh, ct), lambda j, b: (0, 0, j)),
                  pl.BlockSpec((1, ct), lambda j, b: (0, j)),
                  _const_spec((seq, 1)),
                  pl.BlockSpec((2, ct), lambda j, b: (0, j))] + [_const_spec((half, half))] * len(tables),
        out_specs=pl.BlockSpec((seq, ct), lambda j, b: (b, j)),
        out_shape=jax.ShapeDtypeStruct((bsz * seq, HYENA_WIDTH), BF16),
        scratch_shapes=[pltpu.VMEM((2, half, ct), F32)] * 4 + [pltpu.VMEM((2, 1, ct), F32),
                                                               pltpu.VMEM((seq, ct), F32), pltpu.VMEM((seq, ct), F32)],
        compiler_params=_cparams(("arbitrary", "arbitrary"), 60),
        name="hyena",
    )(z, z, z, conv_w9, conv_w9, conv_w9, conv_b, conv_b, conv_b, a, w_fout4, deltas, tcol, skip, *tables)


def _mix_body(oh_ref, hh_ref, ga_ref, gh_ref, wa_ref, wh_ref, m_ref):
    y_a = jnp.dot(oh_ref[...], wa_ref[...], preferred_element_type=F32)
    y_h = jnp.dot(hh_ref[...], wh_ref[...], preferred_element_type=F32)
    m_ref[...] = (ga_ref[...].astype(F32) * y_a + gh_ref[...].astype(F32) * y_h).astype(BF16)


def _mix(oh, hh, sg, w_a, w_h):
    t = oh.shape[0]
    dm = w_a.shape[1]
    tm = 512
    row = lambda i: (i, 0)
    return pl.pallas_call(
        _mix_body,
        grid=(t // tm,),
        in_specs=[pl.BlockSpec((tm, MLSTM_WIDTH), row), pl.BlockSpec((tm, HYENA_WIDTH), row),
                  pl.BlockSpec((tm, dm), row), pl.BlockSpec((tm, dm), lambda i: (i, 1)),
                  _const_spec((MLSTM_WIDTH, dm)), _const_spec((HYENA_WIDTH, dm))],
        out_specs=pl.BlockSpec((tm, dm), row),
        out_shape=jax.ShapeDtypeStruct((t, dm), BF16),
        compiler_params=_cparams(("parallel",), 48),
        name="mix",
    )(oh, hh, sg, sg, w_a, w_h)


def _merge_body(m_ref, x_ref, lng_ref, lnb_ref, g1_ref, l1g_ref, l1b_ref, sc2_ref, sh2_ref, wo_ref, wr_ref, br_ref,
                x1_ref, lg_ref, mo_s, *, tm):
    mo_s[...] = jnp.dot(m_ref[...], wo_ref[...], preferred_element_type=F32)
    wr = wr_ref[...]
    wr_hi = wr.astype(BF16)
    wr_lo = (wr - wr_hi.astype(F32)).astype(BF16)

    def rows(r, carry):
        sl = pl.ds(pl.multiple_of(r * 128, 128), 128)
        x0 = _layer_norm(x_ref[sl, :], lng_ref[...], lnb_ref[...])
        x1 = _layer_norm(DEEPNORM_ALPHA * x0 + g1_ref[0] * mo_s[sl, :], l1g_ref[...], l1b_ref[...])
        x1_ref[sl, :] = x1
        tok = x1 * (1.0 + sc2_ref[0]) + sh2_ref[0]
        t_hi = tok.astype(BF16)
        t_lo = (tok - t_hi.astype(F32)).astype(BF16)
        lg_ref[sl, :] = (jnp.dot(t_hi, wr_hi, preferred_element_type=F32)
                         + jnp.dot(t_lo, wr_hi, preferred_element_type=F32)
                         + jnp.dot(t_hi, wr_lo, preferred_element_type=F32) + br_ref[...])
        return carry
    lax.fori_loop(0, tm // 128, rows, 0)


def _merge(mix, x2d, ln_g, ln_b, g1, ln1_g, ln1_b, sc2, sh2, w_o, w_r, b_r, seq):
    t, dm = x2d.shape
    tm = 512
    per_b = seq // tm
    row = lambda i: (i, 0)
    mod = lambda i: (i // per_b, 0, 0)
    return pl.pallas_call(
        functools.partial(_merge_body, tm=tm),
        grid=(t // tm,),
        scratch_shapes=[pltpu.VMEM((tm, dm), F32)],
        in_specs=[pl.BlockSpec((tm, dm), row), pl.BlockSpec((tm, dm), row),
                  _const_spec((1, dm)), _const_spec((1, dm)),
                  pl.BlockSpec((1, 1, dm), mod),
                  _const_spec((1, dm)), _const_spec((1, dm)),
                  pl.BlockSpec((1, 1, dm), mod), pl.BlockSpec((1, 1, dm), mod),
                  _const_spec((dm, dm)), _const_spec((dm, LANES)), _const_spec((1, LANES))],
        out_specs=[pl.BlockSpec((tm, dm), row), pl.BlockSpec((tm, LANES), row)],
        out_shape=[jax.ShapeDtypeStruct((t, dm), F32), jax.ShapeDtypeStruct((t, LANES), F32)],
        compiler_params=_cparams(("parallel",), 56),
        name="merge",
    )(mix, x2d, ln_g, ln_b, g1, ln1_g, ln1_b, sc2, sh2, w_o, w_r, b_r)


def _route_body(lg_ref, w_ref, d_ref, cnt_ref, run_s, tot_s, *, tr):
    phase = pl.program_id(0)

    @pl.when(pl.program_id(1) == 0)
    def _():
        @pl.when(phase == 1)
        def _():
            tot_s[...] = run_s[...]
        run_s[...] = jnp.zeros_like(run_s)

    lane = lax.broadcasted_iota(I32, (tr, LANES), 1)
    lane_f = lane.astype(F32)
    logit = lg_ref[...]
    hot, val = [], []
    for _ in range(TOP_K):
        mk = jnp.max(logit, axis=-1, keepdims=True)
        ik = jnp.min(jnp.where(logit == mk, lane_f, float(LANES)), axis=-1, keepdims=True)
        hk = lane_f == ik
        logit = jnp.where(hk, -jnp.inf, logit)
        hot.append(hk)
        val.append(mk)
    cnt = jnp.zeros((tr, LANES), F32)
    for hk in hot:
        cnt = cnt + jnp.where(hk, 1.0, 0.0)
    run = run_s[...] + jnp.sum(cnt, axis=0, keepdims=True)

    @pl.when(phase == 0)
    def _():
        cnt_ref[...] = run.astype(I32)

    @pl.when(phase == 1)
    def _():
        total = tot_s[...]
        padded = jnp.floor((total + (MOE_BLOCK - 1.0)) * (1.0 / MOE_BLOCK)) * MOE_BLOCK
        lane8 = lax.broadcasted_iota(I32, (8, LANES), 1)
        incl = padded
        sft = 1
        while sft < LANES:
            incl = incl + jnp.where(lane8 >= sft, pltpu.roll(incl, sft, 1), 0.0)
            sft *= 2
        pstart = (incl - padded)[0:1, :]
        lower = (lax.broadcasted_iota(I32, (tr, tr), 0) > lax.broadcasted_iota(I32, (tr, tr), 1))
        before = jnp.dot(jnp.where(lower, 1.0, 0.0).astype(BF16), cnt.astype(BF16),
                         preferred_element_type=F32) + (run_s[0:1, :] + pstart)
        ex = [jnp.exp(v - val[0]) for v in val]
        denom = ex[0] + ex[1] + ex[2] + ex[3]
        w_out = jnp.zeros((tr, LANES), F32)
        d_out = jnp.zeros((tr, LANES), I32)
        for k in range(TOP_K):
            dest = jnp.sum(jnp.where(hot[k], before, 0.0), axis=-1, keepdims=True)
            w_out = jnp.where(lane == k, ex[k] / denom, w_out)
            d_out = jnp.where(lane == k, dest.astype(I32), d_out)
        w_ref[...] = w_out
        d_ref[...] = d_out[:, :TOP_K]
        cnt_ref[...] = total.astype(I32)

    run_s[...] = run


def _route(logits):
    t = logits.shape[0]
    tr = 512
    row = lambda p, i: (i, 0)
    out_row = lambda p, i: (i * p, 0)
    return pl.pallas_call(
        functools.partial(_route_body, tr=tr),
        grid=(2, t // tr),
        in_specs=[pl.BlockSpec((tr, LANES), row)],
        out_specs=[pl.BlockSpec((tr, LANES), out_row), pl.BlockSpec((tr, TOP_K), out_row),
                   pl.BlockSpec((8, LANES), lambda p, i: (0, 0))],
        out_shape=[jax.ShapeDtypeStruct((t, LANES), F32), jax.ShapeDtypeStruct((t, TOP_K), I32),
                   jax.ShapeDtypeStruct((8, LANES), I32)],
        scratch_shapes=[pltpu.VMEM((8, LANES), F32), pltpu.VMEM((8, LANES), F32)],
        compiler_params=_cparams(("arbitrary", "arbitrary"), 32),
        name="route",
    )(logits)


def _scatter_body(cnt_ref, pstart_ref, used_ref, dest_ref, x1_ref, sc_ref, sh_ref, xb_ref, tok_ref, zero_s, sem, pad_sem,
                  *, ts, n_blocks):
    def row_copy(src, r_src, r_dst, s):
        return pltpu.make_async_copy(src.at[pl.ds(r_src, 1)], xb_ref.at[pl.ds(r_dst, 1)], s)

    def block_copy(blk):
        return pltpu.make_async_copy(zero_s, xb_ref.at[pl.ds(pl.multiple_of(blk * MOE_BLOCK, MOE_BLOCK), MOE_BLOCK)],
                                     pad_sem)

    @pl.when(pl.program_id(0) == 0)
    def _():
        zero_s[...] = jnp.zeros_like(zero_s)

        def last_block(e):
            return pstart_ref[e] // MOE_BLOCK + cnt_ref[e] // MOE_BLOCK

        def pad_start(e, c):
            @pl.when(cnt_ref[e] % MOE_BLOCK != 0)
            def _():
                block_copy(last_block(e)).start()
            return c

        def pad_wait(e, c):
            @pl.when(cnt_ref[e] % MOE_BLOCK != 0)
            def _():
                block_copy(last_block(e)).wait()
            return c
        lax.fori_loop(0, N_EXPERTS, pad_start, 0)
        lax.fori_loop(0, N_EXPERTS, pad_wait, 0)

        def tail_start(blk, c):
            block_copy(blk).start()
            return c

        def tail_wait(blk, c):
            block_copy(blk).wait()
            return c
        lax.fori_loop(used_ref[0], n_blocks, tail_start, 0)
        lax.fori_loop(used_ref[0], n_blocks, tail_wait, 0)

    tok_ref[...] = x1_ref[...] * (1.0 + sc_ref[0]) + sh_ref[0]

    def row(r, carry):
        for k in range(TOP_K):
            row_copy(tok_ref, r, dest_ref[r * TOP_K + k], sem).start()
        return carry
    lax.fori_loop(0, ts, row, 0, unroll=8)
    for _ in range(TOP_K):
        pltpu.make_async_copy(tok_ref, xb_ref.at[pl.ds(0, ts)], sem).wait()


def _scatter(x1, sc2, sh2, dest_flat, counts, pstart, used, n_blocks, seq):
    t, dm = x1.shape
    ts = 256
    per_b = seq // ts
    grid_spec = pltpu.PrefetchScalarGridSpec(
        num_scalar_prefetch=3,
        grid=(t // ts,),
        in_specs=[pl.BlockSpec((ts * TOP_K,), lambda i, *_: (i,), memory_space=pltpu.SMEM),
                  pl.BlockSpec((ts, dm), lambda i, *_: (i, 0)),
                  pl.BlockSpec((1, 1, dm), lambda i, *_: (i // per_b, 0, 0)),
                  pl.BlockSpec((1, 1, dm), lambda i, *_: (i // per_b, 0, 0))],
        out_specs=pl.BlockSpec(memory_space=pl.ANY),
        scratch_shapes=[pltpu.VMEM((ts, dm), F32), pltpu.VMEM((MOE_BLOCK, dm), F32),
                        pltpu.SemaphoreType.DMA(()), pltpu.SemaphoreType.DMA(())],
    )
    return pl.pallas_call(
        functools.partial(_scatter_body, ts=ts, n_blocks=n_blocks),
        grid_spec=grid_spec,
        out_shape=jax.ShapeDtypeStruct((n_blocks * MOE_BLOCK, dm), F32),
        compiler_params=_cparams(("arbitrary",), 32),
        name="scatter",
    )(counts, pstart, used, dest_flat, x1, sc2, sh2)


def _ffn1_body(e_ref, c_ref, blk_ref, oblk_ref, oc_ref, first_ref, n_ref,
               x_ref, wg_ref, wu_ref, bg_ref, bu_ref, a_ref, wg_s, wu_s):
    s = pl.program_id(0)

    @pl.when(first_ref[s] == 1)
    def _():
        wg_s[...] = wg_ref[0].astype(BF16)
        wu_s[...] = wu_ref[0].astype(BF16)

    @pl.when(s < n_ref[0])
    def _():
        x = x_ref[...].astype(BF16)
        g = jnp.minimum(jnp.dot(x, wg_s[...], preferred_element_type=F32) + bg_ref[0], SWIGLU_LIMIT)
        u = jnp.clip(jnp.dot(x, wu_s[...], preferred_element_type=F32) + bu_ref[0], -SWIGLU_LIMIT, SWIGLU_LIMIT)
        a_ref[...] = (g * _sigmoid(SWIGLU_ALPHA * g) * (u + 1.0)).astype(BF16)

    @pl.when(s >= n_ref[0])
    def _():
        a_ref[...] = jnp.zeros_like(a_ref)


def _ffn2_body(e_ref, c_ref, blk_ref, oblk_ref, oc_ref, first_ref, n_ref, a_ref, wd_ref, bd_ref, y_ref, wd_s):
    s = pl.program_id(0)

    @pl.when(first_ref[s] == 1)
    def _():
        wd_s[...] = wd_ref[0].astype(BF16)

    @pl.when(s < n_ref[0])
    def _():
        y_ref[...] = jnp.dot(a_ref[...], wd_s[...], preferred_element_type=F32) + bd_ref[0]

    @pl.when(s >= n_ref[0])
    def _():
        y_ref[...] = jnp.zeros_like(y_ref)


def _ffn1(plan, xb, w_gate, w_up, b_gate, b_up, n_steps):
    rows, dm = xb.shape
    de = w_gate.shape[2]
    wspec = pl.BlockSpec((1, dm, FF_CHUNK), lambda s, e, c, *_: (e[s], 0, c[s]))
    bspec = pl.BlockSpec((1, 1, FF_CHUNK), lambda s, e, c, *_: (e[s], 0, c[s]))
    grid_spec = pltpu.PrefetchScalarGridSpec(
        num_scalar_prefetch=7,
        grid=(n_steps,),
        in_specs=[pl.BlockSpec((MOE_BLOCK, dm), lambda s, e, c, blk, *_: (blk[s], 0)), wspec, wspec, bspec, bspec],
        out_specs=pl.BlockSpec((MOE_BLOCK, FF_CHUNK), lambda s, e, c, blk, oblk, oc, *_: (oblk[s], oc[s])),
        scratch_shapes=[pltpu.VMEM((dm, FF_CHUNK), BF16), pltpu.VMEM((dm, FF_CHUNK), BF16)],
    )
    return pl.pallas_call(
        _ffn1_body, grid_spec=grid_spec,
        out_shape=jax.ShapeDtypeStruct((rows, de), BF16),
        compiler_params=_cparams(("arbitrary",), 60),
        name="ffn1",
    )(*plan, xb, w_gate, w_up, b_gate, b_up)


def _ffn2(plan, act, w_down, b_down, n_steps, chunk):
    rows, de = act.shape
    dm = w_down.shape[2]
    grid_spec = pltpu.PrefetchScalarGridSpec(
        num_scalar_prefetch=7,
        grid=(n_steps,),
        in_specs=[pl.BlockSpec((MOE_BLOCK, de), lambda s, e, c, blk, *_: (blk[s], 0)),
                  pl.BlockSpec((1, de, chunk), lambda s, e, c, *_: (e[s], 0, c[s])),
                  pl.BlockSpec((1, 1, chunk), lambda s, e, c, *_: (e[s], 0, c[s]))],
        out_specs=pl.BlockSpec((MOE_BLOCK, chunk), lambda s, e, c, blk, oblk, oc, *_: (oblk[s], oc[s])),
        scratch_shapes=[pltpu.VMEM((de, chunk), BF16)],
    )
    return pl.pallas_call(
        _ffn2_body, grid_spec=grid_spec,
        out_shape=jax.ShapeDtypeStruct((rows, dm), F32),
        compiler_params=_cparams(("arbitrary",), 60),
        name="ffn2",
    )(*plan, act, w_down, b_down)


def _combine_body(dcur_ref, dnxt_ref, w_ref, x1_ref, g2_ref, lg_ref, lb_ref, yb_ref, o_ref, buf, sem, *, tc, n_tiles):
    i = pl.program_id(0)

    def issue(d_ref, slot):
        def row(r, carry):
            for k in range(TOP_K):
                pltpu.make_async_copy(yb_ref.at[pl.ds(d_ref[r * TOP_K + k], 1)],
                                      buf.at[slot, k, pl.ds(r, 1)], sem.at[slot]).start()
            return carry
        lax.fori_loop(0, tc, row, 0, unroll=8)

    @pl.when(i == 0)
    def _():
        issue(dcur_ref, 0)

    @pl.when(i + 1 < n_tiles)
    def _():
        issue(dnxt_ref, (i + 1) % 2)

    slot = i % 2
    for k in range(TOP_K):
        pltpu.make_async_copy(yb_ref.at[pl.ds(0, tc)], buf.at[slot, k], sem.at[slot]).wait()
    w = w_ref[...]
    y = w[:, 0:1] * buf[slot, 0]
    for k in range(1, TOP_K):
        y = y + w[:, k:k + 1] * buf[slot, k]
    o_ref[...] = _layer_norm(DEEPNORM_ALPHA * x1_ref[...] + g2_ref[0] * y, lg_ref[...], lb_ref[...])


def _combine(dest_flat, w4, x1, g2, ln_g, ln_b, yb, seq):
    t, dm = x1.shape
    tc = 256
    n_tiles = t // tc
    per_b = seq // tc
    row = lambda i: (i, 0)
    return pl.pallas_call(
        functools.partial(_combine_body, tc=tc, n_tiles=n_tiles),
        grid=(n_tiles,),
        in_specs=[pl.BlockSpec((tc * TOP_K,), lambda i: (i,), memory_space=pltpu.SMEM),
                  pl.BlockSpec((tc * TOP_K,), lambda i: (jnp.minimum(i + 1, n_tiles - 1),), memory_space=pltpu.SMEM),
                  pl.BlockSpec((tc, LANES), row),
                  pl.BlockSpec((tc, dm), row),
                  pl.BlockSpec((1, 1, dm), lambda i: (i // per_b, 0, 0)),
                  _const_spec((1, dm)), _const_spec((1, dm)),
                  pl.BlockSpec(memory_space=pl.ANY)],
        out_specs=pl.BlockSpec((tc, dm), row),
        out_shape=jax.ShapeDtypeStruct((t, dm), F32),
        scratch_shapes=[pltpu.VMEM((2, TOP_K, tc, dm), F32), pltpu.SemaphoreType.DMA((2,))],
        compiler_params=_cparams(("arbitrary",), 40),
        name="combine",
    )(dest_flat, dest_flat, w4, x1, g2, ln_g, ln_b, yb)


def _dft_tables(seq):
    n = 2 * seq
    j = np.arange(seq // 2, dtype=np.int64)
    tables = []
    for k in (2 * j, 2 * j + 1):
        ang = (2.0 * np.pi / n) * ((k[:, None] * j[None, :]) % n).astype(np.float64)
        tables += [np.cos(ang), np.sin(ang)]
    tables += [tables[2].T, tables[3].T]
    return tuple(jnp.asarray(m, F32).astype(BF16) for m in tables)


def _filter_features(seq):
    t = jnp.linspace(0.0, 1.0, seq, dtype=F32)[:, None]
    bands = (HYENA_EMB - 1) // 2
    f = jnp.linspace(1e-4, bands - 1, bands, dtype=F32)[None, :]
    ang = 2.0 * math.pi * jnp.arange(seq, dtype=F32)[:, None] * f / seq
    feats = jnp.concatenate([t, jnp.cos(ang), -jnp.sin(ang)], axis=-1)
    max_decay = math.log(HYENA_DECAY_TARGET) / HYENA_SHORT_DECAY_PCT
    min_decay = math.log(HYENA_DECAY_TARGET) / HYENA_LONG_DECAY_PCT
    deltas = jnp.abs(jnp.linspace(min_decay, max_decay, HYENA_WIDTH, dtype=F32))[None, :]
    return t, feats, deltas


def _moe_plan(counts, n_blocks, n_chunks):
    ids = jnp.arange(N_EXPERTS, dtype=I32)
    nblk = (counts + MOE_BLOCK - 1) // MOE_BLOCK
    blk_end = jnp.cumsum(nblk)
    blk_start = blk_end - nblk
    used = blk_end[-1]
    steps = n_chunks * nblk
    step_end = jnp.cumsum(steps)
    n_used = step_end[-1]
    s_all = jnp.arange(n_chunks * n_blocks, dtype=I32)
    s = jnp.minimum(s_all, n_used - 1)
    e_s = jnp.minimum(jnp.sum((s[:, None] >= step_end[None, :]).astype(I32), axis=1), N_EXPERTS - 1)
    onehot = e_s[:, None] == ids[None, :]
    pick = lambda table: jnp.sum(jnp.where(onehot, table[None, :], 0), axis=1)
    loc = s - pick(step_end - steps)
    nb = jnp.maximum(pick(nblk), 1)
    c_s = loc // nb
    r_s = loc % nb
    blk = pick(blk_start) + r_s
    tail = s_all >= n_used
    j = jnp.maximum(s_all - n_used, 0)
    n_tail = jnp.maximum(n_blocks - used, 1)
    oblk = jnp.where(tail, used + j % n_tail, blk)
    oc = jnp.where(tail, j // n_tail, c_s)
    first = jnp.logical_and(r_s == 0, jnp.logical_not(tail))
    as_i32 = lambda v: v.astype(I32)
    plan = tuple(map(as_i32, (e_s, c_s, blk, oblk, oc, first, n_used.reshape(1))))
    return as_i32(blk_start * MOE_BLOCK), as_i32(used.reshape(1)), plan


def kernel(x, c, ctx, c_ctx, ln_in_g, ln_in_b, w_mod, b_mod, w_in, b_in, mlstm_conv_w, mlstm_conv_b,
           w_qh, w_kh, hyena_conv_w, hyena_conv_b, filt_w1, filt_b1, filt_wh, filt_bh, filt_freq, filt_wout,
           hyena_skip, w_proj_a, w_proj_h, w_out, ln1_g, ln1_b, w_router, b_router, w_gate, b_gate,
           w_up, b_up, w_down, b_down, ln2_g, ln2_b):
    bsz, seq, dm = x.shape
    ctx_len = ctx.shape[1]
    t = bsz * seq
    assert w_mod.shape[0] == DEPTH and dm == D_MODEL and ctx_len == CHUNK and bsz + 1 <= 16
    row = lambda v: v.reshape(1, -1)

    cond = jnp.concatenate([c, c_ctx[None], jnp.zeros((16 - bsz - 1, dm), F32)], axis=0)
    mod = _mod(cond, w_mod[0], row(b_mod[0]))
    sh1, sc1, g1, sh2, sc2, g2 = [m[:, None, :] for m in jnp.split(mod, 6, axis=-1)]

    w_main = jnp.concatenate([w_in[0][:, :IN_GATES].astype(BF16), w_in[0][:, IN_O:].astype(BF16)], axis=1)
    b_main = row(jnp.concatenate([b_in[0][:IN_GATES], b_in[0][IN_O:]]))
    w_g = jnp.pad(w_in[0][:, IN_GATES:IN_O], ((0, 0), (0, LANES - 4 * HEADS))).astype(BF16)
    b_g = row(jnp.pad(b_in[0][IN_GATES:IN_O], (0, LANES - 4 * HEADS)))
    lng, lnb = row(ln_in_g), row(ln_in_b)
    x2d = x.reshape(t, dm)
    z, gates, sg = _in_proj(x2d, lng, lnb, sc1[:bsz], sh1[:bsz], w_main, b_main, w_g, b_g, seq, Z_BG)
    zc, gates_c = _in_proj(ctx.reshape(bsz * ctx_len, dm), lng, lnb, sc1[bsz:bsz + 1], sh1[bsz:bsz + 1],
                           w_main[:, :IN_GATES], b_main[:, :IN_GATES], w_g, b_g, bsz * ctx_len, IN_GATES)

    g_all = jnp.concatenate([gates_c[:, :4 * HEADS].reshape(bsz, ctx_len, 4, HEADS),
                             gates[:, :4 * HEADS].reshape(bsz, seq, 4, HEADS)], axis=1)
    gates_t = g_all.transpose(0, 3, 2, 1)

    oh = _mlstm(z, zc, gates_t, mlstm_conv_w[0].reshape(9, MLSTM_WIDTH), row(mlstm_conv_b[0]),
                w_qh[0], w_kh[0].transpose(0, 2, 1), bsz, seq, ctx_len)

    tcol, feats, deltas = _filter_features(seq)
    feats = jnp.pad(feats, ((0, 0), (0, LANES - HYENA_EMB)))
    w1 = jnp.pad(filt_w1[0], ((0, LANES - HYENA_EMB), (0, 0)))
    a = _filt(feats, w1, row(filt_b1[0]), filt_wh[0], filt_bh[0], filt_freq[0])
    fh = a.shape[1]
    w_fout4 = filt_wout[0].reshape(fh, 4, HYENA_WIDTH).transpose(1, 0, 2)
    hh = _hyena(z, hyena_conv_w[0].reshape(9, 3 * HYENA_WIDTH), row(hyena_conv_b[0]), a, w_fout4, deltas, tcol,
                hyena_skip[0], _dft_tables(seq), bsz, seq)

    w_r = jnp.pad(w_router[0], ((0, 0), (0, LANES - N_EXPERTS)))
    b_r = row(jnp.pad(b_router[0], (0, LANES - N_EXPERTS), constant_values=-1e30))
    mix = _mix(oh, hh, sg, w_proj_a[0].astype(BF16), w_proj_h[0].astype(BF16))
    x1, logits = _merge(mix, x2d, lng, lnb, g1[:bsz], row(ln1_g[0]), row(ln1_b[0]), sc2[:bsz], sh2[:bsz],
                        w_out[0].astype(BF16), w_r, b_r, seq)

    w4, dest, counts = _route(logits)
    n_blocks = -(-(t * TOP_K + N_EXPERTS * (MOE_BLOCK - 1)) // MOE_BLOCK)
    n_chunks = dm // FF_CHUNK
    counts = counts[0, :N_EXPERTS]
    pstart, used, plan = _moe_plan(counts, n_blocks, n_chunks)
    dest_flat = dest.reshape(t * TOP_K)
    xb = _scatter(x1, sc2[:bsz], sh2[:bsz], dest_flat, counts, pstart, used, n_blocks, seq)
    act = _ffn1(plan, xb, w_gate[0], w_up[0], b_gate[0][:, None, :], b_up[0][:, None, :], n_chunks * n_blocks)
    _, _, plan_down = _moe_plan(counts, n_blocks, 1)
    yb = _ffn2(plan_down, act, w_down[0], b_down[0][:, None, :], n_blocks, dm)
    out = _combine(dest_flat, w4, x1, g2[:bsz], row(ln2_g[0]), row(ln2_b[0]), yb, seq)
    return out.reshape(bsz, seq, dm)
```

```python
import functools
import math

import numpy as np
import jax
import jax.numpy as jnp
from jax import lax
from jax.experimental import pallas as pl
from jax.experimental.pallas import tpu as pltpu

F32 = jnp.float32
BF16 = jnp.bfloat16
I32 = jnp.int32
HIGHEST = lax.Precision.HIGHEST

D_MODEL = 2048
GRID_W = 64
HEADS = 4
HEAD_DIM = 256
MLSTM_WIDTH = HEADS * HEAD_DIM
HYENA_WIDTH = D_MODEL // 2
HYENA_EMB = 33
HYENA_DECAY_TARGET = 1e-2
HYENA_SHORT_DECAY_PCT = 0.3
HYENA_LONG_DECAY_PCT = 1.5
N_EXPERTS = 32
TOP_K = 4
SWIGLU_LIMIT = 7.0
SWIGLU_ALPHA = 1.702
LN_EPS = 1e-5
DEPTH = 1
DEEPNORM_ALPHA = (2.0 * DEPTH) ** 0.25

IN_V = MLSTM_WIDTH
IN_GATES = 2 * MLSTM_WIDTH
IN_O = IN_GATES + 4 * HEADS
Z_QK, Z_V, Z_O, Z_HY, Z_BG = 0, 1024, 2048, 3072, 6144

LANES = 128
MXU = 256
CHUNK = 256
MOE_BLOCK = 512
FF_CHUNK = 1024
MIB = 1024 * 1024


def _cparams(semantics, vmem_mib):
    return pltpu.CompilerParams(dimension_semantics=semantics, vmem_limit_bytes=vmem_mib * MIB)


def _const_spec(shape):
    nd = len(shape)
    return pl.BlockSpec(shape, lambda *_: (0,) * nd, pipeline_mode=pl.Buffered(1))


def _layer_norm(x, g, b):
    mu = jnp.mean(x, axis=-1, keepdims=True)
    xc = x - mu
    var = jnp.mean(xc * xc, axis=-1, keepdims=True)
    return xc * lax.rsqrt(var + LN_EPS) * g + b


def _sigmoid(x):
    return 1.0 / (1.0 + jnp.exp(-x))


def _silu(x):
    return x * _sigmoid(x)


def _log_sigmoid(x):
    return jnp.minimum(x, 0.0) - jnp.log(1.0 + jnp.exp(-jnp.abs(x)))


def _mod_body(c_ref, w_ref, b_ref, o_ref):
    s = _silu(c_ref[...])
    o_ref[...] = jnp.dot(s.astype(BF16), w_ref[...].astype(BF16), preferred_element_type=F32) + b_ref[...]


def _mod(cond, w, b):
    rows, dm = cond.shape
    n = w.shape[1]
    tn = 1024
    return pl.pallas_call(
        _mod_body,
        grid=(n // tn,),
        in_specs=[pl.BlockSpec((rows, dm), lambda j: (0, 0)),
                  pl.BlockSpec((dm, tn), lambda j: (0, j)),
                  pl.BlockSpec((1, tn), lambda j: (0, j))],
        out_specs=pl.BlockSpec((rows, tn), lambda j: (0, j)),
        out_shape=jax.ShapeDtypeStruct((rows, n), F32),
        compiler_params=_cparams(("arbitrary",), 40),
        name="mod",
    )(cond, w, b)


def _in_proj_body(x_ref, lng_ref, lnb_ref, sc_ref, sh_ref, w_ref, b_ref, wg_ref, bg_ref,
                  z_ref, g_ref, *rest, tm, n_plain):
    hx_s = rest[-1]

    @pl.when(pl.program_id(1) == 0)
    def _():
        def rows(r, carry):
            sl = pl.ds(pl.multiple_of(r * 128, 128), 128)
            xn = _layer_norm(x_ref[sl, :], lng_ref[...], lnb_ref[...])
            hx_s[sl, :] = (xn * (1.0 + sc_ref[0]) + sh_ref[0]).astype(BF16)
            return carry
        lax.fori_loop(0, tm // 128, rows, 0)
        g_ref[...] = jnp.dot(hx_s[...], wg_ref[...], preferred_element_type=F32) + bg_ref[...]

    acc = jnp.dot(hx_s[...], w_ref[...], preferred_element_type=F32) + b_ref[...]
    if len(rest) == 1:
        z_ref[...] = acc
    else:
        @pl.when(pl.program_id(1) < n_plain)
        def _():
            z_ref[...] = acc

        @pl.when(pl.program_id(1) >= n_plain)
        def _():
            rest[0][...] = _sigmoid(acc).astype(BF16)


def _in_proj(x2d, ln_g, ln_b, scale, shift, w, b, w_gates, b_gates, rows_per_mod, plain_cols):
    t, dm = x2d.shape
    n = w.shape[1]
    tm, tn = min(1024, t), 1024
    n_plain = plain_cols // tn
    out_specs = [pl.BlockSpec((tm, tn), lambda i, j: (i, jnp.minimum(j, n_plain - 1))),
                 pl.BlockSpec((tm, LANES), lambda i, j: (i, 0))]
    out_shape = [jax.ShapeDtypeStruct((t, plain_cols), F32), jax.ShapeDtypeStruct((t, LANES), F32)]
    if n > plain_cols:
        out_specs.append(pl.BlockSpec((tm, tn), lambda i, j: (i, jnp.maximum(j - n_plain, 0))))
        out_shape.append(jax.ShapeDtypeStruct((t, n - plain_cols), BF16))
    return pl.pallas_call(
        functools.partial(_in_proj_body, tm=tm, n_plain=n_plain),
        grid=(t // tm, n // tn),
        in_specs=[pl.BlockSpec((tm, dm), lambda i, j: (i, 0)),
                  pl.BlockSpec((1, dm), lambda i, j: (0, 0)),
                  pl.BlockSpec((1, dm), lambda i, j: (0, 0)),
                  pl.BlockSpec((1, 1, dm), lambda i, j: (i * tm // rows_per_mod, 0, 0)),
                  pl.BlockSpec((1, 1, dm), lambda i, j: (i * tm // rows_per_mod, 0, 0)),
                  pl.BlockSpec((dm, tn), lambda i, j: (0, j)),
                  pl.BlockSpec((1, tn), lambda i, j: (0, j)),
                  pl.BlockSpec((dm, LANES), lambda i, j: (0, 0)),
                  pl.BlockSpec((1, LANES), lambda i, j: (0, 0))],
        out_specs=out_specs,
        out_shape=out_shape,
        scratch_shapes=[pltpu.VMEM((tm, dm), BF16)],
        compiler_params=_cparams(("parallel", "arbitrary"), 48),
        name="in_proj",
    )(x2d, ln_g, ln_b, scale, shift, w, b, w_gates, b_gates)


def _dwconv(u, w9, bias, width, single_row):
    length, ch = u.shape
    col = lax.broadcasted_iota(I32, (length, ch), 0) % width
    if not single_row:
        zpad = jnp.zeros((width, ch), F32)
        up = jnp.concatenate([zpad, u[:length - width]], axis=0)
        dn = jnp.concatenate([u[width:], zpad], axis=0)
    out = None
    for dc in (-1, 0, 1):
        a = u * w9[4 + dc:5 + dc]
        if not single_row:
            a = a + up * w9[1 + dc:2 + dc] + dn * w9[7 + dc:8 + dc]
        if dc == -1:
            a = jnp.where(col == 0, 0.0, pltpu.roll(a, 1, 0))
        elif dc == 1:
            a = jnp.where(col == width - 1, 0.0, pltpu.roll(a, length - 1, 0))
        out = a if out is None else out + a
    return out + bias


def _mlstm_body(zqk_ref, zv_ref, zo_ref, cqk_ref, cv_ref, gt_ref, cw_ref, cb_ref, wq_ref, wkt_ref,
                o_ref, q_s, k_s, kt_s, v_s, hf_s, hb_s, r_s, c_s, ct_s, *, ctx_len, seq):
    n_chunks = (ctx_len + seq) // CHUNK
    total = ctx_len + seq
    cw = cw_ref[...]
    cb = cb_ref[...]
    wq = wq_ref[0].astype(BF16)
    wkt = wkt_ref[0].astype(BF16)
    nt = (((1,), (1,)), ((), ()))
    scale = HEAD_DIM ** -0.5

    def project(u, off, n):
        ub = u.astype(BF16)
        q_s[off:off + n, :] = jnp.dot(ub, wq, preferred_element_type=F32).astype(BF16)
        k_s[off:off + n, :] = (lax.dot_general(ub, wkt, nt, preferred_element_type=F32) * scale).astype(BF16)
        kt_s[:, off:off + n] = (lax.dot_general(wkt, ub, nt, preferred_element_type=F32) * scale).astype(BF16)

    project(_silu(_dwconv(cqk_ref[...], cw, cb, ctx_len, True)), 0, ctx_len)
    project(_silu(_dwconv(zqk_ref[...], cw, cb, GRID_W, False)), ctx_len, seq)
    v_s[0:ctx_len, :] = cv_ref[...]
    v_s[ctx_len:total, :] = zv_ref[...]

    gt = gt_ref[0, 0]
    lf = _log_sigmoid(gt)
    pos = lax.broadcasted_iota(I32, (4, total), 1) % CHUNK
    pre, suf = lf, lf
    s = 1
    while s < CHUNK:
        pre = pre + jnp.where(pos >= s, pltpu.roll(pre, s, 1), 0.0)
        suf = suf + jnp.where(pos < CHUNK - s, pltpu.roll(suf, total - s, 1), 0.0)
        s *= 2
    b_f, li_f, b_b, li_b = pre[1:2], gt[0:1], suf[3:4], gt[2:3]
    pm_f, pm_b = li_f - b_f, li_b - b_b
    pos1 = pos[0:1]
    s = 1
    while s < CHUNK:
        pm_f = jnp.maximum(pm_f, jnp.where(pos1 >= s, pltpu.roll(pm_f, s, 1), -jnp.inf))
        pm_b = jnp.maximum(pm_b, jnp.where(pos1 < CHUNK - s, pltpu.roll(pm_b, total - s, 1), -jnp.inf))
        s *= 2
    r_s[...] = jnp.concatenate([b_f, li_f, pm_f, b_b, li_b, pm_b, jnp.zeros((2, total), F32)], axis=0)
    zfill = jnp.zeros((CHUNK - 8, CHUNK), F32)
    for c in range(n_chunks):
        blk = jnp.concatenate([r_s[:, c * CHUNK:(c + 1) * CHUNK], zfill], axis=0).T
        c_s[c * CHUNK:(c + 1) * CHUNK, :] = blk[:, :LANES]

    ct_s[...] = jnp.zeros_like(ct_s)
    row_i = lax.broadcasted_iota(I32, (CHUNK, CHUNK), 0)
    col_i = lax.broadcasted_iota(I32, (CHUNK, CHUNK), 1)
    masks = (row_i >= col_i, row_i <= col_i)

    def chunk_step(c, d, n_vec, m):
        off = c * CHUNK if isinstance(c, int) else pl.multiple_of(c * CHUNK, CHUNK)
        rows = r_s[:, pl.ds(off, CHUNK)]
        cols = c_s[pl.ds(off, CHUNK), :]
        b_row, li_row, pm_row = rows[3 * d:3 * d + 1], rows[3 * d + 1:3 * d + 2], rows[3 * d + 2:3 * d + 3]
        b_col, li_col, pm_col = cols[:, 3 * d:3 * d + 1], cols[:, 3 * d + 1:3 * d + 2], cols[:, 3 * d + 2:3 * d + 3]
        last = slice(CHUNK - 1, CHUNK) if d == 0 else slice(0, 1)
        b_end, pm_end = b_row[:, last], pm_row[:, last]
        qc = q_s[pl.ds(off, CHUNK), :]
        kc = k_s[pl.ds(off, CHUNK), :]
        ktc = kt_s[:, pl.ds(off, CHUNK)]
        vc = v_s[pl.ds(off, CHUNK), :]
        ct = ct_s[d]

        inter = b_col + m
        m_t = jnp.maximum(inter, b_col + pm_col)
        wts = jnp.exp(jnp.where(masks[d], b_col - b_row + li_row, -jnp.inf) - m_t)
        s_inter = jnp.exp(inter - m_t)
        scores = jnp.dot(qc, ktc, preferred_element_type=F32) * wts
        num = (s_inter * jnp.dot(qc, ct.astype(BF16), preferred_element_type=F32)
               + jnp.dot(scores.astype(BF16), vc.astype(BF16), preferred_element_type=F32))
        den = (s_inter * jnp.sum(qc.astype(F32) * n_vec, axis=-1, keepdims=True)
               + jnp.sum(scores, axis=-1, keepdims=True))
        h = num / jnp.maximum(jnp.abs(den), jnp.exp(-m_t))

        m_new = jnp.maximum(b_end + m, b_end + pm_end)
        decay = jnp.exp(b_end + m - m_new)
        w = jnp.exp(b_end - b_col + li_col - m_new)
        ct_s[d] = decay * ct + jnp.dot(ktc, (vc * w).astype(BF16), preferred_element_type=F32)
        n_new = decay * n_vec + jnp.sum(kc.astype(F32) * w, axis=0, keepdims=True)
        return h, n_new, m_new

    n0 = jnp.zeros((1, HEAD_DIM), F32)
    m0 = jnp.zeros((1, 1), F32)
    _, nf, mf = chunk_step(0, 0, n0, m0)
    _, nb, mb = chunk_step(0, 1, n0, m0)

    def body(i, carry):
        nf, mf, nb, mb = carry
        hf, nf, mf = chunk_step(i, 0, nf, mf)
        hf_s[pl.ds(pl.multiple_of(i * CHUNK - ctx_len, CHUNK), CHUNK), :] = hf
        j = n_chunks - i
        hb, nb, mb = chunk_step(j, 1, nb, mb)
        hb_s[pl.ds(pl.multiple_of(j * CHUNK - ctx_len, CHUNK), CHUNK), :] = hb
        return nf, mf, nb, mb

    lax.fori_loop(1, n_chunks, body, (nf, mf, nb, mb))
    o_ref[...] = (_sigmoid(zo_ref[...]) * (hf_s[...] + hb_s[...])).astype(BF16)


def _mlstm(z, zc, gates_t, conv_w9, conv_b, w_qh, w_kh_t, bsz, seq, ctx_len):
    total = ctx_len + seq
    hd = HEAD_DIM
    qk_blk, v_blk, o_blk = Z_QK // hd, Z_V // hd, Z_O // hd
    return pl.pallas_call(
        functools.partial(_mlstm_body, ctx_len=ctx_len, seq=seq),
        grid=(bsz, HEADS),
        in_specs=[pl.BlockSpec((seq, hd), lambda b, h: (b, qk_blk + h)),
                  pl.BlockSpec((seq, hd), lambda b, h: (b, v_blk + h)),
                  pl.BlockSpec((seq, hd), lambda b, h: (b, o_blk + h)),
                  pl.BlockSpec((ctx_len, hd), lambda b, h: (b, qk_blk + h)),
                  pl.BlockSpec((ctx_len, hd), lambda b, h: (b, v_blk + h)),
                  pl.BlockSpec((1, 1, 4, total), lambda b, h: (b, h, 0, 0)),
                  pl.BlockSpec((9, hd), lambda b, h: (0, h)),
                  pl.BlockSpec((1, hd), lambda b, h: (0, h)),
                  pl.BlockSpec((1, hd, hd), lambda b, h: (h, 0, 0)),
                  pl.BlockSpec((1, hd, hd), lambda b, h: (h, 0, 0))],
        out_specs=pl.BlockSpec((seq, hd), lambda b, h: (b, h)),
        out_shape=jax.ShapeDtypeStruct((bsz * seq, MLSTM_WIDTH), BF16),
        scratch_shapes=[pltpu.VMEM((total, hd), BF16), pltpu.VMEM((total, hd), BF16), pltpu.VMEM((hd, total), BF16),
                        pltpu.VMEM((total, hd), F32), pltpu.VMEM((seq, hd), F32), pltpu.VMEM((seq, hd), F32),
                        pltpu.VMEM((8, total), F32), pltpu.VMEM((total, LANES), F32), pltpu.VMEM((2, hd, hd), F32)],
        compiler_params=_cparams(("parallel", "arbitrary"), 48),
        name="mlstm",
    )(z, z, z, zc, zc, gates_t, conv_w9, conv_b, w_qh, w_kh_t)


def _filt_body(z_ref, w1_ref, b1_ref, wh_ref, bh_ref, fr_ref, a_ref):
    fr = fr_ref[...]
    a = jnp.sin(fr[0:1] * (jnp.dot(z_ref[...], w1_ref[...], precision=HIGHEST, preferred_element_type=F32)
                           + b1_ref[...]))
    for i in range(2):
        a = jnp.sin(fr[i + 1:i + 2] * (jnp.dot(a, wh_ref[i], precision=HIGHEST, preferred_element_type=F32)
                                       + bh_ref[i:i + 1]))
    a_ref[...] = a


def _filt(feats, w1, b1, wh, bh, freq):
    length = feats.shape[0]
    fh = w1.shape[1]
    return pl.pallas_call(
        _filt_body,
        out_shape=jax.ShapeDtypeStruct((length, fh), F32),
        name="filt",
    )(feats, w1, b1, wh, bh, freq)


def _hyena_body(zx1_ref, zx2_ref, zv_ref, cw1_ref, cw2_ref, cwv_ref, cb1_ref, cb2_ref, cbv_ref,
                a_ref, wf_ref, dl_ref, t_ref, skip_ref, ce_ref, se_ref, co_ref, so_ref, cot_ref, sot_ref, o_ref,
                hce_s, hse_s, hco_s, hso_s, hn_s, v_s, g_s, *, seq):
    n_fft = 2 * seq
    half = seq // 2
    sign = jnp.where(lax.broadcasted_iota(I32, (seq, 1), 0) % 2 == 0, 1.0, -1.0)
    jrow = lax.broadcasted_iota(I32, (half, 1), 0)
    sgn_j = jnp.where(jrow % 2 == 0, 1.0, -1.0)
    first = jrow == 0
    flip = jnp.where(lax.broadcasted_iota(I32, (MXU, MXU), 0) + lax.broadcasted_iota(I32, (MXU, MXU), 1) == MXU - 1,
                     1.0, 0.0).astype(BF16)

    def mm(m_ref, x):
        return jnp.dot(m_ref[...], x, preferred_element_type=F32)

    def reverse_rows(x):
        hi = x.astype(BF16)
        lo = (x - hi.astype(F32)).astype(BF16)
        blocks = []
        for b in range(half // MXU):
            src = slice(half - MXU * (b + 1), half - MXU * b)
            blocks.append(jnp.dot(flip, hi[src], preferred_element_type=F32)
                          + jnp.dot(flip, lo[src], preferred_element_type=F32))
        return jnp.concatenate(blocks, axis=0)

    def fold(v):
        rolled = pltpu.roll(reverse_rows(v[half:]), 1, 0)
        mid = rolled[0:1]
        vr = jnp.where(first, 0.0, rolled)
        return v[:half] + vr, v[:half] - vr, mid

    def forward(v):
        s, d, mid = fold(v)
        sb, db = s.astype(BF16), d.astype(BF16)
        return mm(ce_ref, sb) + sgn_j * mid, mm(se_ref, db), mm(co_ref, db), mm(so_ref, sb) + sgn_j * mid

    @pl.when(pl.program_id(1) == 0)
    def _():
        window = jnp.exp(-t_ref[...] * dl_ref[...])
        row0 = lax.broadcasted_iota(I32, (seq, 1), 0) == 0
        for o in range(2):
            fwd = jnp.dot(a_ref[...], wf_ref[2 * o], precision=HIGHEST, preferred_element_type=F32) * window
            bwd = jnp.dot(a_ref[...], wf_ref[2 * o + 1], precision=HIGHEST, preferred_element_type=F32) * window
            bwd = jnp.where(row0, 0.0, bwd)
            even = fwd + bwd
            hce_s[o], _, hco_s[o], _ = forward(even)
            _, hse_s[o], _, hso_s[o] = forward(fwd - bwd)
            hn_s[o] = jnp.sum(even * sign, axis=0, keepdims=True)

    def conv_to(dst_ref, z_ref, cw_ref, cb_ref):
        for lo in range(0, z_ref.shape[1], LANES):
            ls = slice(lo, lo + LANES)
            dst_ref[:, ls] = _dwconv(z_ref[:, ls], cw_ref[:, ls], cb_ref[:, ls], GRID_W, False)

    conv_to(v_s, zv_ref, cwv_ref, cbv_ref)

    for o, (zg_ref, cwg_ref, cbg_ref) in enumerate(((zx1_ref, cw1_ref, cb1_ref), (zx2_ref, cw2_ref, cb2_ref))):
        v = v_s[...]
        nyq = jnp.sum(v * sign, axis=0, keepdims=True) * hn_s[o] * (1.0 / n_fft)
        xce, xse, xco, xso = forward(v)
        hce, hse, hco, hso = hce_s[o], hse_s[o], hco_s[o], hso_s[o]
        scale_e = jnp.where(first, 1.0 / n_fft, 2.0 / n_fft)
        zce = scale_e * (xce * hce - xse * hse)
        zse = scale_e * (xce * hse + xse * hce)
        zco = (2.0 / n_fft) * (xco * hco - xso * hso)
        zso = (2.0 / n_fft) * (xco * hso + xso * hco)
        y_mid = jnp.sum(sgn_j * (zce + zso), axis=0, keepdims=True)
        sym = mm(ce_ref, zce.astype(BF16)) + mm(sot_ref, zso.astype(BF16))
        asym = mm(cot_ref, zco.astype(BF16)) + mm(se_ref, zse.astype(BF16))
        rolled = pltpu.roll(reverse_rows(sym - asym), 1, 0)
        y = jnp.concatenate([sym + asym, jnp.where(first, y_mid, rolled)], axis=0)
        conv_to(g_s, zg_ref, cwg_ref, cbg_ref)
        v_s[...] = g_s[...] * (y + sign * nyq + v * skip_ref[o:o + 1, :])

    o_ref[...] = v_s[...].astype(BF16)


def _hyena(z, conv_w9, conv_b, a, w_fout4, deltas, tcol, skip, tables, bsz, seq):
    ct = MXU
    n_ct = HYENA_WIDTH // ct
    hy = Z_HY // ct
    half = seq // 2
    zspec = lambda off: pl.BlockSpec((seq, ct), lambda j, b: (b, hy + off * n_ct + j))
    wspec = lambda off: pl.BlockSpec((9, ct), lambda j, b: (0, off * n_ct + j))
    bspec = lambda off: pl.BlockSpec((1, ct), lambda j, b: (0, off * n_ct + j))
    fh = a.shape[1]
    return pl.pallas_call(
        functools.partial(_hyena_body, seq=seq),
        grid=(n_ct, bsz),
        in_specs=[zspec(0), zspec(1), zspec(2), wspec(0), wspec(1), wspec(2), bspec(0), bspec(1), bspec(2),
                  _const_spec((seq, fh)),
                  pl.BlockSpec((4, fh, ct), lambda j, b: (0, 0, j)),
                  pl.BlockSpec((1, ct), lambda j, b: (0, j)),
                  _const_spec((seq, 1)),
                  pl.BlockSpec((2, ct), lambda j, b: (0, j))] + [_const_spec((half, half))] * len(tables),
        out_specs=pl.BlockSpec((seq, ct), lambda j, b: (b, j)),
        out_shape=jax.ShapeDtypeStruct((bsz * seq, HYENA_WIDTH), BF16),
        scratch_shapes=[pltpu.VMEM((2, half, ct), F32)] * 4 + [pltpu.VMEM((2, 1, ct), F32),
                                                               pltpu.VMEM((seq, ct), F32), pltpu.VMEM((seq, ct), F32)],
        compiler_params=_cparams(("arbitrary", "arbitrary"), 60),
        name="hyena",
    )(z, z, z, conv_w9, conv_w9, conv_w9, conv_b, conv_b, conv_b, a, w_fout4, deltas, tcol, skip, *tables)


def _mix_body(oh_ref, hh_ref, ga_ref, gh_ref, wa_ref, wh_ref, m_ref):
    half = m_ref.shape[1] // 2
    for cols in (slice(0, half), slice(half, 2 * half)):
        y_a = jnp.dot(oh_ref[...], wa_ref[:, cols], preferred_element_type=F32)
        y_h = jnp.dot(hh_ref[...], wh_ref[:, cols], preferred_element_type=F32)
        m_ref[:, cols] = (ga_ref[:, cols].astype(F32) * y_a + gh_ref[:, cols].astype(F32) * y_h).astype(BF16)


def _mix(oh, hh, sg, w_a, w_h):
    t = oh.shape[0]
    dm = w_a.shape[1]
    tm = 1024
    row = lambda i: (i, 0)
    return pl.pallas_call(
        _mix_body,
        grid=(t // tm,),
        in_specs=[pl.BlockSpec((tm, MLSTM_WIDTH), row), pl.BlockSpec((tm, HYENA_WIDTH), row),
                  pl.BlockSpec((tm, dm), row), pl.BlockSpec((tm, dm), lambda i: (i, 1)),
                  _const_spec((MLSTM_WIDTH, dm)), _const_spec((HYENA_WIDTH, dm))],
        out_specs=pl.BlockSpec((tm, dm), row),
        out_shape=jax.ShapeDtypeStruct((t, dm), BF16),
        compiler_params=_cparams(("parallel",), 60),
        name="mix",
    )(oh, hh, sg, sg, w_a, w_h)


def _merge_body(m_ref, x_ref, lng_ref, lnb_ref, g1_ref, l1g_ref, l1b_ref, sc2_ref, sh2_ref, wo_ref, wr_ref, br_ref,
                x1_ref, lg_ref, mo_s, *, tm):
    mo_s[...] = jnp.dot(m_ref[...], wo_ref[...], preferred_element_type=F32)
    wr = wr_ref[...]
    wr_hi = wr.astype(BF16)
    wr_lo = (wr - wr_hi.astype(F32)).astype(BF16)

    def rows(r, carry):
        sl = pl.ds(pl.multiple_of(r * 128, 128), 128)
        x0 = _layer_norm(x_ref[sl, :], lng_ref[...], lnb_ref[...])
        x1 = _layer_norm(DEEPNORM_ALPHA * x0 + g1_ref[0] * mo_s[sl, :], l1g_ref[...], l1b_ref[...])
        x1_ref[sl, :] = x1
        tok = x1 * (1.0 + sc2_ref[0]) + sh2_ref[0]
        t_hi = tok.astype(BF16)
        t_lo = (tok - t_hi.astype(F32)).astype(BF16)
        lg_ref[sl, :] = (jnp.dot(t_hi, wr_hi, preferred_element_type=F32)
                         + jnp.dot(t_lo, wr_hi, preferred_element_type=F32)
                         + jnp.dot(t_hi, wr_lo, preferred_element_type=F32) + br_ref[...])
        return carry
    lax.fori_loop(0, tm // 128, rows, 0)


def _merge(mix, x2d, ln_g, ln_b, g1, ln1_g, ln1_b, sc2, sh2, w_o, w_r, b_r, seq):
    t, dm = x2d.shape
    tm = 512
    per_b = seq // tm
    row = lambda i: (i, 0)
    mod = lambda i: (i // per_b, 0, 0)
    return pl.pallas_call(
        functools.partial(_merge_body, tm=tm),
        grid=(t // tm,),
        scratch_shapes=[pltpu.VMEM((tm, dm), F32)],
        in_specs=[pl.BlockSpec((tm, dm), row), pl.BlockSpec((tm, dm), row),
                  _const_spec((1, dm)), _const_spec((1, dm)),
                  pl.BlockSpec((1, 1, dm), mod),
                  _const_spec((1, dm)), _const_spec((1, dm)),
                  pl.BlockSpec((1, 1, dm), mod), pl.BlockSpec((1, 1, dm), mod),
                  _const_spec((dm, dm)), _const_spec((dm, LANES)), _const_spec((1, LANES))],
        out_specs=[pl.BlockSpec((tm, dm), row), pl.BlockSpec((tm, LANES), row)],
        out_shape=[jax.ShapeDtypeStruct((t, dm), F32), jax.ShapeDtypeStruct((t, LANES), F32)],
        compiler_params=_cparams(("parallel",), 56),
        name="merge",
    )(mix, x2d, ln_g, ln_b, g1, ln1_g, ln1_b, sc2, sh2, w_o, w_r, b_r)


def _route_body(lg_ref, w_ref, d_ref, cnt_ref, run_s, tot_s, *, tr):
    phase = pl.program_id(0)

    @pl.when(pl.program_id(1) == 0)
    def _():
        @pl.when(phase == 1)
        def _():
            tot_s[...] = run_s[...]
        run_s[...] = jnp.zeros_like(run_s)

    lane = lax.broadcasted_iota(I32, (tr, LANES), 1)
    lane_f = lane.astype(F32)
    logit = lg_ref[...]
    hot, val = [], []
    for _ in range(TOP_K):
        mk = jnp.max(logit, axis=-1, keepdims=True)
        ik = jnp.min(jnp.where(logit == mk, lane_f, float(LANES)), axis=-1, keepdims=True)
        hk = lane_f == ik
        logit = jnp.where(hk, -jnp.inf, logit)
        hot.append(hk)
        val.append(mk)
    cnt = jnp.zeros((tr, LANES), F32)
    for hk in hot:
        cnt = cnt + jnp.where(hk, 1.0, 0.0)
    run = run_s[...] + jnp.sum(cnt, axis=0, keepdims=True)

    @pl.when(phase == 0)
    def _():
        cnt_ref[...] = run.astype(I32)

    @pl.when(phase == 1)
    def _():
        total = tot_s[...]
        padded = jnp.floor((total + (MOE_BLOCK - 1.0)) * (1.0 / MOE_BLOCK)) * MOE_BLOCK
        lane8 = lax.broadcasted_iota(I32, (8, LANES), 1)
        incl = padded
        sft = 1
        while sft < LANES:
            incl = incl + jnp.where(lane8 >= sft, pltpu.roll(incl, sft, 1), 0.0)
            sft *= 2
        pstart = (incl - padded)[0:1, :]
        lower = (lax.broadcasted_iota(I32, (tr, tr), 0) > lax.broadcasted_iota(I32, (tr, tr), 1))
        before = jnp.dot(jnp.where(lower, 1.0, 0.0).astype(BF16), cnt.astype(BF16),
                         preferred_element_type=F32) + (run_s[0:1, :] + pstart)
        ex = [jnp.exp(v - val[0]) for v in val]
        denom = ex[0] + ex[1] + ex[2] + ex[3]
        w_out = jnp.zeros((tr, LANES), F32)
        d_out = jnp.zeros((tr, LANES), I32)
        for k in range(TOP_K):
            dest = jnp.sum(jnp.where(hot[k], before, 0.0), axis=-1, keepdims=True)
            w_out = jnp.where(lane == k, ex[k] / denom, w_out)
            d_out = jnp.where(lane == k, dest.astype(I32), d_out)
        w_ref[...] = w_out
        d_ref[...] = d_out[:, :TOP_K]
        cnt_ref[...] = total.astype(I32)

    run_s[...] = run


def _route(logits):
    t = logits.shape[0]
    tr = 512
    row = lambda p, i: (i, 0)
    out_row = lambda p, i: (i * p, 0)
    return pl.pallas_call(
        functools.partial(_route_body, tr=tr),
        grid=(2, t // tr),
        in_specs=[pl.BlockSpec((tr, LANES), row)],
        out_specs=[pl.BlockSpec((tr, LANES), out_row), pl.BlockSpec((tr, TOP_K), out_row),
                   pl.BlockSpec((8, LANES), lambda p, i: (0, 0))],
        out_shape=[jax.ShapeDtypeStruct((t, LANES), F32), jax.ShapeDtypeStruct((t, TOP_K), I32),
                   jax.ShapeDtypeStruct((8, LANES), I32)],
        scratch_shapes=[pltpu.VMEM((8, LANES), F32), pltpu.VMEM((8, LANES), F32)],
        compiler_params=_cparams(("arbitrary", "arbitrary"), 32),
        name="route",
    )(logits)


def _scatter_body(cnt_ref, pstart_ref, used_ref, dest_ref, x1_ref, sc_ref, sh_ref, xb_ref, tok_ref, zero_s, sem, pad_sem,
                  *, ts, n_blocks):
    def row_copy(src, r_src, r_dst, s):
        return pltpu.make_async_copy(src.at[pl.ds(r_src, 1)], xb_ref.at[pl.ds(r_dst, 1)], s)

    def block_copy(blk):
        return pltpu.make_async_copy(zero_s, xb_ref.at[pl.ds(pl.multiple_of(blk * MOE_BLOCK, MOE_BLOCK), MOE_BLOCK)],
                                     pad_sem)

    @pl.when(pl.program_id(0) == 0)
    def _():
        zero_s[...] = jnp.zeros_like(zero_s)

        def last_block(e):
            return pstart_ref[e] // MOE_BLOCK + cnt_ref[e] // MOE_BLOCK

        def pad_start(e, c):
            @pl.when(cnt_ref[e] % MOE_BLOCK != 0)
            def _():
                block_copy(last_block(e)).start()
            return c

        def pad_wait(e, c):
            @pl.when(cnt_ref[e] % MOE_BLOCK != 0)
            def _():
                block_copy(last_block(e)).wait()
            return c
        lax.fori_loop(0, N_EXPERTS, pad_start, 0)
        lax.fori_loop(0, N_EXPERTS, pad_wait, 0)

        def tail_start(blk, c):
            block_copy(blk).start()
            return c

        def tail_wait(blk, c):
            block_copy(blk).wait()
            return c
        lax.fori_loop(used_ref[0], n_blocks, tail_start, 0)
        lax.fori_loop(used_ref[0], n_blocks, tail_wait, 0)

    tok_ref[...] = x1_ref[...] * (1.0 + sc_ref[0]) + sh_ref[0]

    def row(r, carry):
        for k in range(TOP_K):
            row_copy(tok_ref, r, dest_ref[r * TOP_K + k], sem).start()
        return carry
    lax.fori_loop(0, ts, row, 0, unroll=8)
    for _ in range(TOP_K):
        pltpu.make_async_copy(tok_ref, xb_ref.at[pl.ds(0, ts)], sem).wait()


def _scatter(x1, sc2, sh2, dest_flat, counts, pstart, used, n_blocks, seq):
    t, dm = x1.shape
    ts = 256
    per_b = seq // ts
    grid_spec = pltpu.PrefetchScalarGridSpec(
        num_scalar_prefetch=3,
        grid=(t // ts,),
        in_specs=[pl.BlockSpec((ts * TOP_K,), lambda i, *_: (i,), memory_space=pltpu.SMEM),
                  pl.BlockSpec((ts, dm), lambda i, *_: (i, 0)),
                  pl.BlockSpec((1, 1, dm), lambda i, *_: (i // per_b, 0, 0)),
                  pl.BlockSpec((1, 1, dm), lambda i, *_: (i // per_b, 0, 0))],
        out_specs=pl.BlockSpec(memory_space=pl.ANY),
        scratch_shapes=[pltpu.VMEM((ts, dm), F32), pltpu.VMEM((MOE_BLOCK, dm), F32),
                        pltpu.SemaphoreType.DMA(()), pltpu.SemaphoreType.DMA(())],
    )
    return pl.pallas_call(
        functools.partial(_scatter_body, ts=ts, n_blocks=n_blocks),
        grid_spec=grid_spec,
        out_shape=jax.ShapeDtypeStruct((n_blocks * MOE_BLOCK, dm), F32),
        compiler_params=_cparams(("arbitrary",), 32),
        name="scatter",
    )(counts, pstart, used, dest_flat, x1, sc2, sh2)


def _ffn1_body(e_ref, c_ref, blk_ref, oblk_ref, oc_ref, first_ref, n_ref,
               x_ref, wg_ref, wu_ref, bg_ref, bu_ref, a_ref, wg_s, wu_s):
    s = pl.program_id(0)

    @pl.when(first_ref[s] == 1)
    def _():
        wg_s[...] = wg_ref[0].astype(BF16)
        wu_s[...] = wu_ref[0].astype(BF16)

    @pl.when(s < n_ref[0])
    def _():
        x = x_ref[...].astype(BF16)
        g = jnp.minimum(jnp.dot(x, wg_s[...], preferred_element_type=F32) + bg_ref[0], SWIGLU_LIMIT)
        u = jnp.clip(jnp.dot(x, wu_s[...], preferred_element_type=F32) + bu_ref[0], -SWIGLU_LIMIT, SWIGLU_LIMIT)
        a_ref[...] = (g * _sigmoid(SWIGLU_ALPHA * g) * (u + 1.0)).astype(BF16)

    @pl.when(s >= n_ref[0])
    def _():
        a_ref[...] = jnp.zeros_like(a_ref)


def _ffn2_body(e_ref, c_ref, blk_ref, oblk_ref, oc_ref, first_ref, n_ref, a_ref, wd_ref, bd_ref, y_ref, wd_s):
    s = pl.program_id(0)

    @pl.when(first_ref[s] == 1)
    def _():
        wd_s[...] = wd_ref[0].astype(BF16)

    @pl.when(s < n_ref[0])
    def _():
        y_ref[...] = jnp.dot(a_ref[...], wd_s[...], preferred_element_type=F32) + bd_ref[0]

    @pl.when(s >= n_ref[0])
    def _():
        y_ref[...] = jnp.zeros_like(y_ref)


def _ffn1(plan, xb, w_gate, w_up, b_gate, b_up, n_steps):
    rows, dm = xb.shape
    de = w_gate.shape[2]
    wspec = pl.BlockSpec((1, dm, FF_CHUNK), lambda s, e, c, *_: (e[s], 0, c[s]))
    bspec = pl.BlockSpec((1, 1, FF_CHUNK), lambda s, e, c, *_: (e[s], 0, c[s]))
    grid_spec = pltpu.PrefetchScalarGridSpec(
        num_scalar_prefetch=7,
        grid=(n_steps,),
        in_specs=[pl.BlockSpec((MOE_BLOCK, dm), lambda s, e, c, blk, *_: (blk[s], 0)), wspec, wspec, bspec, bspec],
        out_specs=pl.BlockSpec((MOE_BLOCK, FF_CHUNK), lambda s, e, c, blk, oblk, oc, *_: (oblk[s], oc[s])),
        scratch_shapes=[pltpu.VMEM((dm, FF_CHUNK), BF16), pltpu.VMEM((dm, FF_CHUNK), BF16)],
    )
    return pl.pallas_call(
        _ffn1_body, grid_spec=grid_spec,
        out_shape=jax.ShapeDtypeStruct((rows, de), BF16),
        compiler_params=_cparams(("arbitrary",), 60),
        name="ffn1",
    )(*plan, xb, w_gate, w_up, b_gate, b_up)


def _ffn2(plan, act, w_down, b_down, n_steps, chunk):
    rows, de = act.shape
    dm = w_down.shape[2]
    grid_spec = pltpu.PrefetchScalarGridSpec(
        num_scalar_prefetch=7,
        grid=(n_steps,),
        in_specs=[pl.BlockSpec((MOE_BLOCK, de), lambda s, e, c, blk, *_: (blk[s], 0)),
                  pl.BlockSpec((1, de, chunk), lambda s, e, c, *_: (e[s], 0, c[s])),
                  pl.BlockSpec((1, 1, chunk), lambda s, e, c, *_: (e[s], 0, c[s]))],
        out_specs=pl.BlockSpec((MOE_BLOCK, chunk), lambda s, e, c, blk, oblk, oc, *_: (oblk[s], oc[s])),
        scratch_shapes=[pltpu.VMEM((de, chunk), BF16)],
    )
    return pl.pallas_call(
        _ffn2_body, grid_spec=grid_spec,
        out_shape=jax.ShapeDtypeStruct((rows, dm), F32),
        compiler_params=_cparams(("arbitrary",), 60),
        name="ffn2",
    )(*plan, act, w_down, b_down)


def _combine_body(dcur_ref, dnxt_ref, w_ref, x1_ref, g2_ref, lg_ref, lb_ref, yb_ref, o_ref, buf, sem, *, tc, n_tiles):
    i = pl.program_id(0)

    def issue(d_ref, slot):
        def row(r, carry):
            for k in range(TOP_K):
                pltpu.make_async_copy(yb_ref.at[pl.ds(d_ref[r * TOP_K + k], 1)],
                                      buf.at[slot, k, pl.ds(r, 1)], sem.at[slot]).start()
            return carry
        lax.fori_loop(0, tc, row, 0, unroll=8)

    @pl.when(i == 0)
    def _():
        issue(dcur_ref, 0)

    @pl.when(i + 1 < n_tiles)
    def _():
        issue(dnxt_ref, (i + 1) % 2)

    slot = i % 2
    for k in range(TOP_K):
        pltpu.make_async_copy(yb_ref.at[pl.ds(0, tc)], buf.at[slot, k], sem.at[slot]).wait()
    w = w_ref[...]
    y = w[:, 0:1] * buf[slot, 0]
    for k in range(1, TOP_K):
        y = y + w[:, k:k + 1] * buf[slot, k]
    o_ref[...] = _layer_norm(DEEPNORM_ALPHA * x1_ref[...] + g2_ref[0] * y, lg_ref[...], lb_ref[...])


def _combine(dest_flat, w4, x1, g2, ln_g, ln_b, yb, seq):
    t, dm = x1.shape
    tc = 256
    n_tiles = t // tc
    per_b = seq // tc
    row = lambda i: (i, 0)
    return pl.pallas_call(
        functools.partial(_combine_body, tc=tc, n_tiles=n_tiles),
        grid=(n_tiles,),
        in_specs=[pl.BlockSpec((tc * TOP_K,), lambda i: (i,), memory_space=pltpu.SMEM),
                  pl.BlockSpec((tc * TOP_K,), lambda i: (jnp.minimum(i + 1, n_tiles - 1),), memory_space=pltpu.SMEM),
                  pl.BlockSpec((tc, LANES), row),
                  pl.BlockSpec((tc, dm), row),
                  pl.BlockSpec((1, 1, dm), lambda i: (i // per_b, 0, 0)),
                  _const_spec((1, dm)), _const_spec((1, dm)),
                  pl.BlockSpec(memory_space=pl.ANY)],
        out_specs=pl.BlockSpec((tc, dm), row),
        out_shape=jax.ShapeDtypeStruct((t, dm), F32),
        scratch_shapes=[pltpu.VMEM((2, TOP_K, tc, dm), F32), pltpu.SemaphoreType.DMA((2,))],
        compiler_params=_cparams(("arbitrary",), 40),
        name="combine",
    )(dest_flat, dest_flat, w4, x1, g2, ln_g, ln_b, yb)


def _dft_tables(seq):
    n = 2 * seq
    j = np.arange(seq // 2, dtype=np.int64)
    tables = []
    for k in (2 * j, 2 * j + 1):
        ang = (2.0 * np.pi / n) * ((k[:, None] * j[None, :]) % n).astype(np.float64)
        tables += [np.cos(ang), np.sin(ang)]
    tables += [tables[2].T, tables[3].T]
    return tuple(jnp.asarray(m, F32).astype(BF16) for m in tables)


def _filter_features(seq):
    t = jnp.linspace(0.0, 1.0, seq, dtype=F32)[:, None]
    bands = (HYENA_EMB - 1) // 2
    f = jnp.linspace(1e-4, bands - 1, bands, dtype=F32)[None, :]
    ang = 2.0 * math.pi * jnp.arange(seq, dtype=F32)[:, None] * f / seq
    feats = jnp.concatenate([t, jnp.cos(ang), -jnp.sin(ang)], axis=-1)
    max_decay = math.log(HYENA_DECAY_TARGET) / HYENA_SHORT_DECAY_PCT
    min_decay = math.log(HYENA_DECAY_TARGET) / HYENA_LONG_DECAY_PCT
    deltas = jnp.abs(jnp.linspace(min_decay, max_decay, HYENA_WIDTH, dtype=F32))[None, :]
    return t, feats, deltas


def _moe_plan(counts, n_blocks, n_chunks):
    ids = jnp.arange(N_EXPERTS, dtype=I32)
    nblk = (counts + MOE_BLOCK - 1) // MOE_BLOCK
    blk_end = jnp.cumsum(nblk)
    blk_start = blk_end - nblk
    used = blk_end[-1]
    steps = n_chunks * nblk
    step_end = jnp.cumsum(steps)
    n_used = step_end[-1]
    s_all = jnp.arange(n_chunks * n_blocks, dtype=I32)
    s = jnp.minimum(s_all, n_used - 1)
    e_s = jnp.minimum(jnp.sum((s[:, None] >= step_end[None, :]).astype(I32), axis=1), N_EXPERTS - 1)
    onehot = e_s[:, None] == ids[None, :]
    pick = lambda table: jnp.sum(jnp.where(onehot, table[None, :], 0), axis=1)
    loc = s - pick(step_end - steps)
    nb = jnp.maximum(pick(nblk), 1)
    c_s = loc // nb
    r_s = loc % nb
    blk = pick(blk_start) + r_s
    tail = s_all >= n_used
    j = jnp.maximum(s_all - n_used, 0)
    n_tail = jnp.maximum(n_blocks - used, 1)
    oblk = jnp.where(tail, used + j % n_tail, blk)
    oc = jnp.where(tail, j // n_tail, c_s)
    first = jnp.logical_and(r_s == 0, jnp.logical_not(tail))
    as_i32 = lambda v: v.astype(I32)
    plan = tuple(map(as_i32, (e_s, c_s, blk, oblk, oc, first, n_used.reshape(1))))
    return as_i32(blk_start * MOE_BLOCK), as_i32(used.reshape(1)), plan


def kernel(x, c, ctx, c_ctx, ln_in_g, ln_in_b, w_mod, b_mod, w_in, b_in, mlstm_conv_w, mlstm_conv_b,
           w_qh, w_kh, hyena_conv_w, hyena_conv_b, filt_w1, filt_b1, filt_wh, filt_bh, filt_freq, filt_wout,
           hyena_skip, w_proj_a, w_proj_h, w_out, ln1_g, ln1_b, w_router, b_router, w_gate, b_gate,
           w_up, b_up, w_down, b_down, ln2_g, ln2_b):
    bsz, seq, dm = x.shape
    ctx_len = ctx.shape[1]
    t = bsz * seq
    assert w_mod.shape[0] == DEPTH and dm == D_MODEL and ctx_len == CHUNK and bsz + 1 <= 16
    row = lambda v: v.reshape(1, -1)

    cond = jnp.concatenate([c, c_ctx[None], jnp.zeros((16 - bsz - 1, dm), F32)], axis=0)
    mod = _mod(cond, w_mod[0], row(b_mod[0]))
    sh1, sc1, g1, sh2, sc2, g2 = [m[:, None, :] for m in jnp.split(mod, 6, axis=-1)]

    w_main = jnp.concatenate([w_in[0][:, :IN_GATES].astype(BF16), w_in[0][:, IN_O:].astype(BF16)], axis=1)
    b_main = row(jnp.concatenate([b_in[0][:IN_GATES], b_in[0][IN_O:]]))
    w_g = jnp.pad(w_in[0][:, IN_GATES:IN_O], ((0, 0), (0, LANES - 4 * HEADS))).astype(BF16)
    b_g = row(jnp.pad(b_in[0][IN_GATES:IN_O], (0, LANES - 4 * HEADS)))
    lng, lnb = row(ln_in_g), row(ln_in_b)
    x2d = x.reshape(t, dm)
    z, gates, sg = _in_proj(x2d, lng, lnb, sc1[:bsz], sh1[:bsz], w_main, b_main, w_g, b_g, seq, Z_BG)
    zc, gates_c = _in_proj(ctx.reshape(bsz * ctx_len, dm), lng, lnb, sc1[bsz:bsz + 1], sh1[bsz:bsz + 1],
                           w_main[:, :IN_GATES], b_main[:, :IN_GATES], w_g, b_g, bsz * ctx_len, IN_GATES)

    g_all = jnp.concatenate([gates_c[:, :4 * HEADS].reshape(bsz, ctx_len, 4, HEADS),
                             gates[:, :4 * HEADS].reshape(bsz, seq, 4, HEADS)], axis=1)
    gates_t = g_all.transpose(0, 3, 2, 1)

    oh = _mlstm(z, zc, gates_t, mlstm_conv_w[0].reshape(9, MLSTM_WIDTH), row(mlstm_conv_b[0]),
                w_qh[0], w_kh[0].transpose(0, 2, 1), bsz, seq, ctx_len)

    tcol, feats, deltas = _filter_features(seq)
    feats = jnp.pad(feats, ((0, 0), (0, LANES - HYENA_EMB)))
    w1 = jnp.pad(filt_w1[0], ((0, LANES - HYENA_EMB), (0, 0)))
    a = _filt(feats, w1, row(filt_b1[0]), filt_wh[0], filt_bh[0], filt_freq[0])
    fh = a.shape[1]
    w_fout4 = filt_wout[0].reshape(fh, 4, HYENA_WIDTH).transpose(1, 0, 2)
    hh = _hyena(z, hyena_conv_w[0].reshape(9, 3 * HYENA_WIDTH), row(hyena_conv_b[0]), a, w_fout4, deltas, tcol,
                hyena_skip[0], _dft_tables(seq), bsz, seq)

    w_r = jnp.pad(w_router[0], ((0, 0), (0, LANES - N_EXPERTS)))
    b_r = row(jnp.pad(b_router[0], (0, LANES - N_EXPERTS), constant_values=-1e30))
    mix = _mix(oh, hh, sg, w_proj_a[0].astype(BF16), w_proj_h[0].astype(BF16))
    x1, logits = _merge(mix, x2d, lng, lnb, g1[:bsz], row(ln1_g[0]), row(ln1_b[0]), sc2[:bsz], sh2[:bsz],
                        w_out[0].astype(BF16), w_r, b_r, seq)

    w4, dest, counts = _route(logits)
    n_blocks = -(-(t * TOP_K + N_EXPERTS * (MOE_BLOCK - 1)) // MOE_BLOCK)
    n_chunks = dm // FF_CHUNK
    counts = counts[0, :N_EXPERTS]
    pstart, used, plan = _moe_plan(counts, n_blocks, n_chunks)
    dest_flat = dest.reshape(t * TOP_K)
    xb = _scatter(x1, sc2[:bsz], sh2[:bsz], dest_flat, counts, pstart, used, n_blocks, seq)
    act = _ffn1(plan, xb, w_gate[0], w_up[0], b_gate[0][:, None, :], b_up[0][:, None, :], n_chunks * n_blocks)
    _, _, plan_down = _moe_plan(counts, n_blocks, 1)
    yb = _ffn2(plan_down, act, w_down[0], b_down[0][:, None, :], n_blocks, dm)
    out = _combine(dest_flat, w4, x1, g2[:bsz], row(ln2_g[0]), row(ln2_b[0]), yb, seq)
    return out.reshape(bsz, seq, dm)
```
